```python
import functools
import jax, jax.numpy as jnp
from jax import lax
import numpy as np

D_MODEL = 1024
BATCH = 4
SEQ = 4096
DEPTH = 4

GRID_W = 64
CTX_LEN = 256
H_RET = 4
DK_RET = 64
DV_RET = 128
H_ATT = 8
KV_ATT = 2
HD_ATT = 64
H_M = 4
DK_M = 64
DV_M = 128
CHUNK = 128
Q_BLOCK = 128
D_FF = 2816
CONV_W = 3
ROPE_THETA = 10000.0
NORM_EPS = 1e-6
SPLIT_SIZES = (H_RET * DK_RET, H_RET * DK_RET, H_RET * DV_RET, H_RET * DV_RET,
               H_ATT * HD_ATT, KV_ATT * HD_ATT, KV_ATT * HD_ATT,
               H_M * DK_M, H_M * DK_M, H_M * DV_M, H_M * DV_M, 2 * H_M, 2 * H_M,
               3 * D_MODEL)
N_IN = sum(SPLIT_SIZES)

kernel_name = "hybrid_ret_gqa_mlstm_convffn_dit"


def split_proj(p):
    idx, acc = [], 0
    for s in SPLIT_SIZES[:-1]:
        acc += s
        idx.append(acc)
    return jnp.split(p, idx, axis=-1)


def layer_norm(x, gain=None, bias=None):
    xf = x.astype(jnp.float32)
    mu = jnp.mean(xf, axis=-1, keepdims=True)
    var = jnp.mean(jnp.square(xf - mu), axis=-1, keepdims=True)
    y = (xf - mu) * lax.rsqrt(var + NORM_EPS)
    if gain is not None:
        y = y * gain.astype(jnp.float32) + bias.astype(jnp.float32)
    return y.astype(x.dtype)


def rms_norm(x, gain):
    xf = x.astype(jnp.float32)
    y = xf * lax.rsqrt(jnp.mean(jnp.square(xf), axis=-1, keepdims=True) + NORM_EPS)
    return (y * gain.astype(jnp.float32)).astype(x.dtype)


def head_norm(y, gain):
    b, s, h, d = y.shape
    yf = y.astype(jnp.float32)
    mu = jnp.mean(yf, axis=-1, keepdims=True)
    var = jnp.mean(jnp.square(yf - mu), axis=-1, keepdims=True)
    yn = ((yf - mu) * lax.rsqrt(var + NORM_EPS)).reshape(b, s, h * d)
    return (yn * gain.astype(jnp.float32)).astype(y.dtype)


def modulate(x, shift, scale):
    return layer_norm(x) * (1.0 + scale) + shift


def to_chunks(a):
    b, s = a.shape[:2]
    a = a.reshape((b, s // CHUNK, CHUNK) + a.shape[2:])
    return jnp.moveaxis(jnp.moveaxis(a, 1, 0), 2, 3)


def from_chunks(o):
    nc, b, h, l, d = o.shape
    return o.transpose(1, 0, 3, 2, 4).reshape(b, nc * l, h, d)


def retention_scan(log_gamma, args, state):
    q, k, v = (a.astype(jnp.float32) for a in args)
    pos = jnp.arange(CHUNK, dtype=jnp.float32)
    diff = pos[:, None] - pos[None, :]
    lower = diff >= 0
    intra = jnp.where(lower, jnp.exp(jnp.where(lower, diff, 0.0)[None] * log_gamma[:, None, None]), 0.0)
    q_decay = jnp.exp((pos + 1.0)[None, :] * log_gamma[:, None])
    k_decay = jnp.exp((CHUNK - 1.0 - pos)[None, :] * log_gamma[:, None])
    chunk_decay = jnp.exp(CHUNK * log_gamma)

    def step(s_state, blk):
        qc, kc, vc = blk
        scores = jnp.einsum('bhld,bhmd->bhlm', qc, kc) * intra
        out = (jnp.einsum('bhlm,bhme->bhle', scores, vc)
               + jnp.einsum('bhld,bhde->bhle', qc, s_state) * q_decay[None, :, :, None])
        s_state = (s_state * chunk_decay[None, :, None, None]
                   + jnp.einsum('bhld,bhle->bhde', kc * k_decay[None, :, :, None], vc))
        return s_state, out

    state, out = lax.scan(step, state, (to_chunks(q), to_chunks(k), to_chunks(v)))
    return from_chunks(out).astype(args[2].dtype), state


def mlstm_scan(args, state):
    q, k, v, li, lf = (a.astype(jnp.float32) for a in args)
    lower = jnp.tril(jnp.ones((CHUNK, CHUNK), dtype=bool))

    def step(carry, blk):
        c_mat, n_vec, m = carry
        qc, kc, vc, ic, fc = blk
        b = jnp.cumsum(fc, axis=-1)
        d_mat = jnp.where(lower, b[..., :, None] - b[..., None, :] + ic[..., None, :], -jnp.inf)
        inter = b + m[..., None]
        m_t = jnp.maximum(inter, jnp.max(d_mat, axis=-1))
        w = jnp.exp(d_mat - m_t[..., None])
        s_qk = jnp.einsum('bhld,bhmd->bhlm', qc, kc) * w
        a_inter = jnp.exp(inter - m_t)
        num = (jnp.einsum('bhlm,bhme->bhle', s_qk, vc)
               + a_inter[..., None] * jnp.einsum('bhld,bhde->bhle', qc, c_mat))
        den = jnp.sum(s_qk, axis=-1) + a_inter * jnp.einsum('bhld,bhd->bhl', qc, n_vec)
        h = num / jnp.maximum(jnp.abs(den), jnp.exp(-m_t))[..., None]
        b_end = b[..., -1]
        g = b_end[..., None] - b + ic
        m_new = jnp.maximum(b_end + m, jnp.max(g, axis=-1))
        wk = jnp.exp(g - m_new[..., None])
        carry_scale = jnp.exp(b_end + m - m_new)
        c_mat = carry_scale[..., None, None] * c_mat + jnp.einsum('bhld,bhle->bhde', kc * wk[..., None], vc)
        n_vec = carry_scale[..., None] * n_vec + jnp.einsum('bhl,bhld->bhd', wk, kc)
        return (c_mat, n_vec, m_new), h

    state, h = lax.scan(step, state, tuple(to_chunks(a) for a in (q, k, v, li, lf)))
    return from_chunks(h).astype(args[2].dtype), state


def bidirectional(scan_f, scan_b, ctx_f, lat_f, ctx_b, lat_b, init_state):
    flip = lambda args: tuple(jnp.flip(a, axis=1) for a in args)
    c_f, s_f = scan_f(ctx_f, init_state)
    l_f, _ = scan_f(lat_f, s_f)
    c_b, s_b = scan_b(flip(ctx_b), init_state)
    l_b, _ = scan_b(flip(lat_b), s_b)
    return c_f + jnp.flip(c_b, axis=1), l_f + jnp.flip(l_b, axis=1)


def retention_branch(pc, pl, decay_logit, gn_g, with_ctx):
    log_gamma = jax.nn.log_sigmoid(decay_logit.astype(jnp.float32))

    def heads(p):
        b, s = p[0].shape[:2]
        q = p[0].reshape(b, s, H_RET, DK_RET)
        k = p[1].reshape(b, s, H_RET, DK_RET) * (DK_RET ** -0.5)
        v = p[2].reshape(b, s, H_RET, DV_RET)
        return (q, k, v)

    args_c, args_l = heads(pc), heads(pl)
    init = jnp.zeros((pl[0].shape[0], H_RET, DK_RET, DV_RET), jnp.float32)
    o_c, o_l = bidirectional(functools.partial(retention_scan, log_gamma[0]),
                             functools.partial(retention_scan, log_gamma[1]),
                             args_c, args_l, args_c, args_l, init)
    readout = lambda o, p: head_norm(o, gn_g) * jax.nn.silu(p[3])
    return (readout(o_c, pc) if with_ctx else None), readout(o_l, pl)


def rope_half(x, ang):
    n = x.shape[-1] // 2
    cos = jnp.cos(ang)[None, :, None, :]
    sin = jnp.sin(ang)[None, :, None, :]
    x1, x2 = x[..., :n], x[..., n:]
    return jnp.concatenate([x1 * cos - x2 * sin, x2 * cos + x1 * sin], axis=-1).astype(x.dtype)


def axial_rope(x, ang_row, ang_col):
    half = x.shape[-1] // 2
    return jnp.concatenate([rope_half(x[..., :half], ang_row), rope_half(x[..., half:], ang_col)], axis=-1)


def attention_branch(pc, pl, qn_g, kn_g, ang_row, ang_col, with_ctx):
    scale = HD_ATT ** -0.5
    rep = H_ATT // KV_ATT

    def heads(p):
        b, s = p[4].shape[:2]
        q = rms_norm(p[4].reshape(b, s, H_ATT, HD_ATT), qn_g)
        k = rms_norm(p[5].reshape(b, s, KV_ATT, HD_ATT), kn_g)
        v = p[6].reshape(b, s, KV_ATT, HD_ATT)
        return q, k, v

    q_c, k_c, v_c = heads(pc)
    q_l, k_l, v_l = heads(pl)
    q_l = axial_rope(q_l, ang_row, ang_col)
    k_l = axial_rope(k_l, ang_row, ang_col)
    k_all = jnp.concatenate([k_c, k_l], axis=1)
    v_all = jnp.concatenate([v_c, v_l], axis=1)

    def attend(q, k, v):
        b, nq = q.shape[:2]
        qg = q.reshape(b, nq, KV_ATT, rep, HD_ATT)
        s = jnp.einsum('bqgrd,bkgd->bgrqk', qg, k).astype(jnp.float32) * scale
        p = jax.nn.softmax(s, axis=-1).astype(v.dtype)
        return jnp.einsum('bgrqk,bkgd->bqgrd', p, v).reshape(b, nq, H_ATT * HD_ATT)

    b, s = q_l.shape[:2]
    q_blocks = jnp.moveaxis(q_l.reshape(b, s // Q_BLOCK, Q_BLOCK, H_ATT, HD_ATT), 1, 0)
    y_l = jnp.moveaxis(lax.map(lambda qb: attend(qb, k_all, v_all), q_blocks), 0, 1)
    y_l = y_l.reshape(b, s, H_ATT * HD_ATT)
    y_c = attend(q_c, k_c, v_c) if with_ctx else None
    return y_c, y_l


def mlstm_branch(pc, pl, gn_g, with_ctx):
    def heads(p, d):
        b, s = p[7].shape[:2]
        q = p[7].reshape(b, s, H_M, DK_M)
        k = p[8].reshape(b, s, H_M, DK_M) * (DK_M ** -0.5)
        v = p[9].reshape(b, s, H_M, DV_M)
        li = p[11][..., d * H_M:(d + 1) * H_M].astype(jnp.float32)
        lf = jax.nn.log_sigmoid(p[12][..., d * H_M:(d + 1) * H_M].astype(jnp.float32))
        return (q, k, v, li, lf)

    bsz = pl[7].shape[0]
    init = (jnp.zeros((bsz, H_M, DK_M, DV_M), jnp.float32),
            jnp.zeros((bsz, H_M, DK_M), jnp.float32),
            jnp.zeros((bsz, H_M), jnp.float32))
    o_c, o_l = bidirectional(mlstm_scan, mlstm_scan, heads(pc, 0), heads(pl, 0), heads(pc, 1), heads(pl, 1), init)
    readout = lambda o, p: jax.nn.sigmoid(p[10]) * head_norm(o, gn_g)
    return (readout(o_c, pc) if with_ctx else None), readout(o_l, pl)


def gated_merge(p_gate, y_ret, y_att, y_m, w_r, w_a, w_m, w_o):
    g_r, g_a, g_m = jnp.split(p_gate, 3, axis=-1)
    z = (jax.nn.sigmoid(g_r) * (y_ret @ w_r) + jax.nn.sigmoid(g_a) * (y_att @ w_a)
         + jax.nn.sigmoid(g_m) * (y_m @ w_m))
    return z @ w_o


def conv_ffn(h, w_up, conv_w, conv_b, w_down):
    u = h @ w_up
    u = lax.conv_general_dilated(u, conv_w[:, None, :].astype(u.dtype), window_strides=(1,),
                                 padding=((CONV_W // 2, CONV_W // 2),),
                                 dimension_numbers=('NWC', 'WIO', 'NWC'),
                                 feature_group_count=u.shape[-1]) + conv_b
    a, g = jnp.split(u, 2, axis=-1)
    return (jax.nn.silu(g) * a) @ w_down


def setup_inputs(seed: int = 0) -> dict:
    key = jax.random.key(seed)
    ks = jax.random.split(key, 32)
    f32 = jnp.float32
    beta = (8.0 * DEPTH) ** -0.25
    nrm = lambda k, shape, sc: jax.random.normal(k, shape, f32) * sc
    f_off = sum(SPLIT_SIZES[:12])
    b_in = nrm(ks[7], (DEPTH, N_IN), 0.02).at[:, f_off:f_off + 2 * H_M].add(
        jnp.tile(jnp.linspace(3.0, 6.0, H_M, dtype=f32), 2))
    gamma = 1.0 - 2.0 ** (-5.0 - jnp.arange(H_RET, dtype=f32))
    ret_decay_logit = jnp.log(gamma / (1.0 - gamma))[None, None, :] + nrm(ks[8], (DEPTH, 2, H_RET), 0.1)
    return {
        "x": nrm(ks[0], (BATCH, SEQ, D_MODEL), 1.0),
        "c": nrm(ks[1], (BATCH, D_MODEL), 1.0),
        "ctx": nrm(ks[2], (BATCH, CTX_LEN, D_MODEL), 1.0),
        "c_ctx": nrm(ks[3], (D_MODEL,), 1.0),
        "w_mod": nrm(ks[4], (DEPTH, D_MODEL, 6 * D_MODEL), D_MODEL ** -0.5),
        "b_mod": nrm(ks[5], (DEPTH, 6 * D_MODEL), 0.01),
        "w_in": nrm(ks[6], (DEPTH, D_MODEL, N_IN), D_MODEL ** -0.5),
        "b_in": b_in,
        "ret_decay_logit": ret_decay_logit,
        "ret_gn_g": 1.0 + nrm(ks[9], (DEPTH, H_RET * DV_RET), 0.02),
        "attn_qn_g": 1.0 + nrm(ks[10], (DEPTH, HD_ATT), 0.02),
        "attn_kn_g": 1.0 + nrm(ks[11], (DEPTH, HD_ATT), 0.02),
        "mlstm_gn_g": 1.0 + nrm(ks[12], (DEPTH, H_M * DV_M), 0.02),
        "w_br_ret": nrm(ks[13], (DEPTH, H_RET * DV_RET, D_MODEL), (H_RET * DV_RET) ** -0.5 * beta),
        "w_br_att": nrm(ks[14], (DEPTH, H_ATT * HD_ATT, D_MODEL), (H_ATT * HD_ATT) ** -0.5 * beta),
        "w_br_mlstm": nrm(ks[15], (DEPTH, H_M * DV_M, D_MODEL), (H_M * DV_M) ** -0.5 * beta),
        "w_out": nrm(ks[16], (DEPTH, D_MODEL, D_MODEL), D_MODEL ** -0.5 * beta),
        "ln1_g": 1.0 + nrm(ks[17], (DEPTH, D_MODEL), 0.02),
        "ln1_b": nrm(ks[18], (DEPTH, D_MODEL), 0.02),
        "w_up": nrm(ks[19], (DEPTH, D_MODEL, 2 * D_FF), D_MODEL ** -0.5),
        "conv_w": nrm(ks[20], (DEPTH, CONV_W, 2 * D_FF), CONV_W ** -0.5),
        "conv_b": nrm(ks[21], (DEPTH, 2 * D_FF), 0.02),
        "w_down": nrm(ks[22], (DEPTH, D_FF, D_MODEL), D_FF ** -0.5 * beta),
        "ln2_g": 1.0 + nrm(ks[23], (DEPTH, D_MODEL), 0.02),
        "ln2_b": nrm(ks[24], (DEPTH, D_MODEL), 0.02),
    }


def reference(x, c, ctx, c_ctx, w_mod, b_mod, w_in, b_in, ret_decay_logit, ret_gn_g, attn_qn_g, attn_kn_g,
              mlstm_gn_g, w_br_ret, w_br_att, w_br_mlstm, w_out, ln1_g, ln1_b, w_up, conv_w, conv_b, w_down,
              ln2_g, ln2_b):
    alpha = (2.0 * DEPTH) ** 0.25
    seq_lat = x.shape[1]
    rows = seq_lat // GRID_W
    row = jnp.repeat(jnp.arange(rows), GRID_W).astype(jnp.float32)
    col = jnp.tile(jnp.arange(GRID_W), rows).astype(jnp.float32)
    n_freq = HD_ATT // 4
    freqs = ROPE_THETA ** (-jnp.arange(n_freq, dtype=jnp.float32) / n_freq)
    ang_row = row[:, None] * freqs[None, :]
    ang_col = col[:, None] * freqs[None, :]
    silu_c = jax.nn.silu(c)
    silu_cc = jax.nn.silu(c_ctx)
    xc = ctx
    for l in range(DEPTH):
        with_ctx = l < DEPTH - 1
        mod_l = jnp.split((silu_c @ w_mod[l] + b_mod[l])[:, None, :], 6, axis=-1)
        mod_c = jnp.split(silu_cc @ w_mod[l] + b_mod[l], 6, axis=-1)
        pl = split_proj(modulate(x, mod_l[0], mod_l[1]) @ w_in[l] + b_in[l])
        pc = split_proj(modulate(xc, mod_c[0], mod_c[1]) @ w_in[l] + b_in[l])
        yr_c, yr_l = retention_branch(pc, pl, ret_decay_logit[l], ret_gn_g[l], with_ctx)
        ya_c, ya_l = attention_branch(pc, pl, attn_qn_g[l], attn_kn_g[l], ang_row, ang_col, with_ctx)
        ym_c, ym_l = mlstm_branch(pc, pl, mlstm_gn_g[l], with_ctx)
        mix_l = gated_merge(pl[-1], yr_l, ya_l, ym_l, w_br_ret[l], w_br_att[l], w_br_mlstm[l], w_out[l])
        x = layer_norm(alpha * x + mod_l[2] * mix_l, ln1_g[l], ln1_b[l])
        ffn_l = conv_ffn(modulate(x, mod_l[3], mod_l[4]), w_up[l], conv_w[l], conv_b[l], w_down[l])
        x = layer_norm(alpha * x + mod_l[5] * ffn_l, ln2_g[l], ln2_b[l])
        if with_ctx:
            mix_c = gated_merge(pc[-1], yr_c, ya_c, ym_c, w_br_ret[l], w_br_att[l], w_br_mlstm[l], w_out[l])
            xc = layer_norm(alpha * xc + mod_c[2] * mix_c, ln1_g[l], ln1_b[l])
            ffn_c = conv_ffn(modulate(xc, mod_c[3], mod_c[4]), w_up[l], conv_w[l], conv_b[l], w_down[l])
            xc = layer_norm(alpha * xc + mod_c[5] * ffn_c, ln2_g[l], ln2_b[l])
    return x
```

```python
import functools

import jax
import jax.numpy as jnp
from jax import lax
from jax.experimental import pallas as pl
from jax.experimental.pallas import tpu as pltpu

F32 = jnp.float32
BF16 = jnp.bfloat16

D_MODEL = 1024
H_RET, DK_RET, DV_RET = 4, 64, 128
H_ATT, KV_ATT, HD_ATT = 8, 2, 64
H_M, DK_M, DV_M = 4, 64, 128
CHUNK = 128
D_FF = 2816
GRID_W = 64
ROPE_THETA = 10000.0
NORM_EPS = 1e-6

LANES = 128
SUBLANES = 8
ROW_TILE = 256
Q_TILE = 128
KV_TILE = 512
FF_CHUNK = 1408
MOD_COLS = 1536
VMEM_LIMIT = 56 * 1024 * 1024

RET_W = 2 * H_RET * DK_RET + 2 * H_RET * DV_RET
ATT_W = (H_ATT + 2 * KV_ATT) * HD_ATT
MLS_W = 2 * H_M * DK_M + 2 * H_M * DV_M
GATE_W = LANES
MERGE_W = 3 * D_MODEL
OFF_ATT = RET_W
OFF_MLS = OFF_ATT + ATT_W
OFF_GATE = OFF_MLS + MLS_W
OFF_MERGE = OFF_GATE + GATE_W
N_PROJ = OFF_MERGE + MERGE_W


def _dot(a, b):
    return jnp.dot(a, b, preferred_element_type=F32)


def _dot_nt(a, b):
    return lax.dot_general(a, b, (((1,), (1,)), ((), ())), preferred_element_type=F32)


def _dot_tn(a, b):
    return lax.dot_general(a, b, (((0,), (0,)), ((), ())), preferred_element_type=F32)


def _layer_norm(x):
    mu = jnp.mean(x, axis=-1, keepdims=True)
    xc = x - mu
    var = jnp.mean(xc * xc, axis=-1, keepdims=True)
    return xc * lax.rsqrt(var + NORM_EPS)


def _log_sigmoid(x):
    return jnp.minimum(x, 0.0) - jnp.log1p(jnp.exp(-jnp.abs(x)))


def _sigmoid(x):
    return 1.0 / (1.0 + jnp.exp(-x))


def _split3(x):
    hi = x.astype(BF16)
    r1 = x - hi.astype(F32)
    mid = r1.astype(BF16)
    lo = (r1 - mid.astype(F32)).astype(BF16)
    return hi, mid, lo


def _const_spec(shape):
    zeros = (0,) * len(shape)
    return pl.BlockSpec(shape, lambda *_: zeros, pipeline_mode=pl.Buffered(1))


def _params(*sem):
    return pltpu.CompilerParams(dimension_semantics=sem, vmem_limit_bytes=VMEM_LIMIT)


def _mod_kernel(c_ref, w_ref, b_ref, o_ref):
    c = c_ref[...]
    s = (c * _sigmoid(c)).astype(BF16)
    o_ref[0] = _dot(s, w_ref[0].astype(BF16)) + b_ref[0]


def _modulation(cc, w_mod, b_mod):
    depth, _, n = w_mod.shape
    rows = cc.shape[0]
    return pl.pallas_call(
        _mod_kernel,
        out_shape=jax.ShapeDtypeStruct((depth, rows, n), F32),
        grid=(depth, n // MOD_COLS),
        in_specs=[
            pl.BlockSpec((rows, D_MODEL), lambda l, j: (0, 0)),
            pl.BlockSpec((1, D_MODEL, MOD_COLS), lambda l, j: (l, 0, j)),
            pl.BlockSpec((1, 1, MOD_COLS), lambda l, j: (l, 0, j)),
        ],
        out_specs=pl.BlockSpec((1, rows, MOD_COLS), lambda l, j: (l, 0, j)),
        compiler_params=_params("arbitrary", "arbitrary"),
        name="modulation",
    )(cc, w_mod, b_mod.reshape(depth, 1, n))


def _head_sumsq(x, ones_bd):
    sq = x * x
    hi = sq.astype(BF16)
    lo = (sq - hi.astype(F32)).astype(BF16)
    return _dot(hi, ones_bd) + _dot(lo, ones_bd)


def _norm_rope(x, gain, cos, sin, ones_bd, lane, scale):
    y = x * lax.rsqrt(_head_sumsq(x, ones_bd) * (1.0 / HD_ATT) + NORM_EPS) * gain
    above = pltpu.roll(y, LANES - 16, 1)
    below = pltpu.roll(y, 16, 1)
    partner = jnp.where((lane & 16) == 0, above, below)
    return (y * cos + partner * sin) * scale


def _proj_in_kernel(x_ref, mod_ref, w_ref, b_ref, cos_ref, sin_ref, qg_ref, kg_ref,
                    ret_ref, attq_ref, attkv_ref, mls_ref, gate_ref, merge_ref):
    tm = x_ref.shape[0]
    h = (_layer_norm(x_ref[...]) * (1.0 + mod_ref[1:2, :]) + mod_ref[0:1, :]).astype(BF16)

    def seg(lo, hi):
        return _dot(h, w_ref[:, lo:hi]) + b_ref[:, lo:hi]

    def store_qkvx(dst_ref, r):
        dst_ref[:, 0:256] = r[:, 0:256].astype(BF16)
        dst_ref[:, 256:512] = (r[:, 256:512] * 0.125).astype(BF16)
        dst_ref[:, 512:1536] = r[:, 512:1536].astype(BF16)

    store_qkvx(ret_ref, seg(0, RET_W))
    store_qkvx(mls_ref, seg(OFF_MLS, OFF_MLS + MLS_W))
    gate_ref[...] = seg(OFF_GATE, OFF_GATE + GATE_W)
    half = MERGE_W // 2
    merge_ref[:, 0:half] = seg(OFF_MERGE, OFF_MERGE + half).astype(BF16)
    merge_ref[:, half:MERGE_W] = seg(OFF_MERGE + half, OFF_MERGE + MERGE_W).astype(BF16)

    a = seg(OFF_ATT, OFF_ATT + ATT_W)
    lane = lax.broadcasted_iota(jnp.int32, (tm, LANES), 1)
    ri = lax.broadcasted_iota(jnp.int32, (LANES, LANES), 0)
    ci = lax.broadcasted_iota(jnp.int32, (LANES, LANES), 1)
    ones_bd = jnp.where((ri // HD_ATT) == (ci // HD_ATT), 1.0, 0.0).astype(BF16)
    cos = cos_ref[...]
    sin = sin_ref[...]
    qg = qg_ref[...]
    for j in range(H_ATT * HD_ATT // LANES):
        qj = _norm_rope(a[:, j * LANES:(j + 1) * LANES], qg, cos, sin, ones_bd, lane,
                        HD_ATT ** -0.5)
        attq_ref[:, j * LANES:(j + 1) * LANES] = qj.astype(BF16)
    kn = _norm_rope(a[:, 512:640], kg_ref[...], cos, sin, ones_bd, lane, 1.0)
    v = a[:, 640:768]
    kr = pltpu.roll(kn, HD_ATT, 1)
    vr = pltpu.roll(v, HD_ATT, 1)
    low = lane < HD_ATT
    attkv_ref[:, 0:128] = jnp.where(low, kn, kr).astype(BF16)
    attkv_ref[:, 128:256] = jnp.where(low, kr, kn).astype(BF16)
    attkv_ref[:, 256:384] = jnp.where(low, v, 0.0).astype(BF16)
    attkv_ref[:, 384:512] = jnp.where(low, vr, 0.0).astype(BF16)


def _row_maps(t_len, ctx_len, tm):
    nb = t_len // tm
    cb = ctx_len // tm

    def mod_map(i):
        return ((i // nb) * 2 + ((i % nb) >= cb).astype(jnp.int32), 0, 0)

    def pos_map(i):
        return (i % nb, 0)

    return nb, cb, mod_map, pos_map


def _proj_in(xs, mod_tab, w, b, cos_t, sin_t, qg, kg, t_len, ctx_len):
    rows = xs.shape[0]
    tm = ROW_TILE
    _, _, mod_map, pos_map = _row_maps(t_len, ctx_len, tm)
    row_spec = lambda n: pl.BlockSpec((tm, n), lambda i: (i, 0))
    out_shapes = [
        jax.ShapeDtypeStruct((rows, RET_W), BF16),
        jax.ShapeDtypeStruct((rows, H_ATT * HD_ATT), BF16),
        jax.ShapeDtypeStruct((rows, 4 * LANES), BF16),
        jax.ShapeDtypeStruct((rows, MLS_W), BF16),
        jax.ShapeDtypeStruct((rows, GATE_W), F32),
        jax.ShapeDtypeStruct((rows, MERGE_W), BF16),
    ]
    return pl.pallas_call(
        _proj_in_kernel,
        out_shape=out_shapes,
        grid=(rows // tm,),
        in_specs=[
            row_spec(D_MODEL),
            pl.BlockSpec((None, 6, D_MODEL), mod_map),
            _const_spec((D_MODEL, N_PROJ)),
            _const_spec((1, N_PROJ)),
            pl.BlockSpec((tm, LANES), pos_map),
            pl.BlockSpec((tm, LANES), pos_map),
            _const_spec((1, LANES)),
            _const_spec((1, LANES)),
        ],
        out_specs=[row_spec(s.shape[1]) for s in out_shapes],
        compiler_params=_params("arbitrary"),
        name="proj_in",
    )(xs, mod_tab, w, b, cos_t, sin_t, qg, kg)


def _scan_chunk(d, j, n_ctx, n_chunks):
    if d == 0:
        return j
    return jnp.where(j < n_ctx, n_ctx - 1 - j, n_chunks + n_ctx - 1 - j)


def _head_norm(o):
    mu = jnp.mean(o, axis=-1, keepdims=True)
    oc = o - mu
    var = jnp.mean(oc * oc, axis=-1, keepdims=True)
    return oc * lax.rsqrt(var + NORM_EPS)


def _ret_kernel(p_ref, dl_ref, gn_ref, y_ref, o_scr, s_scr, intra_scr, qd_scr, kd_scr, cd_scr,
                *, n_ctx, n_chunks):
    lg = _log_sigmoid(dl_ref[...])
    cd_scr[...] = jnp.exp(float(CHUNK) * lg)
    ri = lax.broadcasted_iota(jnp.int32, (CHUNK, CHUNK), 0).astype(F32)
    ci = lax.broadcasted_iota(jnp.int32, (CHUNK, CHUNK), 1).astype(F32)
    for d in range(2):
        diff = ri - ci if d == 0 else ci - ri
        qpos = ri + 1.0 if d == 0 else float(CHUNK) - ri
        kpos = float(CHUNK - 1) - ri if d == 0 else ri
        keep = diff >= 0.0
        for h in range(H_RET):
            r = d * H_RET + h
            lgb = lg[r:r + 1, :]
            intra_scr[r] = jnp.where(keep, jnp.exp(jnp.where(keep, diff, 0.0) * lgb), 0.0)
            qd_scr[r] = jnp.exp(qpos * lgb)
            kd_scr[r] = jnp.exp(kpos * lgb)

    lane = lax.broadcasted_iota(jnp.int32, (CHUNK, LANES), 1)
    half_mask = (lane < DK_RET, lane >= DK_RET)

    for d in range(2):
        s_scr[...] = jnp.zeros_like(s_scr)

        def body(j, carry, d=d):
            c = _scan_chunk(d, j, n_ctx, n_chunks)
            rows = pl.ds(pl.multiple_of(c * CHUNK, CHUNK), CHUNK)
            for pair in range(H_RET // 2):
                qp = p_ref[rows, pair * LANES:(pair + 1) * LANES]
                kp = p_ref[rows, 256 + pair * LANES:256 + (pair + 1) * LANES]
                kpf = kp.astype(F32)
                for hh in range(2):
                    h = pair * 2 + hh
                    r = d * H_RET + h
                    cols = slice(h * DV_RET, (h + 1) * DV_RET)
                    qm = jnp.where(half_mask[hh], qp, jnp.zeros_like(qp))
                    v = p_ref[rows, 512 + h * DV_RET:512 + (h + 1) * DV_RET]
                    s = _dot_nt(qm, kp) * intra_scr[r]
                    state = s_scr[h]
                    o = _dot(s.astype(BF16), v) + _dot(qm, state.astype(BF16)) * qd_scr[r]
                    kdec = (kpf * kd_scr[r]).astype(BF16)
                    s_scr[h] = state * cd_scr[r:r + 1, :] + _dot_tn(kdec, v)
                    if d == 0:
                        o_scr[rows, cols] = o
                    else:
                        tot = o_scr[rows, cols] + o
                        g = p_ref[rows, 1024 + h * DV_RET:1024 + (h + 1) * DV_RET].astype(F32)
                        y = _head_norm(tot) * gn_ref[:, cols] * (g * _sigmoid(g))
                        y_ref[rows, cols] = y.astype(BF16)
            return carry

        lax.fori_loop(0, n_chunks, body, 0)


def _retention(ret_p, dl, gn, batch, t_len, ctx_len):
    n_chunks = t_len // CHUNK
    kern = functools.partial(_ret_kernel, n_ctx=ctx_len // CHUNK, n_chunks=n_chunks)
    width = H_RET * DV_RET
    return pl.pallas_call(
        kern,
        out_shape=jax.ShapeDtypeStruct((batch * t_len, width), BF16),
        grid=(batch,),
        in_specs=[
            pl.BlockSpec((t_len, RET_W), lambda b: (b, 0)),
            _const_spec((2 * H_RET, LANES)),
            _const_spec((1, width)),
        ],
        out_specs=pl.BlockSpec((t_len, width), lambda b: (b, 0)),
        scratch_shapes=[
            pltpu.VMEM((t_len, width), F32),
            pltpu.VMEM((H_RET, LANES, DV_RET), F32),
            pltpu.VMEM((2 * H_RET, CHUNK, CHUNK), F32),
            pltpu.VMEM((2 * H_RET, CHUNK, CHUNK), F32),
            pltpu.VMEM((2 * H_RET, CHUNK, CHUNK), F32),
            pltpu.VMEM((2 * H_RET, LANES), F32),
        ],
        compiler_params=_params("arbitrary"),
        name="retention",
    )(ret_p, dl, gn)


def _mlstm_kernel(p_ref, gate_ref, gn_ref, y_ref, h_scr, cn_scr, m_scr, *, n_ctx, n_chunks):
    ri = lax.broadcasted_iota(jnp.int32, (CHUNK, CHUNK), 0)
    ci = lax.broadcasted_iota(jnp.int32, (CHUNK, CHUNK), 1)
    lane = lax.broadcasted_iota(jnp.int32, (CHUNK, LANES), 1)
    half_mask = (lane < DK_M, lane >= DK_M)
    ones_v = jnp.ones((CHUNK, LANES), BF16)

    for d in range(2):
        cn_scr[...] = jnp.zeros_like(cn_scr)
        m_scr[...] = jnp.zeros_like(m_scr)
        keep = ri >= ci if d == 0 else ci >= ri
        tri = jnp.where(keep, 1.0, 0.0).astype(BF16)
        end_row = CHUNK - 1 if d == 0 else 0

        def body(j, carry, d=d, keep=keep, tri=tri, end_row=end_row):
            c = _scan_chunk(d, j, n_ctx, n_chunks)
            rows = pl.ds(pl.multiple_of(c * CHUNK, CHUNK), CHUNK)
            gates = gate_ref[rows, :]
            hi, mid, lo = _split3(_log_sigmoid(gates))
            bcum = _dot(tri, hi) + _dot(tri, mid) + _dot(tri, lo)
            zt = jnp.transpose(jnp.where(lane < 2 * H_M, gates, bcum))
            for pair in range(H_M // 2):
                qp = p_ref[rows, pair * LANES:(pair + 1) * LANES]
                kp = p_ref[rows, 256 + pair * LANES:256 + (pair + 1) * LANES]
                kpf = kp.astype(F32)
                for hh in range(2):
                    h = pair * 2 + hh
                    gi = d * H_M + h
                    gb = 2 * H_M + d * H_M + h
                    cols = slice(h * DV_M, (h + 1) * DV_M)
                    ic_col = gates[:, gi:gi + 1]
                    b_col = bcum[:, gb:gb + 1]
                    ic_row = zt[gi:gi + 1, :]
                    b_row = zt[gb:gb + 1, :]
                    b_end = b_col[end_row:end_row + 1, :]
                    m_prev = m_scr[h][0:1, 0:1]
                    qm = jnp.where(half_mask[hh], qp, jnp.zeros_like(qp))
                    v = p_ref[rows, 512 + h * DV_M:512 + (h + 1) * DV_M]
                    vext = jnp.concatenate([v, ones_v], axis=1)

                    dmat = jnp.where(keep, (b_col - b_row) + ic_row, -jnp.inf)
                    inter = b_col + m_prev
                    m_t = jnp.maximum(inter, jnp.max(dmat, axis=-1, keepdims=True))
                    w = jnp.exp(dmat - m_t)
                    s = _dot_nt(qm, kp) * w
                    a_int = jnp.exp(inter - m_t)
                    state = cn_scr[h]
                    sv = _dot(s.astype(BF16), vext)
                    qcn = _dot(qm, state.astype(BF16))
                    num = sv[:, 0:DV_M] + a_int * qcn[:, 0:DV_M]
                    den = sv[:, DV_M:DV_M + 1] + a_int * qcn[:, DV_M:DV_M + 1]
                    hval = num / jnp.maximum(jnp.abs(den), jnp.exp(-m_t))

                    g_col = (b_end - b_col) + ic_col
                    m_new = jnp.maximum(b_end + m_prev, jnp.max(g_col, axis=0, keepdims=True))
                    wk = jnp.exp(g_col - m_new)
                    carry_scale = jnp.exp(b_end + m_prev - m_new)
                    kdec = (kpf * wk).astype(BF16)
                    cn_scr[h] = carry_scale * state + _dot_tn(kdec, vext)
                    m_scr[h] = jnp.broadcast_to(m_new, (SUBLANES, LANES))

                    if d == 0:
                        h_scr[rows, cols] = hval
                    else:
                        tot = h_scr[rows, cols] + hval
                        og = p_ref[rows, 1024 + h * DV_M:1024 + (h + 1) * DV_M].astype(F32)
                        y = _sigmoid(og) * (_head_norm(tot) * gn_ref[:, cols])
                        y_ref[rows, cols] = y.astype(BF16)
            return carry

        lax.fori_loop(0, n_chunks, body, 0)


def _mlstm(mls_p, gates, gn, batch, t_len, ctx_len):
    n_chunks = t_len // CHUNK
    kern = functools.partial(_mlstm_kernel, n_ctx=ctx_len // CHUNK, n_chunks=n_chunks)
    width = H_M * DV_M
    return pl.pallas_call(
        kern,
        out_shape=jax.ShapeDtypeStruct((batch * t_len, width), BF16),
        grid=(batch,),
        in_specs=[
            pl.BlockSpec((t_len, MLS_W), lambda b: (b, 0)),
            pl.BlockSpec((t_len, GATE_W), lambda b: (b, 0)),
            _const_spec((1, width)),
        ],
        out_specs=pl.BlockSpec((t_len, width), lambda b: (b, 0)),
        scratch_shapes=[
            pltpu.VMEM((t_len, width), F32),
            pltpu.VMEM((H_M, LANES, 2 * LANES), F32),
            pltpu.VMEM((H_M, SUBLANES, LANES), F32),
        ],
        compiler_params=_params("arbitrary"),
        name="mlstm",
    )(mls_p, gates, gn)


def _attn_kernel(q_ref, kv_ref, o_ref, m_scr, l_scr, acc_scr, *, ctx_len, t_len):
    tq = q_ref.shape[0]
    rep = H_ATT // KV_ATT
    is_ctx = pl.program_id(1) < ctx_len // tq
    lane = lax.broadcasted_iota(jnp.int32, (tq, LANES), 1)
    low = lane < HD_ATT

    def tile(g, k2, vlo, qs, first):
        s = _dot_nt(qs, k2)
        s_max = jnp.max(s, axis=-1, keepdims=True)
        if first:
            m_new = s_max
            p = jnp.exp(s - m_new)
            l_scr[...] = jnp.sum(p, axis=-1, keepdims=True)
            acc_scr[...] = _dot(p.astype(BF16), vlo)
        else:
            m_old = m_scr[...]
            m_new = jnp.maximum(m_old, s_max)
            p = jnp.exp(s - m_new)
            alpha = jnp.exp(m_old - m_new)
            l_scr[...] = alpha * l_scr[...] + jnp.sum(p, axis=-1, keepdims=True)
            acc_scr[...] = alpha * acc_scr[...] + _dot(p.astype(BF16), vlo)
        m_scr[...] = m_new

    for g in range(KV_ATT):
        parts = []
        for hh in range(rep):
            h = g * rep + hh
            qp = q_ref[:, (h // 2) * LANES:(h // 2 + 1) * LANES]
            parts.append(jnp.where(low if h % 2 == 0 else ~low, qp, jnp.zeros_like(qp)))
        qs = jnp.concatenate(parts, axis=0)
        kcols = slice(g * LANES, (g + 1) * LANES)
        vcols = slice((KV_ATT + g) * LANES, (KV_ATT + g + 1) * LANES)

        tile(g, kv_ref[0:ctx_len, kcols], kv_ref[0:ctx_len, vcols], qs, True)

        @pl.when(jnp.logical_not(is_ctx))
        def _(g=g, qs=qs, kcols=kcols, vcols=vcols):
            def body(t, carry):
                keys = pl.ds(pl.multiple_of(ctx_len + t * KV_TILE, LANES), KV_TILE)
                tile(g, kv_ref[keys, kcols], kv_ref[keys, vcols], qs, False)
                return carry

            lax.fori_loop(0, (t_len - ctx_len) // KV_TILE, body, 0)

        out = acc_scr[...] / l_scr[...]
        for pp in range(rep // 2):
            even = out[(2 * pp) * tq:(2 * pp + 1) * tq]
            odd = out[(2 * pp + 1) * tq:(2 * pp + 2) * tq]
            pair = jnp.where(low, even, pltpu.roll(odd, HD_ATT, 1))
            col = (g * rep // 2 + pp) * LANES
            o_ref[:, col:col + LANES] = pair.astype(BF16)


def _attention(att_q, att_kv, batch, t_len, ctx_len):
    tq = Q_TILE
    nq = t_len // tq
    rep = H_ATT // KV_ATT
    kern = functools.partial(_attn_kernel, ctx_len=ctx_len, t_len=t_len)
    width = H_ATT * HD_ATT
    return pl.pallas_call(
        kern,
        out_shape=jax.ShapeDtypeStruct((batch * t_len, width), BF16),
        grid=(batch, nq),
        in_specs=[
            pl.BlockSpec((tq, width), lambda b, i: (b * nq + i, 0)),
            pl.BlockSpec((t_len, 4 * LANES), lambda b, i: (b, 0)),
        ],
        out_specs=pl.BlockSpec((tq, width), lambda b, i: (b * nq + i, 0)),
        scratch_shapes=[
            pltpu.VMEM((rep * tq, 1), F32),
            pltpu.VMEM((rep * tq, 1), F32),
            pltpu.VMEM((rep * tq, LANES), F32),
        ],
        compiler_params=_params("arbitrary", "arbitrary"),
        name="attention",
    )(att_q, att_kv)


def _merge_kernel(x_ref, mod_ref, yr_ref, ya_ref, ym_ref, mg_ref, wr_ref, wa_ref, wm_ref, wo_ref,
                  lng_ref, lnb_ref, o_ref, *, alpha):
    def branch(y_ref, w_ref, k):
        gate = mg_ref[:, k * D_MODEL:(k + 1) * D_MODEL].astype(F32)
        return _sigmoid(gate) * _dot(y_ref[...], w_ref[...])

    z = branch(yr_ref, wr_ref, 0) + branch(ya_ref, wa_ref, 1) + branch(ym_ref, wm_ref, 2)
    mix = _dot(z.astype(BF16), wo_ref[...])
    y = alpha * x_ref[...] + mod_ref[2:3, :] * mix
    o_ref[...] = _layer_norm(y) * lng_ref[...] + lnb_ref[...]


def _merge(xs, mod_tab, y_ret, y_att, y_m, mg, w_r, w_a, w_m, w_o, ln_g, ln_b, t_len, ctx_len, alpha):
    rows = xs.shape[0]
    tm = ROW_TILE
    _, _, mod_map, _ = _row_maps(t_len, ctx_len, tm)
    row_spec = lambda n: pl.BlockSpec((tm, n), lambda i: (i, 0))
    bw = y_ret.shape[1]
    return pl.pallas_call(
        functools.partial(_merge_kernel, alpha=alpha),
        out_shape=jax.ShapeDtypeStruct((rows, D_MODEL), F32),
        grid=(rows // tm,),
        in_specs=[
            row_spec(D_MODEL),
            pl.BlockSpec((None, 6, D_MODEL), mod_map),
            row_spec(bw), row_spec(bw), row_spec(bw), row_spec(MERGE_W),
            _const_spec((bw, D_MODEL)), _const_spec((bw, D_MODEL)), _const_spec((bw, D_MODEL)),
            _const_spec((D_MODEL, D_MODEL)),
            _const_spec((1, D_MODEL)), _const_spec((1, D_MODEL)),
        ],
        out_specs=row_spec(D_MODEL),
        compiler_params=_params("arbitrary"),
        name="merge",
    )(xs, mod_tab, y_ret, y_att, y_m, mg, w_r, w_a, w_m, w_o, ln_g, ln_b)


def _ffn_kernel(x_ref, xp_ref, xn_ref, mod_ref, wu_ref, cw_ref, cb_ref, wd_ref, lng_ref, lnb_ref,
                o_ref, *, alpha, nb, cb):
    tm = x_ref.shape[0]
    j = pl.program_id(0) % nb
    prev_ok = jnp.logical_and(j != 0, j != cb)
    next_ok = jnp.logical_and(j != cb - 1, j != nb - 1)
    shift = mod_ref[3:4, :]
    scale = 1.0 + mod_ref[4:5, :]

    def mod(xv):
        return _layer_norm(xv) * scale + shift

    x = x_ref[...]
    hp = jnp.where(prev_ok, mod(xp_ref[...]), 0.0)
    hn = jnp.where(next_ok, mod(xn_ref[...]), 0.0)
    hext = jnp.concatenate([hp, mod(x), hn], axis=0).astype(BF16)

    def conv(u, lo, hi):
        w = cw_ref[:, lo:hi]
        return (u[SUBLANES - 1:SUBLANES - 1 + tm] * w[0:1] + u[SUBLANES:SUBLANES + tm] * w[1:2]
                + u[SUBLANES + 1:SUBLANES + 1 + tm] * w[2:3] + cb_ref[:, lo:hi])

    acc = jnp.zeros((tm, D_MODEL), F32)
    for c in range(D_FF // FF_CHUNK):
        lo, hi = c * FF_CHUNK, (c + 1) * FF_CHUNK
        a = conv(_dot(hext, wu_ref[:, lo:hi]), lo, hi)
        g = conv(_dot(hext, wu_ref[:, D_FF + lo:D_FF + hi]), D_FF + lo, D_FF + hi)
        act = (g * _sigmoid(g) * a).astype(BF16)
        acc = acc + _dot(act, wd_ref[lo:hi, :])
    y = alpha * x + mod_ref[5:6, :] * acc
    o_ref[...] = _layer_norm(y) * lng_ref[...] + lnb_ref[...]


def _ffn(xs, mod_tab, w_up, conv_w, conv_b, w_down, ln_g, ln_b, t_len, ctx_len, alpha):
    rows = xs.shape[0]
    tm = ROW_TILE
    nb, cb, mod_map, _ = _row_maps(t_len, ctx_len, tm)
    per = tm // SUBLANES
    last = rows // SUBLANES - 1
    return pl.pallas_call(
        functools.partial(_ffn_kernel, alpha=alpha, nb=nb, cb=cb),
        out_shape=jax.ShapeDtypeStruct((rows, D_MODEL), F32),
        grid=(rows // tm,),
        in_specs=[
            pl.BlockSpec((tm, D_MODEL), lambda i: (i, 0)),
            pl.BlockSpec((SUBLANES, D_MODEL), lambda i: (jnp.maximum(i * per - 1, 0), 0)),
            pl.BlockSpec((SUBLANES, D_MODEL), lambda i: (jnp.minimum((i + 1) * per, last), 0)),
            pl.BlockSpec((None, 6, D_MODEL), mod_map),
            _const_spec((D_MODEL, 2 * D_FF)),
            _const_spec((3, 2 * D_FF)),
            _const_spec((1, 2 * D_FF)),
            _const_spec((D_FF, D_MODEL)),
            _const_spec((1, D_MODEL)), _const_spec((1, D_MODEL)),
        ],
        out_specs=pl.BlockSpec((tm, D_MODEL), lambda i: (i, 0)),
        compiler_params=_params("arbitrary"),
        name="conv_ffn",
    )(xs, xs, xs, mod_tab, w_up, conv_w, conv_b, w_down, ln_g, ln_b)


def _rope_tables(seq, ctx_len):
    rows = seq // GRID_W
    row = jnp.repeat(jnp.arange(rows), GRID_W).astype(F32)
    col = jnp.tile(jnp.arange(GRID_W), rows).astype(F32)
    n_freq = HD_ATT // 4
    freqs = ROPE_THETA ** (-jnp.arange(n_freq, dtype=F32) / n_freq)
    ar = row[:, None] * freqs[None, :]
    ac = col[:, None] * freqs[None, :]
    cos = jnp.concatenate([jnp.cos(ar), jnp.cos(ar), jnp.cos(ac), jnp.cos(ac)], axis=-1)
    sin = jnp.concatenate([-jnp.sin(ar), jnp.sin(ar), -jnp.sin(ac), jnp.sin(ac)], axis=-1)
    cos = jnp.concatenate([jnp.ones((ctx_len, HD_ATT), F32), cos], axis=0)
    sin = jnp.concatenate([jnp.zeros((ctx_len, HD_ATT), F32), sin], axis=0)
    return jnp.tile(cos, (1, 2)), jnp.tile(sin, (1, 2))


def kernel(x, c, ctx, c_ctx, w_mod, b_mod, w_in, b_in, ret_decay_logit, ret_gn_g, attn_qn_g, attn_kn_g,
           mlstm_gn_g, w_br_ret, w_br_att, w_br_mlstm, w_out, ln1_g, ln1_b, w_up, conv_w, conv_b, w_down,
           ln2_g, ln2_b):
    batch, seq, _ = x.shape
    ctx_len = ctx.shape[1]
    depth = w_mod.shape[0]
    t_len = ctx_len + seq
    alpha = (2.0 * depth) ** 0.25
    assert ctx_len % ROW_TILE == 0 and seq % ROW_TILE == 0 and seq % KV_TILE == 0
    assert seq % GRID_W == 0 and batch + 1 <= SUBLANES

    xs = jnp.concatenate([ctx, x], axis=1).reshape(batch * t_len, D_MODEL)

    cc = jnp.zeros((SUBLANES, D_MODEL), F32).at[:batch].set(c).at[batch].set(c_ctx)
    mod = _modulation(cc, w_mod, b_mod)
    mod_lat = mod[:, :batch].reshape(depth, batch, 1, 6, D_MODEL)
    mod_ctx = jnp.broadcast_to(mod[:, batch].reshape(depth, 1, 1, 6, D_MODEL),
                               (depth, batch, 1, 6, D_MODEL))
    mod_tab = jnp.concatenate([mod_ctx, mod_lat], axis=2).reshape(depth, batch * 2, 6, D_MODEL)

    cos_t, sin_t = _rope_tables(seq, ctx_len)
    n_gate = 4 * H_M
    split = OFF_GATE + n_gate

    def pad_proj(a):
        pad = jnp.zeros(a.shape[:-1] + (GATE_W - n_gate,), a.dtype)
        return jnp.concatenate([a[..., :split], pad, a[..., split:]], axis=-1)

    w_in_p = pad_proj(w_in).astype(BF16)
    b_in_p = pad_proj(b_in).reshape(depth, 1, N_PROJ)
    dl = jnp.broadcast_to(ret_decay_logit.reshape(depth, 2 * H_RET, 1), (depth, 2 * H_RET, LANES))
    qg = jnp.tile(attn_qn_g, (1, 2)).reshape(depth, 1, LANES)
    kg = jnp.tile(attn_kn_g, (1, 2)).reshape(depth, 1, LANES)
    w_r, w_a, w_m, w_o = (w.astype(BF16) for w in (w_br_ret, w_br_att, w_br_mlstm, w_out))
    w_up_b = w_up.astype(BF16)
    w_down_b = w_down.astype(BF16)
    row2 = lambda a, l: a[l].reshape(1, -1)

    for l in range(depth):
        ret_p, att_q, att_kv, mls_p, gates, mg = _proj_in(
            xs, mod_tab[l], w_in_p[l], b_in_p[l], cos_t, sin_t, qg[l], kg[l], t_len, ctx_len)
        y_ret = _retention(ret_p, dl[l], row2(ret_gn_g, l), batch, t_len, ctx_len)
        y_att = _attention(att_q, att_kv, batch, t_len, ctx_len)
        y_m = _mlstm(mls_p, gates, row2(mlstm_gn_g, l), batch, t_len, ctx_len)
        xs = _merge(xs, mod_tab[l], y_ret, y_att, y_m, mg, w_r[l], w_a[l], w_m[l], w_o[l],
                    row2(ln1_g, l), row2(ln1_b, l), t_len, ctx_len, alpha)
        xs = _ffn(xs, mod_tab[l], w_up_b[l], conv_w[l], row2(conv_b, l), w_down_b[l],
                  row2(ln2_g, l), row2(ln2_b, l), t_len, ctx_len, alpha)

    return xs.reshape(batch, t_len, D_MODEL)[:, ctx_len:]
```

```python
import functools

import jax
import jax.numpy as jnp
from jax import lax
from jax.experimental import pallas as pl
from jax.experimental.pallas import tpu as pltpu

F32 = jnp.float32
BF16 = jnp.bfloat16

D_MODEL = 1024
H_RET, DK_RET, DV_RET = 4, 64, 128
H_ATT, KV_ATT, HD_ATT = 8, 2, 64
H_M, DK_M, DV_M = 4, 64, 128
CHUNK = 128
D_FF = 2816
GRID_W = 64
ROPE_THETA = 10000.0
NORM_EPS = 1e-6
LOG2_E = 1.4426950408889634

LANES = 128
SUBLANES = 8
ROW_TILE = 256
Q_TILE = 256
Q_SUB = 128
ATT_SAFE_BOUND = 40.0
ATT_ONES_ROWS = 16
FF_CHUNK = 1408
MOD_COLS = 1536
VMEM_LIMIT = 56 * 1024 * 1024

RET_W = 2 * H_RET * DK_RET + 2 * H_RET * DV_RET
ATT_W = (H_ATT + 2 * KV_ATT) * HD_ATT
MLS_W = 2 * H_M * DK_M + 2 * H_M * DV_M
GATE_W = LANES
MERGE_W = 3 * D_MODEL
OFF_ATT = RET_W
OFF_MLS = OFF_ATT + ATT_W
OFF_GATE = OFF_MLS + MLS_W
OFF_MERGE = OFF_GATE + GATE_W
N_PROJ = OFF_MERGE + MERGE_W


def _dot(a, b):
    return jnp.dot(a, b, preferred_element_type=F32)


def _dot_nt(a, b):
    return lax.dot_general(a, b, (((1,), (1,)), ((), ())), preferred_element_type=F32)


def _dot_tn(a, b):
    return lax.dot_general(a, b, (((0,), (0,)), ((), ())), preferred_element_type=F32)


def _layer_norm(x):
    mu = jnp.mean(x, axis=-1, keepdims=True)
    xc = x - mu
    var = jnp.mean(xc * xc, axis=-1, keepdims=True)
    return xc * lax.rsqrt(var + NORM_EPS)


def _log_sigmoid(x):
    return jnp.minimum(x, 0.0) - jnp.log1p(jnp.exp(-jnp.abs(x)))


def _sigmoid(x):
    return 1.0 / (1.0 + jnp.exp(-x))


def _split3(x):
    hi = x.astype(BF16)
    r1 = x - hi.astype(F32)
    mid = r1.astype(BF16)
    lo = (r1 - mid.astype(F32)).astype(BF16)
    return hi, mid, lo


def _const_spec(shape):
    zeros = (0,) * len(shape)
    return pl.BlockSpec(shape, lambda *_: zeros, pipeline_mode=pl.Buffered(1))


def _params(*sem):
    return pltpu.CompilerParams(dimension_semantics=sem, vmem_limit_bytes=VMEM_LIMIT)


def _mod_kernel(c_ref, w_ref, b_ref, o_ref):
    c = c_ref[...]
    s = (c * _sigmoid(c)).astype(BF16)
    o_ref[0] = _dot(s, w_ref[0].astype(BF16)) + b_ref[0]


def _modulation(cc, w_mod, b_mod):
    depth, _, n = w_mod.shape
    rows = cc.shape[0]
    return pl.pallas_call(
        _mod_kernel,
        out_shape=jax.ShapeDtypeStruct((depth, rows, n), F32),
        grid=(depth, n // MOD_COLS),
        in_specs=[
            pl.BlockSpec((rows, D_MODEL), lambda l, j: (0, 0)),
            pl.BlockSpec((1, D_MODEL, MOD_COLS), lambda l, j: (l, 0, j)),
            pl.BlockSpec((1, 1, MOD_COLS), lambda l, j: (l, 0, j)),
        ],
        out_specs=pl.BlockSpec((1, rows, MOD_COLS), lambda l, j: (l, 0, j)),
        compiler_params=_params("arbitrary", "arbitrary"),
        name="modulation",
    )(cc, w_mod, b_mod.reshape(depth, 1, n))


def _head_sumsq(x, ones_bd):
    sq = x * x
    hi = sq.astype(BF16)
    lo = (sq - hi.astype(F32)).astype(BF16)
    return _dot(hi, ones_bd) + _dot(lo, ones_bd)


def _norm_rope(x, gain, cos, sin, ones_bd, lane, scale):
    y = x * lax.rsqrt(_head_sumsq(x, ones_bd) * (1.0 / HD_ATT) + NORM_EPS) * gain
    above = pltpu.roll(y, LANES - 16, 1)
    below = pltpu.roll(y, 16, 1)
    partner = jnp.where((lane & 16) == 0, above, below)
    return (y * cos + partner * sin) * scale


def _proj_in_kernel(x_ref, mod_ref, w_ref, b_ref, cos_ref, sin_ref, qg_ref, kg_ref,
                    ret_ref, attq_ref, attk_ref, attvt_ref, mls_ref, gate_ref, merge_ref):
    tm = x_ref.shape[0]
    h = (_layer_norm(x_ref[...]) * (1.0 + mod_ref[1:2, :]) + mod_ref[0:1, :]).astype(BF16)

    def seg(lo, hi):
        return _dot(h, w_ref[:, lo:hi]) + b_ref[:, lo:hi]

    def store_qkvx(dst_ref, r):
        dst_ref[:, 0:256] = r[:, 0:256].astype(BF16)
        dst_ref[:, 256:512] = (r[:, 256:512] * 0.125).astype(BF16)
        dst_ref[:, 512:1536] = r[:, 512:1536].astype(BF16)

    store_qkvx(ret_ref, seg(0, RET_W))
    store_qkvx(mls_ref, seg(OFF_MLS, OFF_MLS + MLS_W))
    gate_ref[...] = seg(OFF_GATE, OFF_GATE + GATE_W)
    half = MERGE_W // 2
    merge_ref[:, 0:half] = seg(OFF_MERGE, OFF_MERGE + half).astype(BF16)
    merge_ref[:, half:MERGE_W] = seg(OFF_MERGE + half, OFF_MERGE + MERGE_W).astype(BF16)

    a = seg(OFF_ATT, OFF_ATT + ATT_W)
    lane = lax.broadcasted_iota(jnp.int32, (tm, LANES), 1)
    ri = lax.broadcasted_iota(jnp.int32, (LANES, LANES), 0)
    ci = lax.broadcasted_iota(jnp.int32, (LANES, LANES), 1)
    ones_bd = jnp.where((ri // HD_ATT) == (ci // HD_ATT), 1.0, 0.0).astype(BF16)
    cos = cos_ref[...]
    sin = sin_ref[...]
    qg = qg_ref[...]
    for j in range(H_ATT * HD_ATT // LANES):
        qj = _norm_rope(a[:, j * LANES:(j + 1) * LANES], qg, cos, sin, ones_bd, lane,
                        HD_ATT ** -0.5 * LOG2_E)
        attq_ref[:, j * LANES:(j + 1) * LANES] = qj.astype(BF16)
    kn = _norm_rope(a[:, 512:640], kg_ref[...], cos, sin, ones_bd, lane, 1.0)
    kr = pltpu.roll(kn, HD_ATT, 1)
    low = lane < HD_ATT
    attk_ref[:, 0:128] = jnp.where(low, kn, kr).astype(BF16)
    attk_ref[:, 128:256] = jnp.where(low, kr, kn).astype(BF16)
    attvt_ref[0] = jnp.transpose(a[:, 640:768]).astype(BF16)


def _row_maps(t_len, ctx_len, tm):
    nb = t_len // tm
    cb = ctx_len // tm

    def mod_map(i):
        return ((i // nb) * 2 + ((i % nb) >= cb).astype(jnp.int32), 0, 0)

    def pos_map(i):
        return (i % nb, 0)

    return nb, cb, mod_map, pos_map


def _proj_in(xs, mod_tab, w, b, cos_t, sin_t, qg, kg, t_len, ctx_len):
    rows = xs.shape[0]
    tm = ROW_TILE
    _, _, mod_map, pos_map = _row_maps(t_len, ctx_len, tm)
    row_spec = lambda n: pl.BlockSpec((tm, n), lambda i: (i, 0))
    out_shapes = [
        jax.ShapeDtypeStruct((rows, RET_W), BF16),
        jax.ShapeDtypeStruct((rows, H_ATT * HD_ATT), BF16),
        jax.ShapeDtypeStruct((rows, 2 * LANES), BF16),
        jax.ShapeDtypeStruct((rows // tm, KV_ATT * HD_ATT, tm), BF16),
        jax.ShapeDtypeStruct((rows, MLS_W), BF16),
        jax.ShapeDtypeStruct((rows, GATE_W), F32),
        jax.ShapeDtypeStruct((rows, MERGE_W), BF16),
    ]

    def out_spec(s):
        if len(s.shape) == 3:
            return pl.BlockSpec((1,) + s.shape[1:], lambda i: (i, 0, 0))
        return row_spec(s.shape[1])

    return pl.pallas_call(
        _proj_in_kernel,
        out_shape=out_shapes,
        grid=(rows // tm,),
        in_specs=[
            row_spec(D_MODEL),
            pl.BlockSpec((None, 6, D_MODEL), mod_map),
            _const_spec((D_MODEL, N_PROJ)),
            _const_spec((1, N_PROJ)),
            pl.BlockSpec((tm, LANES), pos_map),
            pl.BlockSpec((tm, LANES), pos_map),
            _const_spec((1, LANES)),
            _const_spec((1, LANES)),
        ],
        out_specs=[out_spec(s) for s in out_shapes],
        compiler_params=_params("arbitrary"),
        name="proj_in",
    )(xs, mod_tab, w, b, cos_t, sin_t, qg, kg)


def _scan_chunk(d, j, n_ctx, n_chunks):
    if d == 0:
        return j
    return jnp.where(j < n_ctx, n_ctx - 1 - j, n_chunks + n_ctx - 1 - j)


def _head_norm(o):
    mu = jnp.mean(o, axis=-1, keepdims=True)
    oc = o - mu
    var = jnp.mean(oc * oc, axis=-1, keepdims=True)
    return oc * lax.rsqrt(var + NORM_EPS)


def _ret_kernel(p_ref, dl_ref, gn_ref, y_ref, o_scr, s_scr, intra_scr, qd_scr, kd_scr, cd_scr,
                *, n_ctx, n_chunks):
    lg = _log_sigmoid(dl_ref[...])
    cd_scr[...] = jnp.exp(float(CHUNK) * lg)
    ri = lax.broadcasted_iota(jnp.int32, (CHUNK, CHUNK), 0).astype(F32)
    ci = lax.broadcasted_iota(jnp.int32, (CHUNK, CHUNK), 1).astype(F32)
    for d in range(2):
        diff = ri - ci if d == 0 else ci - ri
        qpos = ri + 1.0 if d == 0 else float(CHUNK) - ri
        kpos = float(CHUNK - 1) - ri if d == 0 else ri
        keep = diff >= 0.0
        for h in range(H_RET):
            r = d * H_RET + h
            lgb = lg[r:r + 1, :]
            intra_scr[r] = jnp.where(keep, jnp.exp(jnp.where(keep, diff, 0.0) * lgb), 0.0)
            qd_scr[r] = jnp.exp(qpos * lgb)
            kd_scr[r] = jnp.exp(kpos * lgb)

    lane = lax.broadcasted_iota(jnp.int32, (CHUNK, LANES), 1)
    half_mask = (lane < DK_RET, lane >= DK_RET)

    for d in range(2):
        s_scr[...] = jnp.zeros_like(s_scr)

        def body(j, carry, d=d):
            c = _scan_chunk(d, j, n_ctx, n_chunks)
            rows = pl.ds(pl.multiple_of(c * CHUNK, CHUNK), CHUNK)
            for pair in range(H_RET // 2):
                qp = p_ref[rows, pair * LANES:(pair + 1) * LANES]
                kp = p_ref[rows, 256 + pair * LANES:256 + (pair + 1) * LANES]
                kpf = kp.astype(F32)
                for hh in range(2):
                    h = pair * 2 + hh
                    r = d * H_RET + h
                    cols = slice(h * DV_RET, (h + 1) * DV_RET)
                    qm = jnp.where(half_mask[hh], qp, jnp.zeros_like(qp))
                    v = p_ref[rows, 512 + h * DV_RET:512 + (h + 1) * DV_RET]
                    s = _dot_nt(qm, kp) * intra_scr[r]
                    state = s_scr[h]
                    o = _dot(s.astype(BF16), v) + _dot(qm, state.astype(BF16)) * qd_scr[r]
                    kdec = (kpf * kd_scr[r]).astype(BF16)
                    s_scr[h] = state * cd_scr[r:r + 1, :] + _dot_tn(kdec, v)
                    if d == 0:
                        o_scr[rows, cols] = o
                    else:
                        tot = o_scr[rows, cols] + o
                        g = p_ref[rows, 1024 + h * DV_RET:1024 + (h + 1) * DV_RET].astype(F32)
                        y = _head_norm(tot) * gn_ref[:, cols] * (g * _sigmoid(g))
                        y_ref[rows, cols] = y.astype(BF16)
            return carry

        lax.fori_loop(0, n_chunks, body, 0)


def _retention(ret_p, dl, gn, batch, t_len, ctx_len):
    n_chunks = t_len // CHUNK
    kern = functools.partial(_ret_kernel, n_ctx=ctx_len // CHUNK, n_chunks=n_chunks)
    width = H_RET * DV_RET
    return pl.pallas_call(
        kern,
        out_shape=jax.ShapeDtypeStruct((batch * t_len, width), BF16),
        grid=(batch,),
        in_specs=[
            pl.BlockSpec((t_len, RET_W), lambda b: (b, 0)),
            _const_spec((2 * H_RET, LANES)),
            _const_spec((1, width)),
        ],
        out_specs=pl.BlockSpec((t_len, width), lambda b: (b, 0)),
        scratch_shapes=[
            pltpu.VMEM((t_len, width), F32),
            pltpu.VMEM((H_RET, LANES, DV_RET), F32),
            pltpu.VMEM((2 * H_RET, CHUNK, CHUNK), F32),
            pltpu.VMEM((2 * H_RET, CHUNK, CHUNK), F32),
            pltpu.VMEM((2 * H_RET, CHUNK, CHUNK), F32),
            pltpu.VMEM((2 * H_RET, LANES), F32),
        ],
        compiler_params=_params("arbitrary"),
        name="retention",
    )(ret_p, dl, gn)


def _mlstm_kernel(p_ref, gate_ref, gn_ref, y_ref, h_scr, cn_scr, m_scr, *, n_ctx, n_chunks):
    ri = lax.broadcasted_iota(jnp.int32, (CHUNK, CHUNK), 0)
    ci = lax.broadcasted_iota(jnp.int32, (CHUNK, CHUNK), 1)
    lane = lax.broadcasted_iota(jnp.int32, (CHUNK, LANES), 1)
    half_mask = (lane < DK_M, lane >= DK_M)
    ones_v = jnp.ones((CHUNK, LANES), BF16)

    for d in range(2):
        cn_scr[...] = jnp.zeros_like(cn_scr)
        m_scr[...] = jnp.zeros_like(m_scr)
        keep = ri >= ci if d == 0 else ci >= ri
        tri = jnp.where(keep, 1.0, 0.0).astype(BF16)
        end_row = CHUNK - 1 if d == 0 else 0

        def body(j, carry, d=d, keep=keep, tri=tri, end_row=end_row):
            c = _scan_chunk(d, j, n_ctx, n_chunks)
            rows = pl.ds(pl.multiple_of(c * CHUNK, CHUNK), CHUNK)
            gates = gate_ref[rows, :]
            hi, mid, lo = _split3(_log_sigmoid(gates))
            bcum = _dot(tri, hi) + _dot(tri, mid) + _dot(tri, lo)
            zt = jnp.transpose(jnp.where(lane < 2 * H_M, gates, bcum))
            for pair in range(H_M // 2):
                qp = p_ref[rows, pair * LANES:(pair + 1) * LANES]
                kp = p_ref[rows, 256 + pair * LANES:256 + (pair + 1) * LANES]
                kpf = kp.astype(F32)
                for hh in range(2):
                    h = pair * 2 + hh
                    gi = d * H_M + h
                    gb = 2 * H_M + d * H_M + h
                    cols = slice(h * DV_M, (h + 1) * DV_M)
                    ic_col = gates[:, gi:gi + 1]
                    b_col = bcum[:, gb:gb + 1]
                    ic_row = zt[gi:gi + 1, :]
                    b_row = zt[gb:gb + 1, :]
                    b_end = b_col[end_row:end_row + 1, :]
                    m_prev = m_scr[h][0:1, 0:1]
                    qm = jnp.where(half_mask[hh], qp, jnp.zeros_like(qp))
                    v = p_ref[rows, 512 + h * DV_M:512 + (h + 1) * DV_M]
                    vext = jnp.concatenate([v, ones_v], axis=1)

                    dmat = jnp.where(keep, (b_col - b_row) + ic_row, -jnp.inf)
                    inter = b_col + m_prev
                    m_t = jnp.maximum(inter, jnp.max(dmat, axis=-1, keepdims=True))
                    w = jnp.exp(dmat - m_t)
                    s = _dot_nt(qm, kp) * w
                    a_int = jnp.exp(inter - m_t)
                    state = cn_scr[h]
                    sv = _dot(s.astype(BF16), vext)
                    qcn = _dot(qm, state.astype(BF16))
                    num = sv[:, 0:DV_M] + a_int * qcn[:, 0:DV_M]
                    den = sv[:, DV_M:DV_M + 1] + a_int * qcn[:, DV_M:DV_M + 1]
                    hval = num / jnp.maximum(jnp.abs(den), jnp.exp(-m_t))

                    g_col = (b_end - b_col) + ic_col
                    m_new = jnp.maximum(b_end + m_prev, jnp.max(g_col, axis=0, keepdims=True))
                    wk = jnp.exp(g_col - m_new)
                    carry_scale = jnp.exp(b_end + m_prev - m_new)
                    kdec = (kpf * wk).astype(BF16)
                    cn_scr[h] = carry_scale * state + _dot_tn(kdec, vext)
                    m_scr[h] = jnp.broadcast_to(m_new, (SUBLANES, LANES))

                    if d == 0:
                        h_scr[rows, cols] = hval
                    else:
                        tot = h_scr[rows, cols] + hval
                        og = p_ref[rows, 1024 + h * DV_M:1024 + (h + 1) * DV_M].astype(F32)
                        y = _sigmoid(og) * (_head_norm(tot) * gn_ref[:, cols])
                        y_ref[rows, cols] = y.astype(BF16)
            return carry

        lax.fori_loop(0, n_chunks, body, 0)


def _mlstm(mls_p, gates, gn, batch, t_len, ctx_len):
    n_chunks = t_len // CHUNK
    kern = functools.partial(_mlstm_kernel, n_ctx=ctx_len // CHUNK, n_chunks=n_chunks)
    width = H_M * DV_M
    return pl.pallas_call(
        kern,
        out_shape=jax.ShapeDtypeStruct((batch * t_len, width), BF16),
        grid=(batch,),
        in_specs=[
            pl.BlockSpec((t_len, MLS_W), lambda b: (b, 0)),
            pl.BlockSpec((t_len, GATE_W), lambda b: (b, 0)),
            _const_spec((1, width)),
        ],
        out_specs=pl.BlockSpec((t_len, width), lambda b: (b, 0)),
        scratch_shapes=[
            pltpu.VMEM((t_len, width), F32),
            pltpu.VMEM((H_M, LANES, 2 * LANES), F32),
            pltpu.VMEM((H_M, SUBLANES, LANES), F32),
        ],
        compiler_params=_params("arbitrary"),
        name="mlstm",
    )(mls_p, gates, gn)


def _attn_kernel(bound_ref, q_ref, k_ref, vt_ref, o_ref, m_scr, alpha_scr, acc_scr, p_scr,
                 *, ctx_len, t_len, bounded):
    tq = q_ref.shape[0]
    tk = vt_ref.shape[2]
    rep = H_ATT // KV_ATT
    n_tiles = jnp.where(pl.program_id(1) < ctx_len // tq, ctx_len // tk, t_len // tk)
    lane = lax.broadcasted_iota(jnp.int32, (Q_SUB, LANES), 1)
    low = lane < HD_ATT
    ones_rows = jnp.ones((ATT_ONES_ROWS, tk), BF16)
    bound = bound_ref[0]

    chains = [(g, qb) for g in range(KV_ATT) for qb in range(tq // Q_SUB)]
    qs = []
    for g, qb in chains:
        parts = []
        for hh in range(rep):
            h = g * rep + hh
            qp = q_ref[qb * Q_SUB:(qb + 1) * Q_SUB, (h // 2) * LANES:(h // 2 + 1) * LANES]
            parts.append(jnp.where(low if h % 2 == 0 else ~low, qp, jnp.zeros_like(qp)))
        qs.append(jnp.concatenate(parts, axis=0))
    m_scr[...] = jnp.full_like(m_scr, -jnp.inf)
    acc_scr[...] = jnp.zeros_like(acc_scr)
    p_scr[...] = jnp.zeros_like(p_scr)
    alpha_scr[...] = jnp.ones_like(alpha_scr)

    def accumulate(c, g, t_prev):
        v_ext = jnp.concatenate([vt_ref[t_prev][g * HD_ATT:(g + 1) * HD_ATT, :], ones_rows], axis=0)
        pv = _dot(v_ext, p_scr[c])
        if bounded:
            acc_scr[c] += pv
        else:
            acc_scr[c] = alpha_scr[c, 0:1, :] * acc_scr[c] + pv

    def body(t, carry):
        keys = pl.ds(pl.multiple_of(t * tk, tk), tk)
        t_prev = jnp.maximum(t - 1, 0)
        scores = [_dot_nt(k_ref[keys, g * LANES:(g + 1) * LANES], qs[c])
                  for c, (g, _) in enumerate(chains)]
        for c, (g, _) in enumerate(chains):
            accumulate(c, g, t_prev)
        for c, s_t in enumerate(scores):
            if bounded:
                p_scr[c] = jnp.exp2(s_t - bound).astype(BF16)
            else:
                m_old = m_scr[c, 0:1, :]
                m_new = jnp.maximum(m_old, jnp.max(s_t, axis=0, keepdims=True))
                alpha_scr[c, 0:1, :] = jnp.exp2(m_old - m_new)
                p_scr[c] = jnp.exp2(s_t - m_new).astype(BF16)
                m_scr[c, 0:1, :] = m_new
        return carry

    lax.fori_loop(0, n_tiles, body, 0)
    for c, (g, _) in enumerate(chains):
        accumulate(c, g, n_tiles - 1)

    for c, (g, qb) in enumerate(chains):
        acc = acc_scr[c]
        out_t = acc[0:HD_ATT] / acc[HD_ATT:HD_ATT + 1]
        for pp in range(rep // 2):
            pair_t = jnp.concatenate([out_t[:, (2 * pp) * Q_SUB:(2 * pp + 1) * Q_SUB],
                                      out_t[:, (2 * pp + 1) * Q_SUB:(2 * pp + 2) * Q_SUB]], axis=0)
            col = (g * rep // 2 + pp) * LANES
            o_ref[qb * Q_SUB:(qb + 1) * Q_SUB, col:col + LANES] = jnp.transpose(pair_t).astype(BF16)


def _attention(att_q, att_k, att_vt, bound, batch, t_len, ctx_len, bounded):
    tq = Q_TILE
    nq = t_len // tq
    rep = H_ATT // KV_ATT
    slabs, vrows, tk = att_vt.shape
    per_batch = slabs // batch
    n_chains = KV_ATT * (tq // Q_SUB)
    kern = functools.partial(_attn_kernel, ctx_len=ctx_len, t_len=t_len, bounded=bounded)
    width = H_ATT * HD_ATT
    return pl.pallas_call(
        kern,
        out_shape=jax.ShapeDtypeStruct((batch * t_len, width), BF16),
        grid=(batch, nq),
        in_specs=[
            pl.BlockSpec(memory_space=pltpu.SMEM),
            pl.BlockSpec((tq, width), lambda b, i: (b * nq + i, 0)),
            pl.BlockSpec((t_len, 2 * LANES), lambda b, i: (b, 0)),
            pl.BlockSpec((per_batch, vrows, tk), lambda b, i: (b, 0, 0)),
        ],
        out_specs=pl.BlockSpec((tq, width), lambda b, i: (b * nq + i, 0)),
        scratch_shapes=[
            pltpu.VMEM((n_chains, SUBLANES, rep * Q_SUB), F32),
            pltpu.VMEM((n_chains, SUBLANES, rep * Q_SUB), F32),
            pltpu.VMEM((n_chains, HD_ATT + ATT_ONES_ROWS, rep * Q_SUB), F32),
            pltpu.VMEM((n_chains, tk, rep * Q_SUB), BF16),
        ],
        compiler_params=_params("arbitrary", "arbitrary"),
        name="attention_bounded" if bounded else "attention_online",
    )(bound, att_q, att_k, att_vt)


def _merge_kernel(x_ref, mod_ref, yr_ref, ya_ref, ym_ref, mg_ref, wr_ref, wa_ref, wm_ref, wo_ref,
                  lng_ref, lnb_ref, o_ref, *, alpha):
    def branch(y_ref, w_ref, k):
        gate = mg_ref[:, k * D_MODEL:(k + 1) * D_MODEL].astype(F32)
        return _sigmoid(gate) * _dot(y_ref[...], w_ref[...])

    z = branch(yr_ref, wr_ref, 0) + branch(ya_ref, wa_ref, 1) + branch(ym_ref, wm_ref, 2)
    mix = _dot(z.astype(BF16), wo_ref[...])
    y = alpha * x_ref[...] + mod_ref[2:3, :] * mix
    o_ref[...] = _layer_norm(y) * lng_ref[...] + lnb_ref[...]


def _merge(xs, mod_tab, y_ret, y_att, y_m, mg, w_r, w_a, w_m, w_o, ln_g, ln_b, t_len, ctx_len, alpha):
    rows = xs.shape[0]
    tm = ROW_TILE
    _, _, mod_map, _ = _row_maps(t_len, ctx_len, tm)
    row_spec = lambda n: pl.BlockSpec((tm, n), lambda i: (i, 0))
    bw = y_ret.shape[1]
    return pl.pallas_call(
        functools.partial(_merge_kernel, alpha=alpha),
        out_shape=jax.ShapeDtypeStruct((rows, D_MODEL), F32),
        grid=(rows // tm,),
        in_specs=[
            row_spec(D_MODEL),
            pl.BlockSpec((None, 6, D_MODEL), mod_map),
            row_spec(bw), row_spec(bw), row_spec(bw), row_spec(MERGE_W),
            _const_spec((bw, D_MODEL)), _const_spec((bw, D_MODEL)), _const_spec((bw, D_MODEL)),
            _const_spec((D_MODEL, D_MODEL)),
            _const_spec((1, D_MODEL)), _const_spec((1, D_MODEL)),
        ],
        out_specs=row_spec(D_MODEL),
        compiler_params=_params("arbitrary"),
        name="merge",
    )(xs, mod_tab, y_ret, y_att, y_m, mg, w_r, w_a, w_m, w_o, ln_g, ln_b)


def _ffn_kernel(x_ref, xp_ref, xn_ref, mod_ref, wu_ref, cw_ref, cb_ref, wd_ref, lng_ref, lnb_ref,
                o_ref, *, alpha, nb, cb):
    tm = x_ref.shape[0]
    j = pl.program_id(0) % nb
    prev_ok = jnp.logical_and(j != 0, j != cb)
    next_ok = jnp.logical_and(j != cb - 1, j != nb - 1)
    shift = mod_ref[3:4, :]
    scale = 1.0 + mod_ref[4:5, :]

    def mod(xv):
        return _layer_norm(xv) * scale + shift

    x = x_ref[...]
    hp = jnp.where(prev_ok, mod(xp_ref[...]), 0.0)
    hn = jnp.where(next_ok, mod(xn_ref[...]), 0.0)
    hext = jnp.concatenate([hp, mod(x), hn], axis=0).astype(BF16)

    def conv(u, lo, hi):
        w = cw_ref[:, lo:hi]
        return (u[SUBLANES - 1:SUBLANES - 1 + tm] * w[0:1] + u[SUBLANES:SUBLANES + tm] * w[1:2]
                + u[SUBLANES + 1:SUBLANES + 1 + tm] * w[2:3] + cb_ref[:, lo:hi])

    acc = jnp.zeros((tm, D_MODEL), F32)
    for c in range(D_FF // FF_CHUNK):
        lo, hi = c * FF_CHUNK, (c + 1) * FF_CHUNK
        a = conv(_dot(hext, wu_ref[:, lo:hi]), lo, hi)
        g = conv(_dot(hext, wu_ref[:, D_FF + lo:D_FF + hi]), D_FF + lo, D_FF + hi)
        act = (g * _sigmoid(g) * a).astype(BF16)
        acc = acc + _dot(act, wd_ref[lo:hi, :])
    y = alpha * x + mod_ref[5:6, :] * acc
    o_ref[...] = _layer_norm(y) * lng_ref[...] + lnb_ref[...]


def _ffn(xs, mod_tab, w_up, conv_w, conv_b, w_down, ln_g, ln_b, t_len, ctx_len, alpha):
    rows = xs.shape[0]
    tm = ROW_TILE
    nb, cb, mod_map, _ = _row_maps(t_len, ctx_len, tm)
    per = tm // SUBLANES
    last = rows // SUBLANES - 1
    return pl.pallas_call(
        functools.partial(_ffn_kernel, alpha=alpha, nb=nb, cb=cb),
        out_shape=jax.ShapeDtypeStruct((rows, D_MODEL), F32),
        grid=(rows // tm,),
        in_specs=[
            pl.BlockSpec((tm, D_MODEL), lambda i: (i, 0)),
            pl.BlockSpec((SUBLANES, D_MODEL), lambda i: (jnp.maximum(i * per - 1, 0), 0)),
            pl.BlockSpec((SUBLANES, D_MODEL), lambda i: (jnp.minimum((i + 1) * per, last), 0)),
            pl.BlockSpec((None, 6, D_MODEL), mod_map),
            _const_spec((D_MODEL, 2 * D_FF)),
            _const_spec((3, 2 * D_FF)),
            _const_spec((1, 2 * D_FF)),
            _const_spec((D_FF, D_MODEL)),
            _const_spec((1, D_MODEL)), _const_spec((1, D_MODEL)),
        ],
        out_specs=pl.BlockSpec((tm, D_MODEL), lambda i: (i, 0)),
        compiler_params=_params("arbitrary"),
        name="conv_ffn",
    )(xs, xs, xs, mod_tab, w_up, conv_w, conv_b, w_down, ln_g, ln_b)


def _rope_tables(seq, ctx_len):
    rows = seq // GRID_W
    row = jnp.repeat(jnp.arange(rows), GRID_W).astype(F32)
    col = jnp.tile(jnp.arange(GRID_W), rows).astype(F32)
    n_freq = HD_ATT // 4
    freqs = ROPE_THETA ** (-jnp.arange(n_freq, dtype=F32) / n_freq)
    ar = row[:, None] * freqs[None, :]
    ac = col[:, None] * freqs[None, :]
    cos = jnp.concatenate([jnp.cos(ar), jnp.cos(ar), jnp.cos(ac), jnp.cos(ac)], axis=-1)
    sin = jnp.concatenate([-jnp.sin(ar), jnp.sin(ar), -jnp.sin(ac), jnp.sin(ac)], axis=-1)
    cos = jnp.concatenate([jnp.ones((ctx_len, HD_ATT), F32), cos], axis=0)
    sin = jnp.concatenate([jnp.zeros((ctx_len, HD_ATT), F32), sin], axis=0)
    return jnp.tile(cos, (1, 2)), jnp.tile(sin, (1, 2))


def kernel(x, c, ctx, c_ctx, w_mod, b_mod, w_in, b_in, ret_decay_logit, ret_gn_g, attn_qn_g, attn_kn_g,
           mlstm_gn_g, w_br_ret, w_br_att, w_br_mlstm, w_out, ln1_g, ln1_b, w_up, conv_w, conv_b, w_down,
           ln2_g, ln2_b):
    batch, seq, _ = x.shape
    ctx_len = ctx.shape[1]
    depth = w_mod.shape[0]
    t_len = ctx_len + seq
    alpha = (2.0 * depth) ** 0.25
    assert ctx_len % ROW_TILE == 0 and seq % ROW_TILE == 0
    assert seq % GRID_W == 0 and batch + 1 <= SUBLANES

    xs = jnp.concatenate([ctx, x], axis=1).reshape(batch * t_len, D_MODEL)

    cc = jnp.zeros((SUBLANES, D_MODEL), F32).at[:batch].set(c).at[batch].set(c_ctx)
    mod = _modulation(cc, w_mod, b_mod)
    mod_lat = mod[:, :batch].reshape(depth, batch, 1, 6, D_MODEL)
    mod_ctx = jnp.broadcast_to(mod[:, batch].reshape(depth, 1, 1, 6, D_MODEL),
                               (depth, batch, 1, 6, D_MODEL))
    mod_tab = jnp.concatenate([mod_ctx, mod_lat], axis=2).reshape(depth, batch * 2, 6, D_MODEL)

    cos_t, sin_t = _rope_tables(seq, ctx_len)
    n_gate = 4 * H_M
    split = OFF_GATE + n_gate

    def pad_proj(a):
        pad = jnp.zeros(a.shape[:-1] + (GATE_W - n_gate,), a.dtype)
        return jnp.concatenate([a[..., :split], pad, a[..., split:]], axis=-1)

    w_in_p = pad_proj(w_in).astype(BF16)
    b_in_p = pad_proj(b_in).reshape(depth, 1, N_PROJ)
    dl = jnp.broadcast_to(ret_decay_logit.reshape(depth, 2 * H_RET, 1), (depth, 2 * H_RET, LANES))
    qg = jnp.tile(attn_qn_g, (1, 2)).reshape(depth, 1, LANES)
    kg = jnp.tile(attn_kn_g, (1, 2)).reshape(depth, 1, LANES)
    w_r, w_a, w_m, w_o = (w.astype(BF16) for w in (w_br_ret, w_br_att, w_br_mlstm, w_out))
    w_up_b = w_up.astype(BF16)
    w_down_b = w_down.astype(BF16)
    row2 = lambda a, l: a[l].reshape(1, -1)

    for l in range(depth):
        ret_p, att_q, att_k, att_vt, mls_p, gates, mg = _proj_in(
            xs, mod_tab[l], w_in_p[l], b_in_p[l], cos_t, sin_t, qg[l], kg[l], t_len, ctx_len)
        y_ret = _retention(ret_p, dl[l], row2(ret_gn_g, l), batch, t_len, ctx_len)
        bound = (1.01 * LOG2_E * HD_ATT ** 0.5 * jnp.max(jnp.abs(attn_qn_g[l]))
                 * jnp.max(jnp.abs(attn_kn_g[l]))).reshape(1)
        attend = lambda fixed: functools.partial(
            _attention, batch=batch, t_len=t_len, ctx_len=ctx_len, bounded=fixed)
        y_att = lax.cond(bound[0] <= ATT_SAFE_BOUND, attend(True), attend(False),
                         att_q, att_k, att_vt, bound)
        y_m = _mlstm(mls_p, gates, row2(mlstm_gn_g, l), batch, t_len, ctx_len)
        xs = _merge(xs, mod_tab[l], y_ret, y_att, y_m, mg, w_r[l], w_a[l], w_m[l], w_o[l],
                    row2(ln1_g, l), row2(ln1_b, l), t_len, ctx_len, alpha)
        xs = _ffn(xs, mod_tab[l], w_up_b[l], conv_w[l], row2(conv_b, l), w_down_b[l],
                  row2(ln2_g, l), row2(ln2_b, l), t_len, ctx_len, alpha)

    return xs.reshape(batch, t_len, D_MODEL)[:, ctx_len:]
```

```python
import functools

import jax
import jax.numpy as jnp
from jax import lax
from jax.experimental import pallas as pl
from jax.experimental.pallas import tpu as pltpu

F32 = jnp.float32
BF16 = jnp.bfloat16

D_MODEL = 1024
H_RET, DK_RET, DV_RET = 4, 64, 128
H_ATT, KV_ATT, HD_ATT = 8, 2, 64
H_M, DK_M, DV_M = 4, 64, 128
CHUNK = 128
D_FF = 2816
GRID_W = 64
ROPE_THETA = 10000.0
NORM_EPS = 1e-6
LOG2_E = 1.4426950408889634

LANES = 128
SUBLANES = 8
ROW_TILE = 256
Q_TILE = 256
Q_SUB = 128
ATT_SAFE_BOUND = 40.0
ATT_ONES_ROWS = 16
FF_CHUNK = 256
MOD_COLS = 1536
VMEM_LIMIT = 56 * 1024 * 1024

RET_W = 2 * H_RET * DK_RET + 2 * H_RET * DV_RET
ATT_W = (H_ATT + 2 * KV_ATT) * HD_ATT
MLS_W = 2 * H_M * DK_M + 2 * H_M * DV_M
GATE_W = LANES
MERGE_W = 3 * D_MODEL
OFF_ATT = RET_W
OFF_MLS = OFF_ATT + ATT_W
OFF_GATE = OFF_MLS + MLS_W
OFF_MERGE = OFF_GATE + GATE_W
N_PROJ = OFF_MERGE + MERGE_W


def _dot(a, b):
    return jnp.dot(a, b, preferred_element_type=F32)


def _dot_nt(a, b):
    return lax.dot_general(a, b, (((1,), (1,)), ((), ())), preferred_element_type=F32)


def _dot_tn(a, b):
    return lax.dot_general(a, b, (((0,), (0,)), ((), ())), preferred_element_type=F32)


def _layer_norm(x):
    mu = jnp.mean(x, axis=-1, keepdims=True)
    xc = x - mu
    var = jnp.mean(xc * xc, axis=-1, keepdims=True)
    return xc * lax.rsqrt(var + NORM_EPS)


def _log_sigmoid(x):
    return jnp.minimum(x, 0.0) - jnp.log1p(jnp.exp(-jnp.abs(x)))


def _sigmoid(x):
    return 1.0 / (1.0 + jnp.exp(-x))


def _split3(x):
    hi = x.astype(BF16)
    r1 = x - hi.astype(F32)
    mid = r1.astype(BF16)
    lo = (r1 - mid.astype(F32)).astype(BF16)
    return hi, mid, lo


def _const_spec(shape):
    zeros = (0,) * len(shape)
    return pl.BlockSpec(shape, lambda *_: zeros, pipeline_mode=pl.Buffered(1))


def _params(*sem):
    return pltpu.CompilerParams(dimension_semantics=sem, vmem_limit_bytes=VMEM_LIMIT)


def _mod_kernel(c_ref, w_ref, b_ref, o_ref):
    c = c_ref[...]
    s = (c * _sigmoid(c)).astype(BF16)
    o_ref[0] = _dot(s, w_ref[0].astype(BF16)) + b_ref[0]


def _modulation(cc, w_mod, b_mod):
    depth, _, n = w_mod.shape
    rows = cc.shape[0]
    return pl.pallas_call(
        _mod_kernel,
        out_shape=jax.ShapeDtypeStruct((depth, rows, n), F32),
        grid=(depth, n // MOD_COLS),
        in_specs=[
            pl.BlockSpec((rows, D_MODEL), lambda l, j: (0, 0)),
            pl.BlockSpec((1, D_MODEL, MOD_COLS), lambda l, j: (l, 0, j)),
            pl.BlockSpec((1, 1, MOD_COLS), lambda l, j: (l, 0, j)),
        ],
        out_specs=pl.BlockSpec((1, rows, MOD_COLS), lambda l, j: (l, 0, j)),
        compiler_params=_params("arbitrary", "arbitrary"),
        name="modulation",
    )(cc, w_mod, b_mod.reshape(depth, 1, n))


def _head_sumsq(x, ones_bd):
    sq = x * x
    hi = sq.astype(BF16)
    lo = (sq - hi.astype(F32)).astype(BF16)
    return _dot(hi, ones_bd) + _dot(lo, ones_bd)


def _norm_rope(x, gain, cos, sin, ones_bd, lane, scale):
    y = x * lax.rsqrt(_head_sumsq(x, ones_bd) * (1.0 / HD_ATT) + NORM_EPS) * gain
    above = pltpu.roll(y, LANES - 16, 1)
    below = pltpu.roll(y, 16, 1)
    partner = jnp.where((lane & 16) == 0, above, below)
    return (y * cos + partner * sin) * scale


def _proj_in_kernel(x_ref, mod_ref, w_ref, b_ref, cos_ref, sin_ref, qg_ref, kg_ref,
                    ret_ref, attq_ref, attk_ref, attvt_ref, mls_ref, gate_ref, merge_ref):
    tm = x_ref.shape[0]
    h = (_layer_norm(x_ref[...]) * (1.0 + mod_ref[1:2, :]) + mod_ref[0:1, :]).astype(BF16)

    def seg(lo, hi):
        return _dot(h, w_ref[:, lo:hi]) + b_ref[:, lo:hi]

    def store_qkvx(dst_ref, r):
        dst_ref[:, 0:256] = r[:, 0:256].astype(BF16)
        dst_ref[:, 256:512] = (r[:, 256:512] * 0.125).astype(BF16)
        dst_ref[:, 512:1536] = r[:, 512:1536].astype(BF16)

    store_qkvx(ret_ref, seg(0, RET_W))
    store_qkvx(mls_ref, seg(OFF_MLS, OFF_MLS + MLS_W))
    gate_ref[...] = seg(OFF_GATE, OFF_GATE + GATE_W)
    half = MERGE_W // 2
    merge_ref[:, 0:half] = seg(OFF_MERGE, OFF_MERGE + half).astype(BF16)
    merge_ref[:, half:MERGE_W] = seg(OFF_MERGE + half, OFF_MERGE + MERGE_W).astype(BF16)

    a = seg(OFF_ATT, OFF_ATT + ATT_W)
    lane = lax.broadcasted_iota(jnp.int32, (tm, LANES), 1)
    ri = lax.broadcasted_iota(jnp.int32, (LANES, LANES), 0)
    ci = lax.broadcasted_iota(jnp.int32, (LANES, LANES), 1)
    ones_bd = jnp.where((ri // HD_ATT) == (ci // HD_ATT), 1.0, 0.0).astype(BF16)
    cos = cos_ref[...]
    sin = sin_ref[...]
    qg = qg_ref[...]
    for j in range(H_ATT * HD_ATT // LANES):
        qj = _norm_rope(a[:, j * LANES:(j + 1) * LANES], qg, cos, sin, ones_bd, lane,
                        HD_ATT ** -0.5 * LOG2_E)
        attq_ref[:, j * LANES:(j + 1) * LANES] = qj.astype(BF16)
    kn = _norm_rope(a[:, 512:640], kg_ref[...], cos, sin, ones_bd, lane, 1.0)
    kr = pltpu.roll(kn, HD_ATT, 1)
    low = lane < HD_ATT
    attk_ref[:, 0:128] = jnp.where(low, kn, kr).astype(BF16)
    attk_ref[:, 128:256] = jnp.where(low, kr, kn).astype(BF16)
    attvt_ref[0] = jnp.transpose(a[:, 640:768]).astype(BF16)


def _row_maps(t_len, ctx_len, tm):
    nb = t_len // tm
    cb = ctx_len // tm

    def mod_map(i):
        return ((i // nb) * 2 + ((i % nb) >= cb).astype(jnp.int32), 0, 0)

    def pos_map(i):
        return (i % nb, 0)

    return nb, cb, mod_map, pos_map


def _proj_in(xs, mod_tab, w, b, cos_t, sin_t, qg, kg, t_len, ctx_len):
    rows = xs.shape[0]
    tm = ROW_TILE
    _, _, mod_map, pos_map = _row_maps(t_len, ctx_len, tm)
    row_spec = lambda n: pl.BlockSpec((tm, n), lambda i: (i, 0))
    out_shapes = [
        jax.ShapeDtypeStruct((rows, RET_W), BF16),
        jax.ShapeDtypeStruct((rows, H_ATT * HD_ATT), BF16),
        jax.ShapeDtypeStruct((rows, 2 * LANES), BF16),
        jax.ShapeDtypeStruct((rows // tm, KV_ATT * HD_ATT, tm), BF16),
        jax.ShapeDtypeStruct((rows, MLS_W), BF16),
        jax.ShapeDtypeStruct((rows, GATE_W), F32),
        jax.ShapeDtypeStruct((rows, MERGE_W), BF16),
    ]

    def out_spec(s):
        if len(s.shape) == 3:
            return pl.BlockSpec((1,) + s.shape[1:], lambda i: (i, 0, 0))
        return row_spec(s.shape[1])

    return pl.pallas_call(
        _proj_in_kernel,
        out_shape=out_shapes,
        grid=(rows // tm,),
        in_specs=[
            row_spec(D_MODEL),
            pl.BlockSpec((None, 6, D_MODEL), mod_map),
            _const_spec((D_MODEL, N_PROJ)),
            _const_spec((1, N_PROJ)),
            pl.BlockSpec((tm, LANES), pos_map),
            pl.BlockSpec((tm, LANES), pos_map),
            _const_spec((1, LANES)),
            _const_spec((1, LANES)),
        ],
        out_specs=[out_spec(s) for s in out_shapes],
        compiler_params=_params("arbitrary"),
        name="proj_in",
    )(xs, mod_tab, w, b, cos_t, sin_t, qg, kg)


def _scan_chunk(d, j, n_ctx, n_chunks):
    if d == 0:
        return j
    return jnp.where(j < n_ctx, n_ctx - 1 - j, n_chunks + n_ctx - 1 - j)


def _head_norm(o):
    mu = jnp.mean(o, axis=-1, keepdims=True)
    oc = o - mu
    var = jnp.mean(oc * oc, axis=-1, keepdims=True)
    return oc * lax.rsqrt(var + NORM_EPS)


def _ret_kernel(p_ref, dl_ref, gn_ref, y_ref, o_scr, s_scr, intra_scr, qd_scr, kd_scr, cd_scr,
                *, n_ctx, n_chunks):
    lg = _log_sigmoid(dl_ref[...])
    cd_scr[...] = jnp.exp(float(CHUNK) * lg)
    ri = lax.broadcasted_iota(jnp.int32, (CHUNK, CHUNK), 0).astype(F32)
    ci = lax.broadcasted_iota(jnp.int32, (CHUNK, CHUNK), 1).astype(F32)
    for d in range(2):
        diff = ri - ci if d == 0 else ci - ri
        qpos = ri + 1.0 if d == 0 else float(CHUNK) - ri
        kpos = float(CHUNK - 1) - ri if d == 0 else ri
        keep = diff >= 0.0
        for h in range(H_RET):
            r = d * H_RET + h
            lgb = lg[r:r + 1, :]
            intra_scr[r] = jnp.where(keep, jnp.exp(jnp.where(keep, diff, 0.0) * lgb), 0.0)
            qd_scr[r] = jnp.exp(qpos * lgb)
            kd_scr[r] = jnp.exp(kpos * lgb)

    lane = lax.broadcasted_iota(jnp.int32, (CHUNK, LANES), 1)
    half_mask = (lane < DK_RET, lane >= DK_RET)

    s_scr[...] = jnp.zeros_like(s_scr)
    o_scr[...] = jnp.zeros_like(o_scr)

    def body(j, carry):
        staged = []
        for d in range(2):
            c = _scan_chunk(d, j, n_ctx, n_chunks)
            rows = pl.ds(pl.multiple_of(c * CHUNK, CHUNK), CHUNK)
            for pair in range(H_RET // 2):
                qp = p_ref[rows, pair * LANES:(pair + 1) * LANES]
                kp = p_ref[rows, 256 + pair * LANES:256 + (pair + 1) * LANES]
                kpf = kp.astype(F32)
                for hh in range(2):
                    h = pair * 2 + hh
                    r = d * H_RET + h
                    qm = jnp.where(half_mask[hh], qp, jnp.zeros_like(qp))
                    v = p_ref[rows, 512 + h * DV_RET:512 + (h + 1) * DV_RET]
                    s = (_dot_nt(qm, kp) * intra_scr[r]).astype(BF16)
                    state = s_scr[r]
                    inter = _dot(qm, state.astype(BF16)) * qd_scr[r]
                    kdec = (kpf * kd_scr[r]).astype(BF16)
                    s_scr[r] = state * cd_scr[r:r + 1, :] + _dot_tn(kdec, v)
                    staged.append((rows, h, s, v, inter))
        for rows, h, s, v, inter in staged:
            cols = slice(h * DV_RET, (h + 1) * DV_RET)
            tot = o_scr[rows, cols] + (_dot(s, v) + inter)
            o_scr[rows, cols] = tot
            g = p_ref[rows, 1024 + h * DV_RET:1024 + (h + 1) * DV_RET].astype(F32)
            y = _head_norm(tot) * gn_ref[:, cols] * (g * _sigmoid(g))
            y_ref[rows, cols] = y.astype(BF16)
        return carry

    lax.fori_loop(0, n_chunks, body, 0)


def _retention(ret_p, dl, gn, batch, t_len, ctx_len):
    n_chunks = t_len // CHUNK
    kern = functools.partial(_ret_kernel, n_ctx=ctx_len // CHUNK, n_chunks=n_chunks)
    width = H_RET * DV_RET
    return pl.pallas_call(
        kern,
        out_shape=jax.ShapeDtypeStruct((batch * t_len, width), BF16),
        grid=(batch,),
        in_specs=[
            pl.BlockSpec((t_len, RET_W), lambda b: (b, 0)),
            _const_spec((2 * H_RET, LANES)),
            _const_spec((1, width)),
        ],
        out_specs=pl.BlockSpec((t_len, width), lambda b: (b, 0)),
        scratch_shapes=[
            pltpu.VMEM((t_len, width), F32),
            pltpu.VMEM((2 * H_RET, LANES, DV_RET), F32),
            pltpu.VMEM((2 * H_RET, CHUNK, CHUNK), F32),
            pltpu.VMEM((2 * H_RET, CHUNK, CHUNK), F32),
            pltpu.VMEM((2 * H_RET, CHUNK, CHUNK), F32),
            pltpu.VMEM((2 * H_RET, LANES), F32),
        ],
        compiler_params=_params("arbitrary"),
        name="retention",
    )(ret_p, dl, gn)


def _mlstm_kernel(p_ref, gate_ref, gn_ref, y_ref, h_scr, cn_scr, m_scr, *, n_ctx, n_chunks):
    ri = lax.broadcasted_iota(jnp.int32, (CHUNK, CHUNK), 0)
    ci = lax.broadcasted_iota(jnp.int32, (CHUNK, CHUNK), 1)
    lane = lax.broadcasted_iota(jnp.int32, (CHUNK, LANES), 1)
    half_mask = (lane < DK_M, lane >= DK_M)
    ones_v = jnp.ones((CHUNK, LANES), BF16)

    cn_scr[...] = jnp.zeros_like(cn_scr)
    m_scr[...] = jnp.zeros_like(m_scr)
    h_scr[...] = jnp.zeros_like(h_scr)
    keeps = (ri >= ci, ci >= ri)
    tris = tuple(jnp.where(k, 1.0, 0.0).astype(BF16) for k in keeps)
    full = (CHUNK, LANES)

    def body(j, carry):
        staged = []
        for d in range(2):
            keep = keeps[d]
            end_row = CHUNK - 1 if d == 0 else 0
            c = _scan_chunk(d, j, n_ctx, n_chunks)
            rows = pl.ds(pl.multiple_of(c * CHUNK, CHUNK), CHUNK)
            gates = gate_ref[rows, :]
            hi, mid, lo = _split3(_log_sigmoid(gates))
            bcum = _dot(tris[d], hi) + _dot(tris[d], mid) + _dot(tris[d], lo)
            cdiff = gates - pltpu.roll(bcum, LANES - 2 * H_M, 1)
            cdiff_t = jnp.transpose(cdiff)
            for pair in range(H_M // 2):
                qp = p_ref[rows, pair * LANES:(pair + 1) * LANES]
                kp = p_ref[rows, 256 + pair * LANES:256 + (pair + 1) * LANES]
                kpf = kp.astype(F32)
                for hh in range(2):
                    h = pair * 2 + hh
                    r = d * H_M + h
                    gb = 2 * H_M + r
                    b_l = jnp.broadcast_to(bcum[:, gb:gb + 1], full)
                    c_l = jnp.broadcast_to(cdiff[:, r:r + 1], full)
                    c_row = cdiff_t[r:r + 1, :]
                    b_end = b_l[end_row:end_row + 1, :]
                    m_prev = m_scr[r][0:1, :]
                    qm = jnp.where(half_mask[hh], qp, jnp.zeros_like(qp))
                    v = p_ref[rows, 512 + h * DV_M:512 + (h + 1) * DV_M]
                    vext = jnp.concatenate([v, ones_v], axis=1)

                    dmat = jnp.where(keep, b_l + c_row, -jnp.inf)
                    inter = b_l + m_prev
                    row_max = jnp.broadcast_to(jnp.max(dmat, axis=-1, keepdims=True), full)
                    m_t = jnp.maximum(inter, row_max)
                    s = (_dot_nt(qm, kp) * jnp.exp(dmat - m_t)).astype(BF16)
                    a_int = jnp.exp(inter - m_t)
                    state = cn_scr[r]
                    qcn = _dot(qm, state.astype(BF16))

                    g_l = b_end + c_l
                    m_new = jnp.maximum(b_end + m_prev, jnp.max(g_l, axis=0, keepdims=True))
                    kdec = (kpf * jnp.exp(g_l - m_new)).astype(BF16)
                    carry_scale = jnp.exp(b_end + m_prev - m_new)
                    cn_scr[r] = (jnp.concatenate([carry_scale, carry_scale], axis=1) * state
                                 + _dot_tn(kdec, vext))
                    m_scr[r] = jnp.broadcast_to(m_new, (SUBLANES, LANES))
                    staged.append((rows, h, s, vext, a_int, qcn, m_t))
        for rows, h, s, vext, a_int, qcn, m_t in staged:
            cols = slice(h * DV_M, (h + 1) * DV_M)
            sv = _dot(s, vext)
            num = sv[:, 0:DV_M] + a_int * qcn[:, 0:DV_M]
            den = sv[:, DV_M:2 * DV_M] + a_int * qcn[:, DV_M:2 * DV_M]
            hval = num / jnp.maximum(jnp.abs(den), jnp.exp(-m_t))
            tot = h_scr[rows, cols] + hval
            h_scr[rows, cols] = tot
            og = p_ref[rows, 1024 + h * DV_M:1024 + (h + 1) * DV_M].astype(F32)
            y = _sigmoid(og) * (_head_norm(tot) * gn_ref[:, cols])
            y_ref[rows, cols] = y.astype(BF16)
        return carry

    lax.fori_loop(0, n_chunks, body, 0)


def _mlstm(mls_p, gates, gn, batch, t_len, ctx_len):
    n_chunks = t_len // CHUNK
    kern = functools.partial(_mlstm_kernel, n_ctx=ctx_len // CHUNK, n_chunks=n_chunks)
    width = H_M * DV_M
    return pl.pallas_call(
        kern,
        out_shape=jax.ShapeDtypeStruct((batch * t_len, width), BF16),
        grid=(batch,),
        in_specs=[
            pl.BlockSpec((t_len, MLS_W), lambda b: (b, 0)),
            pl.BlockSpec((t_len, GATE_W), lambda b: (b, 0)),
            _const_spec((1, width)),
        ],
        out_specs=pl.BlockSpec((t_len, width), lambda b: (b, 0)),
        scratch_shapes=[
            pltpu.VMEM((t_len, width), F32),
            pltpu.VMEM((2 * H_M, LANES, 2 * LANES), F32),
            pltpu.VMEM((2 * H_M, SUBLANES, LANES), F32),
        ],
        compiler_params=_params("arbitrary"),
        name="mlstm",
    )(mls_p, gates, gn)


def _attn_kernel(bound_ref, q_ref, k_ref, vt_ref, o_ref, m_scr, alpha_scr, acc_scr, p_scr, s_scr,
                 *, ctx_len, t_len, bounded):
    tq = q_ref.shape[0]
    tk = vt_ref.shape[2]
    rep = H_ATT // KV_ATT
    is_ctx = pl.program_id(1) < ctx_len // tq
    n_tiles = jnp.where(is_ctx, ctx_len // tk, t_len // tk)
    n_pairs = jnp.where(is_ctx, ctx_len // tk // 2, t_len // tk // 2)
    lane = lax.broadcasted_iota(jnp.int32, (Q_SUB, LANES), 1)
    low = lane < HD_ATT
    ones_rows = jnp.ones((ATT_ONES_ROWS, tk), BF16)
    bound = bound_ref[0]

    chains = [(g, qb) for g in range(KV_ATT) for qb in range(tq // Q_SUB)]
    qs = []
    for g, qb in chains:
        parts = []
        for hh in range(rep):
            h = g * rep + hh
            qp = q_ref[qb * Q_SUB:(qb + 1) * Q_SUB, (h // 2) * LANES:(h // 2 + 1) * LANES]
            parts.append(jnp.where(low if h % 2 == 0 else ~low, qp, jnp.zeros_like(qp)))
        qs.append(jnp.concatenate(parts, axis=0))
    m_scr[...] = jnp.full_like(m_scr, -jnp.inf)
    acc_scr[...] = jnp.zeros_like(acc_scr)
    p_scr[1] = jnp.zeros_like(p_scr[1])
    alpha_scr[...] = jnp.ones_like(alpha_scr)

    def score_tile(t, slot):
        keys = pl.ds(pl.multiple_of(t * tk, tk), tk)
        for c, (g, _) in enumerate(chains):
            s_scr[slot, c] = _dot_nt(k_ref[keys, g * LANES:(g + 1) * LANES], qs[c])

    def accumulate(t_prev, slot):
        for c, (g, _) in enumerate(chains):
            v_ext = jnp.concatenate([vt_ref[t_prev][g * HD_ATT:(g + 1) * HD_ATT, :], ones_rows], axis=0)
            pv = _dot(v_ext, p_scr[slot, c])
            if bounded:
                acc_scr[c] += pv
            else:
                acc_scr[c] = alpha_scr[slot, c, 0:1, :] * acc_scr[c] + pv

    def stage(t, cur):
        oth = 1 - cur
        score_tile(jnp.minimum(t + 1, n_tiles - 1), oth)
        for c in range(len(chains)):
            s_t = s_scr[cur, c]
            if bounded:
                p_scr[cur, c] = jnp.exp2(s_t - bound).astype(BF16)
            else:
                m_old = m_scr[c, 0:1, :]
                m_new = jnp.maximum(m_old, jnp.max(s_t, axis=0, keepdims=True))
                alpha_scr[cur, c, 0:1, :] = jnp.exp2(m_old - m_new)
                p_scr[cur, c] = jnp.exp2(s_t - m_new).astype(BF16)
                m_scr[c, 0:1, :] = m_new
        accumulate(jnp.maximum(t - 1, 0), oth)

    def two_tiles(i, carry):
        stage(2 * i, 0)
        stage(2 * i + 1, 1)
        return carry

    score_tile(0, 0)
    lax.fori_loop(0, n_pairs, two_tiles, 0)
    stage(n_tiles - 1, 0)
    accumulate(n_tiles - 1, 0)

    for c, (g, qb) in enumerate(chains):
        acc = acc_scr[c]
        out_t = acc[0:HD_ATT] / acc[HD_ATT:HD_ATT + 1]
        for pp in range(rep // 2):
            pair_t = jnp.concatenate([out_t[:, (2 * pp) * Q_SUB:(2 * pp + 1) * Q_SUB],
                                      out_t[:, (2 * pp + 1) * Q_SUB:(2 * pp + 2) * Q_SUB]], axis=0)
            col = (g * rep // 2 + pp) * LANES
            o_ref[qb * Q_SUB:(qb + 1) * Q_SUB, col:col + LANES] = jnp.transpose(pair_t).astype(BF16)


def _attention(att_q, att_k, att_vt, bound, batch, t_len, ctx_len, bounded):
    tq = Q_TILE
    nq = t_len // tq
    rep = H_ATT // KV_ATT
    slabs, vrows, tk = att_vt.shape
    per_batch = slabs // batch
    n_chains = KV_ATT * (tq // Q_SUB)
    kern = functools.partial(_attn_kernel, ctx_len=ctx_len, t_len=t_len, bounded=bounded)
    width = H_ATT * HD_ATT
    return pl.pallas_call(
        kern,
        out_shape=jax.ShapeDtypeStruct((batch * t_len, width), BF16),
        grid=(batch, nq),
        in_specs=[
            pl.BlockSpec(memory_space=pltpu.SMEM),
            pl.BlockSpec((tq, width), lambda b, i: (b * nq + i, 0)),
            pl.BlockSpec((t_len, 2 * LANES), lambda b, i: (b, 0)),
            pl.BlockSpec((per_batch, vrows, tk), lambda b, i: (b, 0, 0)),
        ],
        out_specs=pl.BlockSpec((tq, width), lambda b, i: (b * nq + i, 0)),
        scratch_shapes=[
            pltpu.VMEM((n_chains, SUBLANES, rep * Q_SUB), F32),
            pltpu.VMEM((2, n_chains, SUBLANES, rep * Q_SUB), F32),
            pltpu.VMEM((n_chains, HD_ATT + ATT_ONES_ROWS, rep * Q_SUB), F32),
            pltpu.VMEM((2, n_chains, tk, rep * Q_SUB), BF16),
            pltpu.VMEM((2, n_chains, tk, rep * Q_SUB), F32),
        ],
        compiler_params=_params("arbitrary", "arbitrary"),
        name="attention_bounded" if bounded else "attention_online",
    )(bound, att_q, att_k, att_vt)


def _merge_kernel(x_ref, mod_ref, yr_ref, ya_ref, ym_ref, mg_ref, wr_ref, wa_ref, wm_ref, wo_ref,
                  lng_ref, lnb_ref, o_ref, *, alpha):
    def branch(y_ref, w_ref, k):
        gate = mg_ref[:, k * D_MODEL:(k + 1) * D_MODEL].astype(F32)
        return _sigmoid(gate) * _dot(y_ref[...], w_ref[...])

    z = branch(yr_ref, wr_ref, 0) + branch(ya_ref, wa_ref, 1) + branch(ym_ref, wm_ref, 2)
    mix = _dot(z.astype(BF16), wo_ref[...])
    y = alpha * x_ref[...] + mod_ref[2:3, :] * mix
    o_ref[...] = _layer_norm(y) * lng_ref[...] + lnb_ref[...]


def _merge(xs, mod_tab, y_ret, y_att, y_m, mg, w_r, w_a, w_m, w_o, ln_g, ln_b, t_len, ctx_len, alpha):
    rows = xs.shape[0]
    tm = ROW_TILE
    _, _, mod_map, _ = _row_maps(t_len, ctx_len, tm)
    row_spec = lambda n: pl.BlockSpec((tm, n), lambda i: (i, 0))
    bw = y_ret.shape[1]
    return pl.pallas_call(
        functools.partial(_merge_kernel, alpha=alpha),
        out_shape=jax.ShapeDtypeStruct((rows, D_MODEL), F32),
        grid=(rows // tm,),
        in_specs=[
            row_spec(D_MODEL),
            pl.BlockSpec((None, 6, D_MODEL), mod_map),
            row_spec(bw), row_spec(bw), row_spec(bw), row_spec(MERGE_W),
            _const_spec((bw, D_MODEL)), _const_spec((bw, D_MODEL)), _const_spec((bw, D_MODEL)),
            _const_spec((D_MODEL, D_MODEL)),
            _const_spec((1, D_MODEL)), _const_spec((1, D_MODEL)),
        ],
        out_specs=row_spec(D_MODEL),
        compiler_params=_params("arbitrary"),
        name="merge",
    )(xs, mod_tab, y_ret, y_att, y_m, mg, w_r, w_a, w_m, w_o, ln_g, ln_b)


def _ffn_kernel(x_ref, xp_ref, xn_ref, mod_ref, wu_ref, cw_ref, cb_ref, wd_ref, lng_ref, lnb_ref,
                o_ref, u_scr, *, alpha, nb, cb):
    tm = x_ref.shape[0]
    j = pl.program_id(0) % nb
    prev_ok = jnp.logical_and(j != 0, j != cb)
    next_ok = jnp.logical_and(j != cb - 1, j != nb - 1)
    shift = mod_ref[3:4, :]
    scale = 1.0 + mod_ref[4:5, :]

    def mod(xv):
        return _layer_norm(xv) * scale + shift

    x = x_ref[...]
    hp = jnp.where(prev_ok, mod(xp_ref[...]), 0.0)
    hn = jnp.where(next_ok, mod(xn_ref[...]), 0.0)
    hext = jnp.concatenate([hp, mod(x), hn], axis=0).astype(BF16)

    def conv(lo, hi):
        w = cw_ref[:, lo:hi]
        return (u_scr[SUBLANES - 1:SUBLANES - 1 + tm, lo:hi] * w[0:1]
                + u_scr[SUBLANES:SUBLANES + tm, lo:hi] * w[1:2]
                + u_scr[SUBLANES + 1:SUBLANES + 1 + tm, lo:hi] * w[2:3] + cb_ref[:, lo:hi])

    def up(c):
        for lo in (c * FF_CHUNK, D_FF + c * FF_CHUNK):
            u_scr[:, lo:lo + FF_CHUNK] = _dot(hext, wu_ref[:, lo:lo + FF_CHUNK])

    n_chunks = D_FF // FF_CHUNK
    acc = jnp.zeros((tm, D_MODEL), F32)
    up(0)
    for c in range(n_chunks):
        if c + 1 < n_chunks:
            up(c + 1)
        lo, hi = c * FF_CHUNK, (c + 1) * FF_CHUNK
        a = conv(lo, hi)
        g = conv(D_FF + lo, D_FF + hi)
        act = (g * _sigmoid(g) * a).astype(BF16)
        acc = acc + _dot(act, wd_ref[lo:hi, :])
    y = alpha * x + mod_ref[5:6, :] * acc
    o_ref[...] = _layer_norm(y) * lng_ref[...] + lnb_ref[...]


def _ffn(xs, mod_tab, w_up, conv_w, conv_b, w_down, ln_g, ln_b, t_len, ctx_len, alpha):
    rows = xs.shape[0]
    tm = ROW_TILE
    nb, cb, mod_map, _ = _row_maps(t_len, ctx_len, tm)
    per = tm // SUBLANES
    last = rows // SUBLANES - 1
    return pl.pallas_call(
        functools.partial(_ffn_kernel, alpha=alpha, nb=nb, cb=cb),
        out_shape=jax.ShapeDtypeStruct((rows, D_MODEL), F32),
        grid=(rows // tm,),
        in_specs=[
            pl.BlockSpec((tm, D_MODEL), lambda i: (i, 0)),
            pl.BlockSpec((SUBLANES, D_MODEL), lambda i: (jnp.maximum(i * per - 1, 0), 0)),
            pl.BlockSpec((SUBLANES, D_MODEL), lambda i: (jnp.minimum((i + 1) * per, last), 0)),
            pl.BlockSpec((None, 6, D_MODEL), mod_map),
            _const_spec((D_MODEL, 2 * D_FF)),
            _const_spec((3, 2 * D_FF)),
            _const_spec((1, 2 * D_FF)),
            _const_spec((D_FF, D_MODEL)),
            _const_spec((1, D_MODEL)), _const_spec((1, D_MODEL)),
        ],
        out_specs=pl.BlockSpec((tm, D_MODEL), lambda i: (i, 0)),
        scratch_shapes=[pltpu.VMEM((tm + 2 * SUBLANES, 2 * D_FF), F32)],
        compiler_params=_params("arbitrary"),
        name="conv_ffn",
    )(xs, xs, xs, mod_tab, w_up, conv_w, conv_b, w_down, ln_g, ln_b)


def _rope_tables(seq, ctx_len):
    rows = seq // GRID_W
    row = jnp.repeat(jnp.arange(rows), GRID_W).astype(F32)
    col = jnp.tile(jnp.arange(GRID_W), rows).astype(F32)
    n_freq = HD_ATT // 4
    freqs = ROPE_THETA ** (-jnp.arange(n_freq, dtype=F32) / n_freq)
    ar = row[:, None] * freqs[None, :]
    ac = col[:, None] * freqs[None, :]
    cos = jnp.concatenate([jnp.cos(ar), jnp.cos(ar), jnp.cos(ac), jnp.cos(ac)], axis=-1)
    sin = jnp.concatenate([-jnp.sin(ar), jnp.sin(ar), -jnp.sin(ac), jnp.sin(ac)], axis=-1)
    cos = jnp.concatenate([jnp.ones((ctx_len, HD_ATT), F32), cos], axis=0)
    sin = jnp.concatenate([jnp.zeros((ctx_len, HD_ATT), F32), sin], axis=0)
    return jnp.tile(cos, (1, 2)), jnp.tile(sin, (1, 2))


def kernel(x, c, ctx, c_ctx, w_mod, b_mod, w_in, b_in, ret_decay_logit, ret_gn_g, attn_qn_g, attn_kn_g,
           mlstm_gn_g, w_br_ret, w_br_att, w_br_mlstm, w_out, ln1_g, ln1_b, w_up, conv_w, conv_b, w_down,
           ln2_g, ln2_b):
    batch, seq, _ = x.shape
    ctx_len = ctx.shape[1]
    depth = w_mod.shape[0]
    t_len = ctx_len + seq
    alpha = (2.0 * depth) ** 0.25
    assert ctx_len % ROW_TILE == 0 and seq % ROW_TILE == 0
    assert (ctx_len // ROW_TILE) % 2 == 1 and (t_len // ROW_TILE) % 2 == 1
    assert seq % GRID_W == 0 and batch + 1 <= SUBLANES

    xs = jnp.concatenate([ctx, x], axis=1).reshape(batch * t_len, D_MODEL)

    cc = jnp.zeros((SUBLANES, D_MODEL), F32).at[:batch].set(c).at[batch].set(c_ctx)
    mod = _modulation(cc, w_mod, b_mod)
    mod_lat = mod[:, :batch].reshape(depth, batch, 1, 6, D_MODEL)
    mod_ctx = jnp.broadcast_to(mod[:, batch].reshape(depth, 1, 1, 6, D_MODEL),
                               (depth, batch, 1, 6, D_MODEL))
    mod_tab = jnp.concatenate([mod_ctx, mod_lat], axis=2).reshape(depth, batch * 2, 6, D_MODEL)

    cos_t, sin_t = _rope_tables(seq, ctx_len)
    n_gate = 4 * H_M
    split = OFF_GATE + n_gate

    def pad_proj(a):
        pad = jnp.zeros(a.shape[:-1] + (GATE_W - n_gate,), a.dtype)
        return jnp.concatenate([a[..., :split], pad, a[..., split:]], axis=-1)

    w_in_p = pad_proj(w_in).astype(BF16)
    b_in_p = pad_proj(b_in).reshape(depth, 1, N_PROJ)
    dl = jnp.broadcast_to(ret_decay_logit.reshape(depth, 2 * H_RET, 1), (depth, 2 * H_RET, LANES))
    qg = jnp.tile(attn_qn_g, (1, 2)).reshape(depth, 1, LANES)
    kg = jnp.tile(attn_kn_g, (1, 2)).reshape(depth, 1, LANES)
    w_r, w_a, w_m, w_o = (w.astype(BF16) for w in (w_br_ret, w_br_att, w_br_mlstm, w_out))
    w_up_b = w_up.astype(BF16)
    w_down_b = w_down.astype(BF16)
    row2 = lambda a, l: a[l].reshape(1, -1)

    for l in range(depth):
        ret_p, att_q, att_k, att_vt, mls_p, gates, mg = _proj_in(
            xs, mod_tab[l], w_in_p[l], b_in_p[l], cos_t, sin_t, qg[l], kg[l], t_len, ctx_len)
        y_ret = _retention(ret_p, dl[l], row2(ret_gn_g, l), batch, t_len, ctx_len)
        bound = (1.01 * LOG2_E * HD_ATT ** 0.5 * jnp.max(jnp.abs(attn_qn_g[l]))
                 * jnp.max(jnp.abs(attn_kn_g[l]))).reshape(1)
        attend = lambda fixed: functools.partial(
            _attention, batch=batch, t_len=t_len, ctx_len=ctx_len, bounded=fixed)
        y_att = lax.cond(bound[0] <= ATT_SAFE_BOUND, attend(True), attend(False),
                         att_q, att_k, att_vt, bound)
        y_m = _mlstm(mls_p, gates, row2(mlstm_gn_g, l), batch, t_len, ctx_len)
        xs = _merge(xs, mod_tab[l], y_ret, y_att, y_m, mg, w_r[l], w_a[l], w_m[l], w_o[l],
                    row2(ln1_g, l), row2(ln1_b, l), t_len, ctx_len, alpha)
        xs = _ffn(xs, mod_tab[l], w_up_b[l], conv_w[l], row2(conv_b, l), w_down_b[l],
                  row2(ln2_g, l), row2(ln2_b, l), t_len, ctx_len, alpha)

    return xs.reshape(batch, t_len, D_MODEL)[:, ctx_len:]
```

```python
import functools

import jax
import jax.numpy as jnp
from jax import lax
from jax.experimental import pallas as pl
from jax.experimental.pallas import tpu as pltpu

F32 = jnp.float32
BF16 = jnp.bfloat16

D_MODEL = 1024
H_RET, DK_RET, DV_RET = 4, 64, 128
H_ATT, KV_ATT, HD_ATT = 8, 2, 64
H_M, DK_M, DV_M = 4, 64, 128
CHUNK = 128
D_FF = 2816
GRID_W = 64
ROPE_THETA = 10000.0
NORM_EPS = 1e-6
LOG2_E = 1.4426950408889634

LANES = 128
SUBLANES = 8
ROW_TILE = 256
Q_TILE = 256
Q_SUB = 128
ATT_SAFE_BOUND = 40.0
ATT_SLABS = 2
ATT_ONES_ROWS = 16
FF_CHUNK = 256
MOD_COLS = 1536
VMEM_LIMIT = 56 * 1024 * 1024

RET_W = 2 * H_RET * DK_RET + 2 * H_RET * DV_RET
ATT_W = (H_ATT + 2 * KV_ATT) * HD_ATT
MLS_W = 2 * H_M * DK_M + 2 * H_M * DV_M
GATE_W = LANES
MERGE_W = 3 * D_MODEL
OFF_ATT = RET_W
OFF_MLS = OFF_ATT + ATT_W
OFF_GATE = OFF_MLS + MLS_W
OFF_MERGE = OFF_GATE + GATE_W
N_PROJ = OFF_MERGE + MERGE_W


def _dot(a, b):
    return jnp.dot(a, b, preferred_element_type=F32)


def _dot_nt(a, b):
    return lax.dot_general(a, b, (((1,), (1,)), ((), ())), preferred_element_type=F32)


def _dot_tn(a, b):
    return lax.dot_general(a, b, (((0,), (0,)), ((), ())), preferred_element_type=F32)


def _layer_norm(x):
    mu = jnp.mean(x, axis=-1, keepdims=True)
    xc = x - mu
    var = jnp.mean(xc * xc, axis=-1, keepdims=True)
    return xc * lax.rsqrt(var + NORM_EPS)


def _log_sigmoid(x):
    return jnp.minimum(x, 0.0) - jnp.log1p(jnp.exp(-jnp.abs(x)))


def _sigmoid(x):
    return 1.0 / (1.0 + jnp.exp(-x))


def _split3(x):
    hi = x.astype(BF16)
    r1 = x - hi.astype(F32)
    mid = r1.astype(BF16)
    lo = (r1 - mid.astype(F32)).astype(BF16)
    return hi, mid, lo


def _const_spec(shape):
    zeros = (0,) * len(shape)
    return pl.BlockSpec(shape, lambda *_: zeros, pipeline_mode=pl.Buffered(1))


def _params(*sem):
    return pltpu.CompilerParams(dimension_semantics=sem, vmem_limit_bytes=VMEM_LIMIT)


def _mod_kernel(c_ref, w_ref, b_ref, o_ref):
    c = c_ref[...]
    s = (c * _sigmoid(c)).astype(BF16)
    o_ref[0] = _dot(s, w_ref[0].astype(BF16)) + b_ref[0]


def _modulation(cc, w_mod, b_mod):
    depth, _, n = w_mod.shape
    rows = cc.shape[0]
    return pl.pallas_call(
        _mod_kernel,
        out_shape=jax.ShapeDtypeStruct((depth, rows, n), F32),
        grid=(depth, n // MOD_COLS),
        in_specs=[
            pl.BlockSpec((rows, D_MODEL), lambda l, j: (0, 0)),
            pl.BlockSpec((1, D_MODEL, MOD_COLS), lambda l, j: (l, 0, j)),
            pl.BlockSpec((1, 1, MOD_COLS), lambda l, j: (l, 0, j)),
        ],
        out_specs=pl.BlockSpec((1, rows, MOD_COLS), lambda l, j: (l, 0, j)),
        compiler_params=_params("arbitrary", "arbitrary"),
        name="modulation",
    )(cc, w_mod, b_mod.reshape(depth, 1, n))


def _head_sumsq(x, ones_bd):
    sq = x * x
    hi = sq.astype(BF16)
    lo = (sq - hi.astype(F32)).astype(BF16)
    return _dot(hi, ones_bd) + _dot(lo, ones_bd)


def _norm_rope(x, gain, cos, sin, ones_bd, lane, scale):
    y = x * lax.rsqrt(_head_sumsq(x, ones_bd) * (1.0 / HD_ATT) + NORM_EPS) * gain
    above = pltpu.roll(y, LANES - 16, 1)
    below = pltpu.roll(y, 16, 1)
    partner = jnp.where((lane & 16) == 0, above, below)
    return (y * cos + partner * sin) * scale


def _proj_in_kernel(x_ref, mod_ref, w_ref, b_ref, cos_ref, sin_ref, qg_ref, kg_ref,
                    ret_ref, attq_ref, attk_ref, attvt_ref, mls_ref, gate_ref, merge_ref):
    tm = x_ref.shape[0]
    h = (_layer_norm(x_ref[...]) * (1.0 + mod_ref[1:2, :]) + mod_ref[0:1, :]).astype(BF16)

    def seg(lo, hi):
        return _dot(h, w_ref[:, lo:hi]) + b_ref[:, lo:hi]

    def store_qkvx(dst_ref, r):
        dst_ref[:, 0:256] = r[:, 0:256].astype(BF16)
        dst_ref[:, 256:512] = (r[:, 256:512] * 0.125).astype(BF16)
        dst_ref[:, 512:1536] = r[:, 512:1536].astype(BF16)

    store_qkvx(ret_ref, seg(0, RET_W))
    store_qkvx(mls_ref, seg(OFF_MLS, OFF_MLS + MLS_W))
    gate_ref[...] = seg(OFF_GATE, OFF_GATE + GATE_W)
    half = MERGE_W // 2
    merge_ref[:, 0:half] = seg(OFF_MERGE, OFF_MERGE + half).astype(BF16)
    merge_ref[:, half:MERGE_W] = seg(OFF_MERGE + half, OFF_MERGE + MERGE_W).astype(BF16)

    a = seg(OFF_ATT, OFF_ATT + ATT_W)
    lane = lax.broadcasted_iota(jnp.int32, (tm, LANES), 1)
    ri = lax.broadcasted_iota(jnp.int32, (LANES, LANES), 0)
    ci = lax.broadcasted_iota(jnp.int32, (LANES, LANES), 1)
    ones_bd = jnp.where((ri // HD_ATT) == (ci // HD_ATT), 1.0, 0.0).astype(BF16)
    cos = cos_ref[...]
    sin = sin_ref[...]
    qg = qg_ref[...]
    for j in range(H_ATT * HD_ATT // LANES):
        qj = _norm_rope(a[:, j * LANES:(j + 1) * LANES], qg, cos, sin, ones_bd, lane,
                        HD_ATT ** -0.5 * LOG2_E)
        attq_ref[:, j * LANES:(j + 1) * LANES] = qj.astype(BF16)
    kn = _norm_rope(a[:, 512:640], kg_ref[...], cos, sin, ones_bd, lane, 1.0)
    kr = pltpu.roll(kn, HD_ATT, 1)
    low = lane < HD_ATT
    attk_ref[:, 0:128] = jnp.where(low, kn, kr).astype(BF16)
    attk_ref[:, 128:256] = jnp.where(low, kr, kn).astype(BF16)
    attvt_ref[0] = jnp.transpose(a[:, 640:768]).astype(BF16)


def _row_maps(t_len, ctx_len, tm):
    nb = t_len // tm
    cb = ctx_len // tm

    def mod_map(i):
        return ((i // nb) * 2 + ((i % nb) >= cb).astype(jnp.int32), 0, 0)

    def pos_map(i):
        return (i % nb, 0)

    return nb, cb, mod_map, pos_map


def _proj_in(xs, mod_tab, w, b, cos_t, sin_t, qg, kg, t_len, ctx_len):
    rows = xs.shape[0]
    tm = ROW_TILE
    _, _, mod_map, pos_map = _row_maps(t_len, ctx_len, tm)
    row_spec = lambda n: pl.BlockSpec((tm, n), lambda i: (i, 0))
    out_shapes = [
        jax.ShapeDtypeStruct((rows, RET_W), BF16),
        jax.ShapeDtypeStruct((rows, H_ATT * HD_ATT), BF16),
        jax.ShapeDtypeStruct((rows, 2 * LANES), BF16),
        jax.ShapeDtypeStruct((rows // tm, KV_ATT * HD_ATT, tm), BF16),
        jax.ShapeDtypeStruct((rows, MLS_W), BF16),
        jax.ShapeDtypeStruct((rows, GATE_W), F32),
        jax.ShapeDtypeStruct((rows, MERGE_W), BF16),
    ]

    def out_spec(s):
        if len(s.shape) == 3:
            return pl.BlockSpec((1,) + s.shape[1:], lambda i: (i, 0, 0))
        return row_spec(s.shape[1])

    return pl.pallas_call(
        _proj_in_kernel,
        out_shape=out_shapes,
        grid=(rows // tm,),
        in_specs=[
            row_spec(D_MODEL),
            pl.BlockSpec((None, 6, D_MODEL), mod_map),
            _const_spec((D_MODEL, N_PROJ)),
            _const_spec((1, N_PROJ)),
            pl.BlockSpec((tm, LANES), pos_map),
            pl.BlockSpec((tm, LANES), pos_map),
            _const_spec((1, LANES)),
            _const_spec((1, LANES)),
        ],
        out_specs=[out_spec(s) for s in out_shapes],
        compiler_params=_params("arbitrary"),
        name="proj_in",
    )(xs, mod_tab, w, b, cos_t, sin_t, qg, kg)


def _scan_chunk(d, j, n_ctx, n_chunks):
    if d == 0:
        return j
    return jnp.where(j < n_ctx, n_ctx - 1 - j, n_chunks + n_ctx - 1 - j)


def _run_scan_steps(body, n_ctx, n_chunks):
    for lo, hi in ((0, n_ctx), (n_ctx, n_chunks)):
        mid = lo + (hi - lo) // 2
        for start, stop, readout in ((lo, mid, False), (mid, hi, True)):
            if stop > start:
                lax.fori_loop(start, stop, functools.partial(body, readout=readout), 0)


def _head_norm(o):
    mu = jnp.mean(o, axis=-1, keepdims=True)
    oc = o - mu
    var = jnp.mean(oc * oc, axis=-1, keepdims=True)
    return oc * lax.rsqrt(var + NORM_EPS)


def _ret_kernel(p_ref, dl_ref, gn_ref, y_ref, o_scr, s_scr, intra_scr, qd_scr, kd_scr, cd_scr,
                *, n_ctx, n_chunks):
    lg = _log_sigmoid(dl_ref[...])
    cd_scr[...] = jnp.exp(float(CHUNK) * lg)
    ri = lax.broadcasted_iota(jnp.int32, (CHUNK, CHUNK), 0).astype(F32)
    ci = lax.broadcasted_iota(jnp.int32, (CHUNK, CHUNK), 1).astype(F32)
    for d in range(2):
        diff = ri - ci if d == 0 else ci - ri
        qpos = ri + 1.0 if d == 0 else float(CHUNK) - ri
        kpos = float(CHUNK - 1) - ri if d == 0 else ri
        keep = diff >= 0.0
        for h in range(H_RET):
            r = d * H_RET + h
            lgb = lg[r:r + 1, :]
            intra_scr[r] = jnp.where(keep, jnp.exp(jnp.where(keep, diff, 0.0) * lgb), 0.0)
            qd_scr[r] = jnp.exp(qpos * lgb)
            kd_scr[r] = jnp.exp(kpos * lgb)

    lane = lax.broadcasted_iota(jnp.int32, (CHUNK, LANES), 1)
    half_mask = (lane < DK_RET, lane >= DK_RET)

    s_scr[...] = jnp.zeros_like(s_scr)
    o_scr[...] = jnp.zeros_like(o_scr)

    def body(j, carry, readout):
        staged = []
        for d in range(2):
            c = _scan_chunk(d, j, n_ctx, n_chunks)
            rows = pl.ds(pl.multiple_of(c * CHUNK, CHUNK), CHUNK)
            for pair in range(H_RET // 2):
                qp = p_ref[rows, pair * LANES:(pair + 1) * LANES]
                kp = p_ref[rows, 256 + pair * LANES:256 + (pair + 1) * LANES]
                kpf = kp.astype(F32)
                for hh in range(2):
                    h = pair * 2 + hh
                    r = d * H_RET + h
                    qm = jnp.where(half_mask[hh], qp, jnp.zeros_like(qp))
                    v = p_ref[rows, 512 + h * DV_RET:512 + (h + 1) * DV_RET]
                    s = (_dot_nt(qm, kp) * intra_scr[r]).astype(BF16)
                    state = s_scr[r]
                    inter = _dot(qm, state.astype(BF16)) * qd_scr[r]
                    kdec = (kpf * kd_scr[r]).astype(BF16)
                    s_scr[r] = state * cd_scr[r:r + 1, :] + _dot_tn(kdec, v)
                    staged.append((rows, h, s, v, inter))
        for rows, h, s, v, inter in staged:
            cols = slice(h * DV_RET, (h + 1) * DV_RET)
            tot = o_scr[rows, cols] + (_dot(s, v) + inter)
            o_scr[rows, cols] = tot
            if readout:
                g = p_ref[rows, 1024 + h * DV_RET:1024 + (h + 1) * DV_RET].astype(F32)
                y = _head_norm(tot) * gn_ref[:, cols] * (g * _sigmoid(g))
                y_ref[rows, cols] = y.astype(BF16)
        return carry

    _run_scan_steps(body, n_ctx, n_chunks)


def _retention(ret_p, dl, gn, batch, t_len, ctx_len):
    n_chunks = t_len // CHUNK
    kern = functools.partial(_ret_kernel, n_ctx=ctx_len // CHUNK, n_chunks=n_chunks)
    width = H_RET * DV_RET
    return pl.pallas_call(
        kern,
        out_shape=jax.ShapeDtypeStruct((batch * t_len, width), BF16),
        grid=(batch,),
        in_specs=[
            pl.BlockSpec((t_len, RET_W), lambda b: (b, 0)),
            _const_spec((2 * H_RET, LANES)),
            _const_spec((1, width)),
        ],
        out_specs=pl.BlockSpec((t_len, width), lambda b: (b, 0)),
        scratch_shapes=[
            pltpu.VMEM((t_len, width), F32),
            pltpu.VMEM((2 * H_RET, LANES, DV_RET), F32),
            pltpu.VMEM((2 * H_RET, CHUNK, CHUNK), F32),
            pltpu.VMEM((2 * H_RET, CHUNK, CHUNK), F32),
            pltpu.VMEM((2 * H_RET, CHUNK, CHUNK), F32),
            pltpu.VMEM((2 * H_RET, LANES), F32),
        ],
        compiler_params=_params("arbitrary"),
        name="retention",
    )(ret_p, dl, gn)


def _mlstm_kernel(p_ref, gate_ref, gn_ref, y_ref, h_scr, cn_scr, m_scr, *, n_ctx, n_chunks):
    ri = lax.broadcasted_iota(jnp.int32, (CHUNK, CHUNK), 0)
    ci = lax.broadcasted_iota(jnp.int32, (CHUNK, CHUNK), 1)
    lane = lax.broadcasted_iota(jnp.int32, (CHUNK, LANES), 1)
    half_mask = (lane < DK_M, lane >= DK_M)
    ones_v = jnp.ones((CHUNK, LANES), BF16)

    cn_scr[...] = jnp.zeros_like(cn_scr)
    m_scr[...] = jnp.zeros_like(m_scr)
    h_scr[...] = jnp.zeros_like(h_scr)
    keeps = (ri >= ci, ci >= ri)
    tris = tuple(jnp.where(k, 1.0, 0.0).astype(BF16) for k in keeps)
    full = (CHUNK, LANES)

    def body(j, carry, readout):
        staged = []
        for d in range(2):
            keep = keeps[d]
            end_row = CHUNK - 1 if d == 0 else 0
            c = _scan_chunk(d, j, n_ctx, n_chunks)
            rows = pl.ds(pl.multiple_of(c * CHUNK, CHUNK), CHUNK)
            gates = gate_ref[rows, :]
            hi, mid, lo = _split3(_log_sigmoid(gates))
            bcum = _dot(tris[d], hi) + _dot(tris[d], mid) + _dot(tris[d], lo)
            cdiff = gates - pltpu.roll(bcum, LANES - 2 * H_M, 1)
            cdiff_t = jnp.transpose(cdiff)
            for pair in range(H_M // 2):
                qp = p_ref[rows, pair * LANES:(pair + 1) * LANES]
                kp = p_ref[rows, 256 + pair * LANES:256 + (pair + 1) * LANES]
                kpf = kp.astype(F32)
                for hh in range(2):
                    h = pair * 2 + hh
                    r = d * H_M + h
                    gb = 2 * H_M + r
                    b_l = jnp.broadcast_to(bcum[:, gb:gb + 1], full)
                    c_l = jnp.broadcast_to(cdiff[:, r:r + 1], full)
                    c_row = cdiff_t[r:r + 1, :]
                    b_end = b_l[end_row:end_row + 1, :]
                    m_prev = m_scr[r][0:1, :]
                    qm = jnp.where(half_mask[hh], qp, jnp.zeros_like(qp))
                    v = p_ref[rows, 512 + h * DV_M:512 + (h + 1) * DV_M]
                    vext = jnp.concatenate([v, ones_v], axis=1)

                    dmat = jnp.where(keep, b_l + c_row, -jnp.inf)
                    inter = b_l + m_prev
                    row_max = jnp.broadcast_to(jnp.max(dmat, axis=-1, keepdims=True), full)
                    m_t = jnp.maximum(inter, row_max)
                    s = (_dot_nt(qm, kp) * jnp.exp(dmat - m_t)).astype(BF16)
                    a_int = jnp.exp(inter - m_t)
                    state = cn_scr[r]
                    qcn = _dot(qm, state.astype(BF16))

                    g_l = b_end + c_l
                    m_new = jnp.maximum(b_end + m_prev, jnp.max(g_l, axis=0, keepdims=True))
                    kdec = (kpf * jnp.exp(g_l - m_new)).astype(BF16)
                    carry_scale = jnp.exp(b_end + m_prev - m_new)
                    cn_scr[r] = (jnp.concatenate([carry_scale, carry_scale], axis=1) * state
                                 + _dot_tn(kdec, vext))
                    m_scr[r] = jnp.broadcast_to(m_new, (SUBLANES, LANES))
                    staged.append((rows, h, s, vext, a_int, qcn, m_t))
        for rows, h, s, vext, a_int, qcn, m_t in staged:
            cols = slice(h * DV_M, (h + 1) * DV_M)
            sv = _dot(s, vext)
            num = sv[:, 0:DV_M] + a_int * qcn[:, 0:DV_M]
            den = sv[:, DV_M:2 * DV_M] + a_int * qcn[:, DV_M:2 * DV_M]
            hval = num / jnp.maximum(jnp.abs(den), jnp.exp(-m_t))
            tot = h_scr[rows, cols] + hval
            h_scr[rows, cols] = tot
            if readout:
                og = p_ref[rows, 1024 + h * DV_M:1024 + (h + 1) * DV_M].astype(F32)
                y = _sigmoid(og) * (_head_norm(tot) * gn_ref[:, cols])
                y_ref[rows, cols] = y.astype(BF16)
        return carry

    _run_scan_steps(body, n_ctx, n_chunks)


def _mlstm(mls_p, gates, gn, batch, t_len, ctx_len):
    n_chunks = t_len // CHUNK
    kern = functools.partial(_mlstm_kernel, n_ctx=ctx_len // CHUNK, n_chunks=n_chunks)
    width = H_M * DV_M
    return pl.pallas_call(
        kern,
        out_shape=jax.ShapeDtypeStruct((batch * t_len, width), BF16),
        grid=(batch,),
        in_specs=[
            pl.BlockSpec((t_len, MLS_W), lambda b: (b, 0)),
            pl.BlockSpec((t_len, GATE_W), lambda b: (b, 0)),
            _const_spec((1, width)),
        ],
        out_specs=pl.BlockSpec((t_len, width), lambda b: (b, 0)),
        scratch_shapes=[
            pltpu.VMEM((t_len, width), F32),
            pltpu.VMEM((2 * H_M, LANES, 2 * LANES), F32),
            pltpu.VMEM((2 * H_M, SUBLANES, LANES), F32),
        ],
        compiler_params=_params("arbitrary"),
        name="mlstm",
    )(mls_p, gates, gn)


def _attn_kernel(bound_ref, q_ref, k_ref, vt_ref, o_ref, m_scr, alpha_scr, acc_scr, p_scr,
                 *, ctx_len, t_len, bounded):
    tq = q_ref.shape[0]
    tk = vt_ref.shape[2]
    rep = H_ATT // KV_ATT
    is_ctx = pl.program_id(1) < ctx_len // tq
    lane = lax.broadcasted_iota(jnp.int32, (Q_SUB, LANES), 1)
    low = lane < HD_ATT
    bound = bound_ref[0]

    chains = [(g, qb) for g in range(KV_ATT) for qb in range(tq // Q_SUB)]
    qs = []
    for g, qb in chains:
        parts = []
        for hh in range(rep):
            h = g * rep + hh
            qp = q_ref[qb * Q_SUB:(qb + 1) * Q_SUB, (h // 2) * LANES:(h // 2 + 1) * LANES]
            parts.append(jnp.where(low if h % 2 == 0 else ~low, qp, jnp.zeros_like(qp)))
        qs.append(jnp.concatenate(parts, axis=0))
    m_scr[...] = jnp.full_like(m_scr, -jnp.inf)
    acc_scr[...] = jnp.zeros_like(acc_scr)
    p_scr[...] = jnp.zeros_like(p_scr)
    alpha_scr[...] = jnp.ones_like(alpha_scr)
    shift = 0.5 * bound

    def v_rows(g, first_slab, n_slabs):
        vt = [vt_ref[first_slab + i][g * HD_ATT:(g + 1) * HD_ATT, :] for i in range(n_slabs)]
        vt = vt[0] if n_slabs == 1 else jnp.concatenate(vt, axis=1)
        return jnp.concatenate([vt, jnp.ones((ATT_ONES_ROWS, n_slabs * tk), BF16)], axis=0)

    def probabilities(c, s_t):
        if bounded:
            return jnp.exp2((s_t - shift).astype(BF16))
        m_old = m_scr[c, 0:1, :]
        m_new = jnp.maximum(m_old, jnp.max(s_t, axis=0, keepdims=True))
        alpha_scr[c, 0:1, :] = jnp.exp2(m_old - m_new)
        m_scr[c, 0:1, :] = m_new
        return jnp.exp2(s_t - m_new).astype(BF16)

    def accumulate(c, pv):
        if bounded:
            acc_scr[c] += pv
        else:
            acc_scr[c] = alpha_scr[c, 0:1, :] * acc_scr[c] + pv

    for c, (g, _) in enumerate(chains):
        s_t = _dot_nt(k_ref[0:tk, g * LANES:(g + 1) * LANES], qs[c])
        p_scr[c, 0:tk] = probabilities(c, s_t)

    def pending_slab(step):
        return jnp.maximum(ATT_SLABS * step - 1, 0)

    def body(t, carry):
        keys = pl.ds(pl.multiple_of(tk + t * (ATT_SLABS * tk), tk), ATT_SLABS * tk)
        scores = [_dot_nt(k_ref[keys, g * LANES:(g + 1) * LANES], qs[c])
                  for c, (g, _) in enumerate(chains)]
        for c, (g, _) in enumerate(chains):
            accumulate(c, _dot(v_rows(g, pending_slab(t), ATT_SLABS), p_scr[c]))
        for c, s_t in enumerate(scores):
            p_scr[c] = probabilities(c, s_t)
        return carry

    n_steps = jnp.where(is_ctx, 0, (t_len - tk) // (ATT_SLABS * tk))
    lax.fori_loop(0, n_steps, body, 0)
    for c, (g, _) in enumerate(chains):
        accumulate(c, _dot(v_rows(g, pending_slab(n_steps), ATT_SLABS), p_scr[c]))

    for c, (g, qb) in enumerate(chains):
        acc = acc_scr[c]
        out_t = acc[0:HD_ATT] / acc[HD_ATT:HD_ATT + 1]
        for pp in range(rep // 2):
            pair_t = jnp.concatenate([out_t[:, (2 * pp) * Q_SUB:(2 * pp + 1) * Q_SUB],
                                      out_t[:, (2 * pp + 1) * Q_SUB:(2 * pp + 2) * Q_SUB]], axis=0)
            col = (g * rep // 2 + pp) * LANES
            o_ref[qb * Q_SUB:(qb + 1) * Q_SUB, col:col + LANES] = jnp.transpose(pair_t).astype(BF16)


def _attention(att_q, att_k, att_vt, bound, batch, t_len, ctx_len, bounded):
    tq = Q_TILE
    nq = t_len // tq
    rep = H_ATT // KV_ATT
    slabs, vrows, tk = att_vt.shape
    per_batch = slabs // batch
    n_chains = KV_ATT * (tq // Q_SUB)
    kern = functools.partial(_attn_kernel, ctx_len=ctx_len, t_len=t_len, bounded=bounded)
    width = H_ATT * HD_ATT
    return pl.pallas_call(
        kern,
        out_shape=jax.ShapeDtypeStruct((batch * t_len, width), BF16),
        grid=(batch, nq),
        in_specs=[
            pl.BlockSpec(memory_space=pltpu.SMEM),
            pl.BlockSpec((tq, width), lambda b, i: (b * nq + i, 0)),
            pl.BlockSpec((t_len, 2 * LANES), lambda b, i: (b, 0)),
            pl.BlockSpec((per_batch, vrows, tk), lambda b, i: (b, 0, 0)),
        ],
        out_specs=pl.BlockSpec((tq, width), lambda b, i: (b * nq + i, 0)),
        scratch_shapes=[
            pltpu.VMEM((n_chains, SUBLANES, rep * Q_SUB), F32),
            pltpu.VMEM((n_chains, SUBLANES, rep * Q_SUB), F32),
            pltpu.VMEM((n_chains, HD_ATT + ATT_ONES_ROWS, rep * Q_SUB), F32),
            pltpu.VMEM((n_chains, ATT_SLABS * tk, rep * Q_SUB), BF16),
        ],
        compiler_params=_params("arbitrary", "arbitrary"),
        name="attention_bounded" if bounded else "attention_online",
    )(bound, att_q, att_k, att_vt)


def _merge_kernel(x_ref, mod_ref, yr_ref, ya_ref, ym_ref, mg_ref, wr_ref, wa_ref, wm_ref, wo_ref,
                  lng_ref, lnb_ref, o_ref, *, alpha):
    def branch(y_ref, w_ref, k):
        gate = mg_ref[:, k * D_MODEL:(k + 1) * D_MODEL].astype(F32)
        return _sigmoid(gate) * _dot(y_ref[...], w_ref[...])

    z = branch(yr_ref, wr_ref, 0) + branch(ya_ref, wa_ref, 1) + branch(ym_ref, wm_ref, 2)
    mix = _dot(z.astype(BF16), wo_ref[...])
    y = alpha * x_ref[...] + mod_ref[2:3, :] * mix
    o_ref[...] = _layer_norm(y) * lng_ref[...] + lnb_ref[...]


def _merge(xs, mod_tab, y_ret, y_att, y_m, mg, w_r, w_a, w_m, w_o, ln_g, ln_b, t_len, ctx_len, alpha):
    rows = xs.shape[0]
    tm = ROW_TILE
    _, _, mod_map, _ = _row_maps(t_len, ctx_len, tm)
    row_spec = lambda n: pl.BlockSpec((tm, n), lambda i: (i, 0))
    bw = y_ret.shape[1]
    return pl.pallas_call(
        functools.partial(_merge_kernel, alpha=alpha),
        out_shape=jax.ShapeDtypeStruct((rows, D_MODEL), F32),
        grid=(rows // tm,),
        in_specs=[
            row_spec(D_MODEL),
            pl.BlockSpec((None, 6, D_MODEL), mod_map),
            row_spec(bw), row_spec(bw), row_spec(bw), row_spec(MERGE_W),
            _const_spec((bw, D_MODEL)), _const_spec((bw, D_MODEL)), _const_spec((bw, D_MODEL)),
            _const_spec((D_MODEL, D_MODEL)),
            _const_spec((1, D_MODEL)), _const_spec((1, D_MODEL)),
        ],
        out_specs=row_spec(D_MODEL),
        compiler_params=_params("arbitrary"),
        name="merge",
    )(xs, mod_tab, y_ret, y_att, y_m, mg, w_r, w_a, w_m, w_o, ln_g, ln_b)


def _ffn_kernel(x_ref, xp_ref, xn_ref, mod_ref, wu_ref, cw_ref, cb_ref, wd_ref, lng_ref, lnb_ref,
                o_ref, act_scr, *, alpha, nb, cb):
    tm = x_ref.shape[0]
    j = pl.program_id(0) % nb
    prev_ok = jnp.logical_and(j != 0, j != cb)
    next_ok = jnp.logical_and(j != cb - 1, j != nb - 1)
    shift = mod_ref[3:4, :]
    scale = 1.0 + mod_ref[4:5, :]

    def mod(xv):
        return _layer_norm(xv) * scale + shift

    x = x_ref[...]
    hp = jnp.where(prev_ok, mod(xp_ref[...]), 0.0)
    hn = jnp.where(next_ok, mod(xn_ref[...]), 0.0)
    hext = jnp.concatenate([hp, mod(x), hn], axis=0).astype(BF16)

    def conv(u, lo, hi):
        w = cw_ref[:, lo:hi]
        return (u[SUBLANES - 1:SUBLANES - 1 + tm] * w[0:1] + u[SUBLANES:SUBLANES + tm] * w[1:2]
                + u[SUBLANES + 1:SUBLANES + 1 + tm] * w[2:3] + cb_ref[:, lo:hi])

    def up(c):
        lo, hi = c * FF_CHUNK, (c + 1) * FF_CHUNK
        return _dot(hext, wu_ref[:, lo:hi]), _dot(hext, wu_ref[:, D_FF + lo:D_FF + hi])

    n_chunks = D_FF // FF_CHUNK
    split = (n_chunks + 1) // 2
    acc = None
    ua, ug = up(0)
    for c in range(n_chunks):
        nxt = up(c + 1) if c + 1 < n_chunks else None
        lo, hi = c * FF_CHUNK, (c + 1) * FF_CHUNK
        a = conv(ua, lo, hi)
        g = conv(ug, D_FF + lo, D_FF + hi)
        act_scr[:, lo:hi] = (g * _sigmoid(g) * a).astype(BF16)
        if c + 1 == split:
            acc = _dot(act_scr[:, 0:hi], wd_ref[0:hi, :])
        if nxt is not None:
            ua, ug = nxt
    acc = acc + _dot(act_scr[:, split * FF_CHUNK:D_FF], wd_ref[split * FF_CHUNK:D_FF, :])
    y = alpha * x + mod_ref[5:6, :] * acc
    o_ref[...] = _layer_norm(y) * lng_ref[...] + lnb_ref[...]


def _ffn(xs, mod_tab, w_up, conv_w, conv_b, w_down, ln_g, ln_b, t_len, ctx_len, alpha, latent_only):
    rows = xs.shape[0]
    tm = ROW_TILE
    nb, cb, mod_map, _ = _row_maps(t_len, ctx_len, tm)
    per = tm // SUBLANES
    last = rows // SUBLANES - 1
    if latent_only:
        out_rows = rows // nb * (nb - cb)
        out_map = lambda i: ((i // nb) * (nb - cb) + jnp.maximum(i % nb - cb, 0), 0)
    else:
        out_rows = rows
        out_map = lambda i: (i, 0)
    return pl.pallas_call(
        functools.partial(_ffn_kernel, alpha=alpha, nb=nb, cb=cb),
        out_shape=jax.ShapeDtypeStruct((out_rows, D_MODEL), F32),
        grid=(rows // tm,),
        in_specs=[
            pl.BlockSpec((tm, D_MODEL), lambda i: (i, 0)),
            pl.BlockSpec((SUBLANES, D_MODEL), lambda i: (jnp.maximum(i * per - 1, 0), 0)),
            pl.BlockSpec((SUBLANES, D_MODEL), lambda i: (jnp.minimum((i + 1) * per, last), 0)),
            pl.BlockSpec((None, 6, D_MODEL), mod_map),
            _const_spec((D_MODEL, 2 * D_FF)),
            _const_spec((3, 2 * D_FF)),
            _const_spec((1, 2 * D_FF)),
            _const_spec((D_FF, D_MODEL)),
            _const_spec((1, D_MODEL)), _const_spec((1, D_MODEL)),
        ],
        out_specs=pl.BlockSpec((tm, D_MODEL), out_map),
        scratch_shapes=[pltpu.VMEM((tm, D_FF), BF16)],
        compiler_params=_params("arbitrary"),
        name="conv_ffn",
    )(xs, xs, xs, mod_tab, w_up, conv_w, conv_b, w_down, ln_g, ln_b)


def _rope_tables(seq, ctx_len):
    rows = seq // GRID_W
    row = jnp.repeat(jnp.arange(rows), GRID_W).astype(F32)
    col = jnp.tile(jnp.arange(GRID_W), rows).astype(F32)
    n_freq = HD_ATT // 4
    freqs = ROPE_THETA ** (-jnp.arange(n_freq, dtype=F32) / n_freq)
    ar = row[:, None] * freqs[None, :]
    ac = col[:, None] * freqs[None, :]
    cos = jnp.concatenate([jnp.cos(ar), jnp.cos(ar), jnp.cos(ac), jnp.cos(ac)], axis=-1)
    sin = jnp.concatenate([-jnp.sin(ar), jnp.sin(ar), -jnp.sin(ac), jnp.sin(ac)], axis=-1)
    cos = jnp.concatenate([jnp.ones((ctx_len, HD_ATT), F32), cos], axis=0)
    sin = jnp.concatenate([jnp.zeros((ctx_len, HD_ATT), F32), sin], axis=0)
    return jnp.tile(cos, (1, 2)), jnp.tile(sin, (1, 2))


def kernel(x, c, ctx, c_ctx, w_mod, b_mod, w_in, b_in, ret_decay_logit, ret_gn_g, attn_qn_g, attn_kn_g,
           mlstm_gn_g, w_br_ret, w_br_att, w_br_mlstm, w_out, ln1_g, ln1_b, w_up, conv_w, conv_b, w_down,
           ln2_g, ln2_b):
    batch, seq, _ = x.shape
    ctx_len = ctx.shape[1]
    depth = w_mod.shape[0]
    t_len = ctx_len + seq
    alpha = (2.0 * depth) ** 0.25
    assert ctx_len == ROW_TILE and seq % (ATT_SLABS * ROW_TILE) == 0
    assert seq % GRID_W == 0 and batch + 1 <= SUBLANES

    xs = jnp.concatenate([ctx, x], axis=1).reshape(batch * t_len, D_MODEL)

    cc = jnp.zeros((SUBLANES, D_MODEL), F32).at[:batch].set(c).at[batch].set(c_ctx)
    mod = _modulation(cc, w_mod, b_mod)
    mod_lat = mod[:, :batch].reshape(depth, batch, 1, 6, D_MODEL)
    mod_ctx = jnp.broadcast_to(mod[:, batch].reshape(depth, 1, 1, 6, D_MODEL),
                               (depth, batch, 1, 6, D_MODEL))
    mod_tab = jnp.concatenate([mod_ctx, mod_lat], axis=2).reshape(depth, batch * 2, 6, D_MODEL)

    cos_t, sin_t = _rope_tables(seq, ctx_len)
    n_gate = 4 * H_M
    split = OFF_GATE + n_gate

    def pad_proj(a):
        pad = jnp.zeros(a.shape[:-1] + (GATE_W - n_gate,), a.dtype)
        return jnp.concatenate([a[..., :split], pad, a[..., split:]], axis=-1)

    w_in_p = pad_proj(w_in).astype(BF16)
    b_in_p = pad_proj(b_in).reshape(depth, 1, N_PROJ)
    dl = jnp.broadcast_to(ret_decay_logit.reshape(depth, 2 * H_RET, 1), (depth, 2 * H_RET, LANES))
    qg = jnp.tile(attn_qn_g, (1, 2)).reshape(depth, 1, LANES)
    kg = jnp.tile(attn_kn_g, (1, 2)).reshape(depth, 1, LANES)
    w_r, w_a, w_m, w_o = (w.astype(BF16) for w in (w_br_ret, w_br_att, w_br_mlstm, w_out))
    w_up_b = w_up.astype(BF16)
    w_down_b = w_down.astype(BF16)
    row2 = lambda a, l: a[l].reshape(1, -1)

    for l in range(depth):
        ret_p, att_q, att_k, att_vt, mls_p, gates, mg = _proj_in(
            xs, mod_tab[l], w_in_p[l], b_in_p[l], cos_t, sin_t, qg[l], kg[l], t_len, ctx_len)
        y_ret = _retention(ret_p, dl[l], row2(ret_gn_g, l), batch, t_len, ctx_len)
        bound = (1.01 * LOG2_E * HD_ATT ** 0.5 * jnp.max(jnp.abs(attn_qn_g[l]))
                 * jnp.max(jnp.abs(attn_kn_g[l]))).reshape(1)
        attend = lambda fixed: functools.partial(
            _attention, batch=batch, t_len=t_len, ctx_len=ctx_len, bounded=fixed)
        y_att = lax.cond(bound[0] <= ATT_SAFE_BOUND, attend(True), attend(False),
                         att_q, att_k, att_vt, bound)
        y_m = _mlstm(mls_p, gates, row2(mlstm_gn_g, l), batch, t_len, ctx_len)
        xs = _merge(xs, mod_tab[l], y_ret, y_att, y_m, mg, w_r[l], w_a[l], w_m[l], w_o[l],
                    row2(ln1_g, l), row2(ln1_b, l), t_len, ctx_len, alpha)
        xs = _ffn(xs, mod_tab[l], w_up_b[l], conv_w[l], row2(conv_b, l), w_down_b[l],
                  row2(ln2_g, l), row2(ln2_b, l), t_len, ctx_len, alpha,
                  latent_only=l == depth - 1)

    return xs.reshape(batch, seq, D_MODEL)
```

```python
import functools

import jax
import jax.numpy as jnp
from jax import lax
from jax.experimental import pallas as pl
from jax.experimental.pallas import tpu as pltpu

F32 = jnp.float32
BF16 = jnp.bfloat16

D_MODEL = 1024
H_RET, DK_RET, DV_RET = 4, 64, 128
H_ATT, KV_ATT, HD_ATT = 8, 2, 64
H_M, DK_M, DV_M = 4, 64, 128
CHUNK = 128
D_FF = 2816
GRID_W = 64
ROPE_THETA = 10000.0
NORM_EPS = 1e-6
LOG2_E = 1.4426950408889634

LANES = 128
SUBLANES = 8
ROW_TILE = 256
Q_TILE = 256
Q_SUB = 128
ATT_SAFE_BOUND = 40.0
ATT_SLABS = 2
ATT_ONES_ROWS = 16
FF_CHUNK = 256
MOD_COLS = 1536
VMEM_LIMIT = 56 * 1024 * 1024

RET_W = 2 * H_RET * DK_RET + 2 * H_RET * DV_RET
ATT_W = (H_ATT + 2 * KV_ATT) * HD_ATT
MLS_W = 2 * H_M * DK_M + 2 * H_M * DV_M
GATE_W = LANES
MERGE_W = 3 * D_MODEL
OFF_ATT = RET_W
OFF_MLS = OFF_ATT + ATT_W
OFF_GATE = OFF_MLS + MLS_W
OFF_MERGE = OFF_GATE + GATE_W
N_PROJ = OFF_MERGE + MERGE_W


def _dot(a, b):
    return jnp.dot(a, b, preferred_element_type=F32)


def _dot_nt(a, b):
    return lax.dot_general(a, b, (((1,), (1,)), ((), ())), preferred_element_type=F32)


def _dot_tn(a, b):
    return lax.dot_general(a, b, (((0,), (0,)), ((), ())), preferred_element_type=F32)


def _layer_norm(x):
    mu = jnp.mean(x, axis=-1, keepdims=True)
    xc = x - mu
    var = jnp.mean(xc * xc, axis=-1, keepdims=True)
    return xc * lax.rsqrt(var + NORM_EPS)


def _log_sigmoid(x):
    return jnp.minimum(x, 0.0) - jnp.log1p(jnp.exp(-jnp.abs(x)))


def _sigmoid(x):
    return 1.0 / (1.0 + jnp.exp(-x))


def _split3(x):
    hi = x.astype(BF16)
    r1 = x - hi.astype(F32)
    mid = r1.astype(BF16)
    lo = (r1 - mid.astype(F32)).astype(BF16)
    return hi, mid, lo


def _const_spec(shape):
    zeros = (0,) * len(shape)
    return pl.BlockSpec(shape, lambda *_: zeros, pipeline_mode=pl.Buffered(1))


def _params(*sem):
    return pltpu.CompilerParams(dimension_semantics=sem, vmem_limit_bytes=VMEM_LIMIT)


def _mod_kernel(c_ref, w_ref, b_ref, o_ref):
    c = c_ref[...]
    s = (c * _sigmoid(c)).astype(BF16)
    o_ref[0] = _dot(s, w_ref[0].astype(BF16)) + b_ref[0]


def _modulation(cc, w_mod, b_mod):
    depth, _, n = w_mod.shape
    rows = cc.shape[0]
    return pl.pallas_call(
        _mod_kernel,
        out_shape=jax.ShapeDtypeStruct((depth, rows, n), F32),
        grid=(depth, n // MOD_COLS),
        in_specs=[
            pl.BlockSpec((rows, D_MODEL), lambda l, j: (0, 0)),
            pl.BlockSpec((1, D_MODEL, MOD_COLS), lambda l, j: (l, 0, j)),
            pl.BlockSpec((1, 1, MOD_COLS), lambda l, j: (l, 0, j)),
        ],
        out_specs=pl.BlockSpec((1, rows, MOD_COLS), lambda l, j: (l, 0, j)),
        compiler_params=_params("arbitrary", "arbitrary"),
        name="modulation",
    )(cc, w_mod, b_mod.reshape(depth, 1, n))


def _head_sumsq(x, ones_bd):
    sq = x * x
    hi = sq.astype(BF16)
    lo = (sq - hi.astype(F32)).astype(BF16)
    return _dot(hi, ones_bd) + _dot(lo, ones_bd)


def _norm_rope(x, gain, cos, sin, ones_bd, lane, scale):
    y = x * lax.rsqrt(_head_sumsq(x, ones_bd) * (1.0 / HD_ATT) + NORM_EPS) * gain
    above = pltpu.roll(y, LANES - 16, 1)
    below = pltpu.roll(y, 16, 1)
    partner = jnp.where((lane & 16) == 0, above, below)
    return (y * cos + partner * sin) * scale


def _proj_in_kernel(*refs, n_x, nb, cb):
    x_refs = refs[:n_x]
    (mod_ref, w_ref, b_ref, cos_ref, sin_ref, qg_ref, kg_ref,
     ret_ref, attq_ref, attk_ref, attvt_ref, mls_ref, gate_ref, merge_ref) = refs[n_x:]
    tm = x_refs[0].shape[0]
    x = _read_rows(x_refs, nb, cb)
    h = (_layer_norm(x) * (1.0 + mod_ref[1:2, :]) + mod_ref[0:1, :]).astype(BF16)

    def seg(lo, hi):
        return _dot(h, w_ref[:, lo:hi]) + b_ref[:, lo:hi]

    def store_qkvx(dst_ref, r):
        dst_ref[:, 0:256] = r[:, 0:256].astype(BF16)
        dst_ref[:, 256:512] = (r[:, 256:512] * 0.125).astype(BF16)
        dst_ref[:, 512:1536] = r[:, 512:1536].astype(BF16)

    store_qkvx(ret_ref, seg(0, RET_W))
    store_qkvx(mls_ref, seg(OFF_MLS, OFF_MLS + MLS_W))
    gate_ref[...] = seg(OFF_GATE, OFF_GATE + GATE_W)
    half = MERGE_W // 2
    merge_ref[:, 0:half] = seg(OFF_MERGE, OFF_MERGE + half).astype(BF16)
    merge_ref[:, half:MERGE_W] = seg(OFF_MERGE + half, OFF_MERGE + MERGE_W).astype(BF16)

    a = seg(OFF_ATT, OFF_ATT + ATT_W)
    lane = lax.broadcasted_iota(jnp.int32, (tm, LANES), 1)
    ri = lax.broadcasted_iota(jnp.int32, (LANES, LANES), 0)
    ci = lax.broadcasted_iota(jnp.int32, (LANES, LANES), 1)
    ones_bd = jnp.where((ri // HD_ATT) == (ci // HD_ATT), 1.0, 0.0).astype(BF16)
    cos = cos_ref[...]
    sin = sin_ref[...]
    qg = qg_ref[...]
    for j in range(H_ATT * HD_ATT // LANES):
        qj = _norm_rope(a[:, j * LANES:(j + 1) * LANES], qg, cos, sin, ones_bd, lane,
                        HD_ATT ** -0.5 * LOG2_E)
        attq_ref[:, j * LANES:(j + 1) * LANES] = qj.astype(BF16)
    kn = _norm_rope(a[:, 512:640], kg_ref[...], cos, sin, ones_bd, lane, 1.0)
    kr = pltpu.roll(kn, HD_ATT, 1)
    low = lane < HD_ATT
    attk_ref[:, 0:128] = jnp.where(low, kn, kr).astype(BF16)
    attk_ref[:, 128:256] = jnp.where(low, kr, kn).astype(BF16)
    attvt_ref[0] = jnp.transpose(a[:, 640:768]).astype(BF16)


def _row_maps(t_len, ctx_len, tm):
    nb = t_len // tm
    cb = ctx_len // tm

    def mod_map(i):
        return ((i // nb) * 2 + ((i % nb) >= cb).astype(jnp.int32), 0, 0)

    def pos_map(i):
        return (i % nb, 0)

    return nb, cb, mod_map, pos_map


def _row_inputs(xs, t_len, ctx_len, tm):
    nb, cb = t_len // tm, ctx_len // tm
    if not isinstance(xs, tuple):
        return [xs], [pl.BlockSpec((tm, D_MODEL), lambda i: (i, 0))], xs.shape[0]
    ctx_map = lambda i: ((i // nb) * cb + jnp.minimum(i % nb, cb - 1), 0)
    lat_map = lambda i: ((i // nb) * (nb - cb) + jnp.maximum(i % nb - cb, 0), 0)
    specs = [pl.BlockSpec((tm, D_MODEL), ctx_map), pl.BlockSpec((tm, D_MODEL), lat_map)]
    return list(xs), specs, xs[0].shape[0] + xs[1].shape[0]


def _read_rows(x_refs, nb, cb):
    if len(x_refs) == 1:
        return x_refs[0][...]
    return jnp.where(pl.program_id(0) % nb < cb, x_refs[0][...], x_refs[1][...])


def _proj_in(xs, mod_tab, w, b, cos_t, sin_t, qg, kg, t_len, ctx_len):
    tm = ROW_TILE
    nb, cb, mod_map, pos_map = _row_maps(t_len, ctx_len, tm)
    x_args, x_specs, rows = _row_inputs(xs, t_len, ctx_len, tm)
    row_spec = lambda n: pl.BlockSpec((tm, n), lambda i: (i, 0))
    out_shapes = [
        jax.ShapeDtypeStruct((rows, RET_W), BF16),
        jax.ShapeDtypeStruct((rows, H_ATT * HD_ATT), BF16),
        jax.ShapeDtypeStruct((rows, 2 * LANES), BF16),
        jax.ShapeDtypeStruct((rows // tm, KV_ATT * HD_ATT, tm), BF16),
        jax.ShapeDtypeStruct((rows, MLS_W), BF16),
        jax.ShapeDtypeStruct((rows, GATE_W), F32),
        jax.ShapeDtypeStruct((rows, MERGE_W), BF16),
    ]

    def out_spec(s):
        if len(s.shape) == 3:
            return pl.BlockSpec((1,) + s.shape[1:], lambda i: (i, 0, 0))
        return row_spec(s.shape[1])

    return pl.pallas_call(
        functools.partial(_proj_in_kernel, n_x=len(x_args), nb=nb, cb=cb),
        out_shape=out_shapes,
        grid=(rows // tm,),
        in_specs=x_specs + [
            pl.BlockSpec((None, 6, D_MODEL), mod_map),
            _const_spec((D_MODEL, N_PROJ)),
            _const_spec((1, N_PROJ)),
            pl.BlockSpec((tm, LANES), pos_map),
            pl.BlockSpec((tm, LANES), pos_map),
            _const_spec((1, LANES)),
            _const_spec((1, LANES)),
        ],
        out_specs=[out_spec(s) for s in out_shapes],
        compiler_params=_params("arbitrary"),
        name="proj_in",
    )(*x_args, mod_tab, w, b, cos_t, sin_t, qg, kg)


def _scan_chunk(d, j, n_ctx, n_chunks):
    if d == 0:
        return j
    return jnp.where(j < n_ctx, n_ctx - 1 - j, n_chunks + n_ctx - 1 - j)


def _run_scan_steps(body, n_ctx, n_chunks):
    for lo, hi in ((0, n_ctx), (n_ctx, n_chunks)):
        mid = lo + (hi - lo) // 2
        for start, stop, readout in ((lo, mid, False), (mid, hi, True)):
            if stop > start:
                lax.fori_loop(start, stop, functools.partial(body, readout=readout), 0)


def _head_norm(o):
    mu = jnp.mean(o, axis=-1, keepdims=True)
    oc = o - mu
    var = jnp.mean(oc * oc, axis=-1, keepdims=True)
    return oc * lax.rsqrt(var + NORM_EPS)


def _ret_kernel(p_ref, dl_ref, gn_ref, y_ref, o_scr, s_scr, intra_scr, qd_scr, kd_scr, cd_scr,
                *, n_ctx, n_chunks):
    lg = _log_sigmoid(dl_ref[...])
    cd_scr[...] = jnp.exp(float(CHUNK) * lg)
    ri = lax.broadcasted_iota(jnp.int32, (CHUNK, CHUNK), 0).astype(F32)
    ci = lax.broadcasted_iota(jnp.int32, (CHUNK, CHUNK), 1).astype(F32)
    for d in range(2):
        diff = ri - ci if d == 0 else ci - ri
        qpos = ri + 1.0 if d == 0 else float(CHUNK) - ri
        kpos = float(CHUNK - 1) - ri if d == 0 else ri
        keep = diff >= 0.0
        for h in range(H_RET):
            r = d * H_RET + h
            lgb = lg[r:r + 1, :]
            intra_scr[r] = jnp.where(keep, jnp.exp(jnp.where(keep, diff, 0.0) * lgb), 0.0)
            qd_scr[r] = jnp.exp(qpos * lgb)
            kd_scr[r] = jnp.exp(kpos * lgb)

    lane = lax.broadcasted_iota(jnp.int32, (CHUNK, LANES), 1)
    half_mask = (lane < DK_RET, lane >= DK_RET)

    s_scr[...] = jnp.zeros_like(s_scr)
    o_scr[...] = jnp.zeros_like(o_scr)

    def body(j, carry, readout):
        staged = []
        for d in range(2):
            c = _scan_chunk(d, j, n_ctx, n_chunks)
            rows = pl.ds(pl.multiple_of(c * CHUNK, CHUNK), CHUNK)
            for pair in range(H_RET // 2):
                qp = p_ref[rows, pair * LANES:(pair + 1) * LANES]
                kp = p_ref[rows, 256 + pair * LANES:256 + (pair + 1) * LANES]
                kpf = kp.astype(F32)
                for hh in range(2):
                    h = pair * 2 + hh
                    r = d * H_RET + h
                    qm = jnp.where(half_mask[hh], qp, jnp.zeros_like(qp))
                    v = p_ref[rows, 512 + h * DV_RET:512 + (h + 1) * DV_RET]
                    s = (_dot_nt(qm, kp) * intra_scr[r]).astype(BF16)
                    state = s_scr[r]
                    inter = _dot(qm, state.astype(BF16)) * qd_scr[r]
                    kdec = (kpf * kd_scr[r]).astype(BF16)
                    s_scr[r] = state * cd_scr[r:r + 1, :] + _dot_tn(kdec, v)
                    staged.append((rows, h, s, v, inter))
        for rows, h, s, v, inter in staged:
            cols = slice(h * DV_RET, (h + 1) * DV_RET)
            tot = o_scr[rows, cols] + (_dot(s, v) + inter)
            o_scr[rows, cols] = tot
            if readout:
                g = p_ref[rows, 1024 + h * DV_RET:1024 + (h + 1) * DV_RET].astype(F32)
                y = _head_norm(tot) * gn_ref[:, cols] * (g * _sigmoid(g))
                y_ref[rows, cols] = y.astype(BF16)
        return carry

    _run_scan_steps(body, n_ctx, n_chunks)


def _retention(ret_p, dl, gn, batch, t_len, ctx_len):
    n_chunks = t_len // CHUNK
    kern = functools.partial(_ret_kernel, n_ctx=ctx_len // CHUNK, n_chunks=n_chunks)
    width = H_RET * DV_RET
    return pl.pallas_call(
        kern,
        out_shape=jax.ShapeDtypeStruct((batch * t_len, width), BF16),
        grid=(batch,),
        in_specs=[
            pl.BlockSpec((t_len, RET_W), lambda b: (b, 0)),
            _const_spec((2 * H_RET, LANES)),
            _const_spec((1, width)),
        ],
        out_specs=pl.BlockSpec((t_len, width), lambda b: (b, 0)),
        scratch_shapes=[
            pltpu.VMEM((t_len, width), F32),
            pltpu.VMEM((2 * H_RET, LANES, DV_RET), F32),
            pltpu.VMEM((2 * H_RET, CHUNK, CHUNK), F32),
            pltpu.VMEM((2 * H_RET, CHUNK, CHUNK), F32),
            pltpu.VMEM((2 * H_RET, CHUNK, CHUNK), F32),
            pltpu.VMEM((2 * H_RET, LANES), F32),
        ],
        compiler_params=_params("arbitrary"),
        name="retention",
    )(ret_p, dl, gn)


def _mlstm_kernel(p_ref, gate_ref, gn_ref, y_ref, h_scr, cn_scr, m_scr, *, n_ctx, n_chunks):
    ri = lax.broadcasted_iota(jnp.int32, (CHUNK, CHUNK), 0)
    ci = lax.broadcasted_iota(jnp.int32, (CHUNK, CHUNK), 1)
    lane = lax.broadcasted_iota(jnp.int32, (CHUNK, LANES), 1)
    half_mask = (lane < DK_M, lane >= DK_M)
    ones_v = jnp.ones((CHUNK, LANES), BF16)

    cn_scr[...] = jnp.zeros_like(cn_scr)
    m_scr[...] = jnp.zeros_like(m_scr)
    h_scr[...] = jnp.zeros_like(h_scr)
    keeps = (ri >= ci, ci >= ri)
    tris = tuple(jnp.where(k, 1.0, 0.0).astype(BF16) for k in keeps)
    full = (CHUNK, LANES)

    def body(j, carry, readout):
        staged = []
        for d in range(2):
            keep = keeps[d]
            end_row = CHUNK - 1 if d == 0 else 0
            c = _scan_chunk(d, j, n_ctx, n_chunks)
            rows = pl.ds(pl.multiple_of(c * CHUNK, CHUNK), CHUNK)
            gates = gate_ref[rows, :]
            hi, mid, lo = _split3(_log_sigmoid(gates))
            bcum = _dot(tris[d], hi) + _dot(tris[d], mid) + _dot(tris[d], lo)
            cdiff = gates - pltpu.roll(bcum, LANES - 2 * H_M, 1)
            cdiff_t = jnp.transpose(cdiff)
            for pair in range(H_M // 2):
                qp = p_ref[rows, pair * LANES:(pair + 1) * LANES]
                kp = p_ref[rows, 256 + pair * LANES:256 + (pair + 1) * LANES]
                kpf = kp.astype(F32)
                for hh in range(2):
                    h = pair * 2 + hh
                    r = d * H_M + h
                    gb = 2 * H_M + r
                    b_l = jnp.broadcast_to(bcum[:, gb:gb + 1], full)
                    c_l = jnp.broadcast_to(cdiff[:, r:r + 1], full)
                    c_row = cdiff_t[r:r + 1, :]
                    b_end = b_l[end_row:end_row + 1, :]
                    m_prev = m_scr[r][0:1, :]
                    qm = jnp.where(half_mask[hh], qp, jnp.zeros_like(qp))
                    v = p_ref[rows, 512 + h * DV_M:512 + (h + 1) * DV_M]
                    vext = jnp.concatenate([v, ones_v], axis=1)

                    dmat = jnp.where(keep, b_l + c_row, -jnp.inf)
                    inter = b_l + m_prev
                    row_max = jnp.broadcast_to(jnp.max(dmat, axis=-1, keepdims=True), full)
                    m_t = jnp.maximum(inter, row_max)
                    s = (_dot_nt(qm, kp) * jnp.exp(dmat - m_t)).astype(BF16)
                    a_int = jnp.exp(inter - m_t)
                    state = cn_scr[r]
                    qcn = _dot(qm, state.astype(BF16))

                    g_l = b_end + c_l
                    m_new = jnp.maximum(b_end + m_prev, jnp.max(g_l, axis=0, keepdims=True))
                    kdec = (kpf * jnp.exp(g_l - m_new)).astype(BF16)
                    carry_scale = jnp.exp(b_end + m_prev - m_new)
                    cn_scr[r] = (jnp.concatenate([carry_scale, carry_scale], axis=1) * state
                                 + _dot_tn(kdec, vext))
                    m_scr[r] = jnp.broadcast_to(m_new, (SUBLANES, LANES))
                    staged.append((rows, h, s, vext, a_int, qcn, m_t))
        for rows, h, s, vext, a_int, qcn, m_t in staged:
            cols = slice(h * DV_M, (h + 1) * DV_M)
            sv = _dot(s, vext)
            num = sv[:, 0:DV_M] + a_int * qcn[:, 0:DV_M]
            den = sv[:, DV_M:2 * DV_M] + a_int * qcn[:, DV_M:2 * DV_M]
            hval = num / jnp.maximum(jnp.abs(den), jnp.exp(-m_t))
            tot = h_scr[rows, cols] + hval
            h_scr[rows, cols] = tot
            if readout:
                og = p_ref[rows, 1024 + h * DV_M:1024 + (h + 1) * DV_M].astype(F32)
                y = _sigmoid(og) * (_head_norm(tot) * gn_ref[:, cols])
                y_ref[rows, cols] = y.astype(BF16)
        return carry

    _run_scan_steps(body, n_ctx, n_chunks)


def _mlstm(mls_p, gates, gn, batch, t_len, ctx_len):
    n_chunks = t_len // CHUNK
    kern = functools.partial(_mlstm_kernel, n_ctx=ctx_len // CHUNK, n_chunks=n_chunks)
    width = H_M * DV_M
    return pl.pallas_call(
        kern,
        out_shape=jax.ShapeDtypeStruct((batch * t_len, width), BF16),
        grid=(batch,),
        in_specs=[
            pl.BlockSpec((t_len, MLS_W), lambda b: (b, 0)),
            pl.BlockSpec((t_len, GATE_W), lambda b: (b, 0)),
            _const_spec((1, width)),
        ],
        out_specs=pl.BlockSpec((t_len, width), lambda b: (b, 0)),
        scratch_shapes=[
            pltpu.VMEM((t_len, width), F32),
            pltpu.VMEM((2 * H_M, LANES, 2 * LANES), F32),
            pltpu.VMEM((2 * H_M, SUBLANES, LANES), F32),
        ],
        compiler_params=_params("arbitrary"),
        name="mlstm",
    )(mls_p, gates, gn)


def _attn_kernel(bound_ref, q_ref, k_ref, vt_ref, o_ref, m_scr, alpha_scr, acc_scr, p_scr,
                 *, ctx_len, t_len, bounded):
    tq = q_ref.shape[0]
    tk = vt_ref.shape[2]
    rep = H_ATT // KV_ATT
    is_ctx = pl.program_id(1) < ctx_len // tq
    lane = lax.broadcasted_iota(jnp.int32, (Q_SUB, LANES), 1)
    low = lane < HD_ATT
    bound = bound_ref[0]

    chains = [(g, qb) for g in range(KV_ATT) for qb in range(tq // Q_SUB)]
    qs = []
    for g, qb in chains:
        parts = []
        for hh in range(rep):
            h = g * rep + hh
            qp = q_ref[qb * Q_SUB:(qb + 1) * Q_SUB, (h // 2) * LANES:(h // 2 + 1) * LANES]
            parts.append(jnp.where(low if h % 2 == 0 else ~low, qp, jnp.zeros_like(qp)))
        qs.append(jnp.concatenate(parts, axis=0))
    if not bounded:
        m_scr[...] = jnp.full_like(m_scr, -jnp.inf)
        alpha_scr[...] = jnp.ones_like(alpha_scr)
    acc_scr[...] = jnp.zeros_like(acc_scr)
    p_scr[:, tk:, :] = jnp.zeros((len(chains), (ATT_SLABS - 1) * tk, rep * Q_SUB), BF16)
    shift = 0.5 * bound

    def v_rows(g, first_slab, n_slabs):
        vt = [vt_ref[first_slab + i][g * HD_ATT:(g + 1) * HD_ATT, :] for i in range(n_slabs)]
        vt = vt[0] if n_slabs == 1 else jnp.concatenate(vt, axis=1)
        return jnp.concatenate([vt, jnp.ones((ATT_ONES_ROWS, n_slabs * tk), BF16)], axis=0)

    def probabilities(c, s_t):
        if bounded:
            return jnp.exp2((s_t - shift).astype(BF16))
        m_old = m_scr[c, 0:1, :]
        m_new = jnp.maximum(m_old, jnp.max(s_t, axis=0, keepdims=True))
        alpha_scr[c, 0:1, :] = jnp.exp2(m_old - m_new)
        m_scr[c, 0:1, :] = m_new
        return jnp.exp2(s_t - m_new).astype(BF16)

    def accumulate(c, pv):
        if bounded:
            acc_scr[c] += pv
        else:
            acc_scr[c] = alpha_scr[c, 0:1, :] * acc_scr[c] + pv

    for c, (g, _) in enumerate(chains):
        s_t = _dot_nt(k_ref[0:tk, g * LANES:(g + 1) * LANES], qs[c])
        p_scr[c, 0:tk] = probabilities(c, s_t)

    def pending_slab(step):
        return jnp.maximum(ATT_SLABS * step - 1, 0)

    def body(t, carry):
        keys = pl.ds(pl.multiple_of(tk + t * (ATT_SLABS * tk), tk), ATT_SLABS * tk)
        scores = [_dot_nt(k_ref[keys, g * LANES:(g + 1) * LANES], qs[c])
                  for c, (g, _) in enumerate(chains)]
        for c, (g, _) in enumerate(chains):
            accumulate(c, _dot(v_rows(g, pending_slab(t), ATT_SLABS), p_scr[c]))
        for c, s_t in enumerate(scores):
            p_scr[c] = probabilities(c, s_t)
        return carry

    n_steps = jnp.where(is_ctx, 0, (t_len - tk) // (ATT_SLABS * tk))
    lax.fori_loop(0, n_steps, body, 0)
    for c, (g, _) in enumerate(chains):
        accumulate(c, _dot(v_rows(g, pending_slab(n_steps), ATT_SLABS), p_scr[c]))

    for c, (g, qb) in enumerate(chains):
        acc = acc_scr[c]
        out_t = acc[0:HD_ATT] / acc[HD_ATT:HD_ATT + 1]
        for pp in range(rep // 2):
            pair_t = jnp.concatenate([out_t[:, (2 * pp) * Q_SUB:(2 * pp + 1) * Q_SUB],
                                      out_t[:, (2 * pp + 1) * Q_SUB:(2 * pp + 2) * Q_SUB]], axis=0)
            col = (g * rep // 2 + pp) * LANES
            o_ref[qb * Q_SUB:(qb + 1) * Q_SUB, col:col + LANES] = jnp.transpose(pair_t).astype(BF16)


def _attention(att_q, att_k, att_vt, bound, batch, t_len, ctx_len, bounded):
    tq = Q_TILE
    nq = t_len // tq
    rep = H_ATT // KV_ATT
    slabs, vrows, tk = att_vt.shape
    per_batch = slabs // batch
    n_chains = KV_ATT * (tq // Q_SUB)
    kern = functools.partial(_attn_kernel, ctx_len=ctx_len, t_len=t_len, bounded=bounded)
    width = H_ATT * HD_ATT
    return pl.pallas_call(
        kern,
        out_shape=jax.ShapeDtypeStruct((batch * t_len, width), BF16),
        grid=(batch, nq),
        in_specs=[
            pl.BlockSpec(memory_space=pltpu.SMEM),
            pl.BlockSpec((tq, width), lambda b, i: (b * nq + i, 0)),
            pl.BlockSpec((t_len, 2 * LANES), lambda b, i: (b, 0)),
            pl.BlockSpec((per_batch, vrows, tk), lambda b, i: (b, 0, 0)),
        ],
        out_specs=pl.BlockSpec((tq, width), lambda b, i: (b * nq + i, 0)),
        scratch_shapes=[
            pltpu.VMEM((n_chains, SUBLANES, rep * Q_SUB), F32),
            pltpu.VMEM((n_chains, SUBLANES, rep * Q_SUB), F32),
            pltpu.VMEM((n_chains, HD_ATT + ATT_ONES_ROWS, rep * Q_SUB), F32),
            pltpu.VMEM((n_chains, ATT_SLABS * tk, rep * Q_SUB), BF16),
        ],
        compiler_params=_params("arbitrary", "arbitrary"),
        name="attention_bounded" if bounded else "attention_online",
    )(bound, att_q, att_k, att_vt)


def _merge_kernel(*refs, alpha, n_x, nb, cb):
    x_refs = refs[:n_x]
    (mod_ref, yr_ref, ya_ref, ym_ref, mg_ref, wr_ref, wa_ref, wm_ref, wo_ref,
     lng_ref, lnb_ref, o_ref) = refs[n_x:]

    def branch(y_ref, w_ref, k):
        gate = mg_ref[:, k * D_MODEL:(k + 1) * D_MODEL].astype(F32)
        return _sigmoid(gate) * _dot(y_ref[...], w_ref[...])

    z = branch(yr_ref, wr_ref, 0) + branch(ya_ref, wa_ref, 1) + branch(ym_ref, wm_ref, 2)
    mix = _dot(z.astype(BF16), wo_ref[...])
    y = alpha * _read_rows(x_refs, nb, cb) + mod_ref[2:3, :] * mix
    o_ref[...] = _layer_norm(y) * lng_ref[...] + lnb_ref[...]


def _merge(xs, mod_tab, y_ret, y_att, y_m, mg, w_r, w_a, w_m, w_o, ln_g, ln_b, t_len, ctx_len, alpha):
    tm = ROW_TILE
    nb, cb, mod_map, _ = _row_maps(t_len, ctx_len, tm)
    x_args, x_specs, rows = _row_inputs(xs, t_len, ctx_len, tm)
    row_spec = lambda n: pl.BlockSpec((tm, n), lambda i: (i, 0))
    bw = y_ret.shape[1]
    return pl.pallas_call(
        functools.partial(_merge_kernel, alpha=alpha, n_x=len(x_args), nb=nb, cb=cb),
        out_shape=jax.ShapeDtypeStruct((rows, D_MODEL), F32),
        grid=(rows // tm,),
        in_specs=x_specs + [
            pl.BlockSpec((None, 6, D_MODEL), mod_map),
            row_spec(bw), row_spec(bw), row_spec(bw), row_spec(MERGE_W),
            _const_spec((bw, D_MODEL)), _const_spec((bw, D_MODEL)), _const_spec((bw, D_MODEL)),
            _const_spec((D_MODEL, D_MODEL)),
            _const_spec((1, D_MODEL)), _const_spec((1, D_MODEL)),
        ],
        out_specs=row_spec(D_MODEL),
        compiler_params=_params("arbitrary"),
        name="merge",
    )(*x_args, mod_tab, y_ret, y_att, y_m, mg, w_r, w_a, w_m, w_o, ln_g, ln_b)


def _ffn_kernel(x_ref, xp_ref, xn_ref, mod_ref, wu_ref, cw_ref, cb_ref, wd_ref, lng_ref, lnb_ref,
                o_ref, act_scr, *, alpha, nb, cb):
    tm = x_ref.shape[0]
    j = pl.program_id(0) % nb
    prev_ok = jnp.logical_and(j != 0, j != cb)
    next_ok = jnp.logical_and(j != cb - 1, j != nb - 1)
    shift = mod_ref[3:4, :]
    scale = 1.0 + mod_ref[4:5, :]

    def mod(xv):
        return _layer_norm(xv) * scale + shift

    x = x_ref[...]
    hp = jnp.where(prev_ok, mod(xp_ref[...]), 0.0)
    hn = jnp.where(next_ok, mod(xn_ref[...]), 0.0)
    hext = jnp.concatenate([hp, mod(x), hn], axis=0).astype(BF16)

    def conv(u, lo, hi):
        w = cw_ref[:, lo:hi]
        return (u[SUBLANES - 1:SUBLANES - 1 + tm] * w[0:1] + u[SUBLANES:SUBLANES + tm] * w[1:2]
                + u[SUBLANES + 1:SUBLANES + 1 + tm] * w[2:3] + cb_ref[:, lo:hi])

    def up(c):
        lo, hi = c * FF_CHUNK, (c + 1) * FF_CHUNK
        return _dot(hext, wu_ref[:, lo:hi]), _dot(hext, wu_ref[:, D_FF + lo:D_FF + hi])

    n_chunks = D_FF // FF_CHUNK
    split = (n_chunks + 1) // 2
    acc = None
    ua, ug = up(0)
    for c in range(n_chunks):
        nxt = up(c + 1) if c + 1 < n_chunks else None
        lo, hi = c * FF_CHUNK, (c + 1) * FF_CHUNK
        a = conv(ua, lo, hi)
        g = conv(ug, D_FF + lo, D_FF + hi)
        act_scr[:, lo:hi] = (g * _sigmoid(g) * a).astype(BF16)
        if c + 1 == split:
            acc = _dot(act_scr[:, 0:hi], wd_ref[0:hi, :])
        if nxt is not None:
            ua, ug = nxt
    acc = acc + _dot(act_scr[:, split * FF_CHUNK:D_FF], wd_ref[split * FF_CHUNK:D_FF, :])
    y = alpha * x + mod_ref[5:6, :] * acc
    o_ref[...] = _layer_norm(y) * lng_ref[...] + lnb_ref[...]


def _ffn(xs, mod_tab, w_up, conv_w, conv_b, w_down, ln_g, ln_b, t_len, ctx_len, alpha, latent_only):
    rows = xs.shape[0]
    tm = ROW_TILE
    nb, cb, mod_map, _ = _row_maps(t_len, ctx_len, tm)
    per = tm // SUBLANES
    last = rows // SUBLANES - 1
    if latent_only:
        out_rows = rows // nb * (nb - cb)
        out_map = lambda i: ((i // nb) * (nb - cb) + jnp.maximum(i % nb - cb, 0), 0)
    else:
        out_rows = rows
        out_map = lambda i: (i, 0)
    return pl.pallas_call(
        functools.partial(_ffn_kernel, alpha=alpha, nb=nb, cb=cb),
        out_shape=jax.ShapeDtypeStruct((out_rows, D_MODEL), F32),
        grid=(rows // tm,),
        in_specs=[
            pl.BlockSpec((tm, D_MODEL), lambda i: (i, 0)),
            pl.BlockSpec((SUBLANES, D_MODEL), lambda i: (jnp.maximum(i * per - 1, 0), 0)),
            pl.BlockSpec((SUBLANES, D_MODEL), lambda i: (jnp.minimum((i + 1) * per, last), 0)),
            pl.BlockSpec((None, 6, D_MODEL), mod_map),
            _const_spec((D_MODEL, 2 * D_FF)),
            _const_spec((3, 2 * D_FF)),
            _const_spec((1, 2 * D_FF)),
            _const_spec((D_FF, D_MODEL)),
            _const_spec((1, D_MODEL)), _const_spec((1, D_MODEL)),
        ],
        out_specs=pl.BlockSpec((tm, D_MODEL), out_map),
        scratch_shapes=[pltpu.VMEM((tm, D_FF), BF16)],
        compiler_params=_params("arbitrary"),
        name="conv_ffn",
    )(xs, xs, xs, mod_tab, w_up, conv_w, conv_b, w_down, ln_g, ln_b)


def _rope_tables(seq, ctx_len):
    rows = seq // GRID_W
    row = jnp.repeat(jnp.arange(rows), GRID_W).astype(F32)
    col = jnp.tile(jnp.arange(GRID_W), rows).astype(F32)
    n_freq = HD_ATT // 4
    freqs = ROPE_THETA ** (-jnp.arange(n_freq, dtype=F32) / n_freq)
    ar = row[:, None] * freqs[None, :]
    ac = col[:, None] * freqs[None, :]
    cos = jnp.concatenate([jnp.cos(ar), jnp.cos(ar), jnp.cos(ac), jnp.cos(ac)], axis=-1)
    sin = jnp.concatenate([-jnp.sin(ar), jnp.sin(ar), -jnp.sin(ac), jnp.sin(ac)], axis=-1)
    cos = jnp.concatenate([jnp.ones((ctx_len, HD_ATT), F32), cos], axis=0)
    sin = jnp.concatenate([jnp.zeros((ctx_len, HD_ATT), F32), sin], axis=0)
    return jnp.tile(cos, (1, 2)), jnp.tile(sin, (1, 2))


def kernel(x, c, ctx, c_ctx, w_mod, b_mod, w_in, b_in, ret_decay_logit, ret_gn_g, attn_qn_g, attn_kn_g,
           mlstm_gn_g, w_br_ret, w_br_att, w_br_mlstm, w_out, ln1_g, ln1_b, w_up, conv_w, conv_b, w_down,
           ln2_g, ln2_b):
    batch, seq, _ = x.shape
    ctx_len = ctx.shape[1]
    depth = w_mod.shape[0]
    t_len = ctx_len + seq
    alpha = (2.0 * depth) ** 0.25
    assert ctx_len == ROW_TILE and seq % (ATT_SLABS * ROW_TILE) == 0
    assert seq % GRID_W == 0 and batch + 1 <= SUBLANES

    xs = (ctx.reshape(batch * ctx_len, D_MODEL), x.reshape(batch * seq, D_MODEL))

    cc = jnp.zeros((SUBLANES, D_MODEL), F32).at[:batch].set(c).at[batch].set(c_ctx)
    mod = _modulation(cc, w_mod, b_mod)
    mod_lat = mod[:, :batch].reshape(depth, batch, 1, 6, D_MODEL)
    mod_ctx = jnp.broadcast_to(mod[:, batch].reshape(depth, 1, 1, 6, D_MODEL),
                               (depth, batch, 1, 6, D_MODEL))
    mod_tab = jnp.concatenate([mod_ctx, mod_lat], axis=2).reshape(depth, batch * 2, 6, D_MODEL)

    cos_t, sin_t = _rope_tables(seq, ctx_len)
    n_gate = 4 * H_M
    split = OFF_GATE + n_gate

    def pad_proj(a):
        pad = jnp.zeros(a.shape[:-1] + (GATE_W - n_gate,), a.dtype)
        return jnp.concatenate([a[..., :split], pad, a[..., split:]], axis=-1)

    b_in_p = pad_proj(b_in).reshape(depth, 1, N_PROJ)
    dl = jnp.broadcast_to(ret_decay_logit.reshape(depth, 2 * H_RET, 1), (depth, 2 * H_RET, LANES))
    qg = jnp.tile(attn_qn_g, (1, 2)).reshape(depth, 1, LANES)
    kg = jnp.tile(attn_kn_g, (1, 2)).reshape(depth, 1, LANES)
    row2 = lambda a, l: a[l].reshape(1, -1)
    cast = lambda w, l: w[l].astype(BF16)

    for l in range(depth):
        ret_p, att_q, att_k, att_vt, mls_p, gates, mg = _proj_in(
            xs, mod_tab[l], pad_proj(w_in[l]).astype(BF16), b_in_p[l], cos_t, sin_t, qg[l], kg[l],
            t_len, ctx_len)
        y_ret = _retention(ret_p, dl[l], row2(ret_gn_g, l), batch, t_len, ctx_len)
        bound = (1.01 * LOG2_E * HD_ATT ** 0.5 * jnp.max(jnp.abs(attn_qn_g[l]))
                 * jnp.max(jnp.abs(attn_kn_g[l]))).reshape(1)
        attend = lambda fixed: functools.partial(
            _attention, batch=batch, t_len=t_len, ctx_len=ctx_len, bounded=fixed)
        y_att = lax.cond(bound[0] <= ATT_SAFE_BOUND, attend(True), attend(False),
                         att_q, att_k, att_vt, bound)
        y_m = _mlstm(mls_p, gates, row2(mlstm_gn_g, l), batch, t_len, ctx_len)
        xs = _merge(xs, mod_tab[l], y_ret, y_att, y_m, mg, cast(w_br_ret, l), cast(w_br_att, l),
                    cast(w_br_mlstm, l), cast(w_out, l), row2(ln1_g, l), row2(ln1_b, l),
                    t_len, ctx_len, alpha)
        xs = _ffn(xs, mod_tab[l], cast(w_up, l), conv_w[l], row2(conv_b, l), cast(w_down, l),
                  row2(ln2_g, l), row2(ln2_b, l), t_len, ctx_len, alpha,
                  latent_only=l == depth - 1)

    return xs.reshape(batch, seq, D_MODEL)
```

```python
import functools

import jax
import jax.numpy as jnp
import numpy as np
from jax import lax
from jax.experimental import pallas as pl
from jax.experimental.pallas import tpu as pltpu

F32 = jnp.float32
BF16 = jnp.bfloat16

D_MODEL = 1024
H_RET, DK_RET, DV_RET = 4, 64, 128
H_ATT, KV_ATT, HD_ATT = 8, 2, 64
H_M, DK_M, DV_M = 4, 64, 128
CHUNK = 128
D_FF = 2816
GRID_W = 64
ROPE_THETA = 10000.0
NORM_EPS = 1e-6
LOG2_E = 1.4426950408889634

LANES = 128
SUBLANES = 8
ROW_TILE = 256
Q_TILE = 256
Q_SUB = 128
ATT_SAFE_BOUND = 40.0
ATT_SLABS = 2
ATT_ONES_ROWS = 16
FF_CHUNK = 256
MOD_COLS = 1536
VMEM_LIMIT = 56 * 1024 * 1024

RET_W = 2 * H_RET * DK_RET + 2 * H_RET * DV_RET
ATT_W = (H_ATT + 2 * KV_ATT) * HD_ATT
MLS_W = 2 * H_M * DK_M + 2 * H_M * DV_M
GATE_W = LANES
MERGE_W = 3 * D_MODEL
OFF_ATT = RET_W
OFF_MLS = OFF_ATT + ATT_W
OFF_GATE = OFF_MLS + MLS_W
OFF_MERGE = OFF_GATE + GATE_W
N_PROJ = OFF_MERGE + MERGE_W


def _dot(a, b):
    return jnp.dot(a, b, preferred_element_type=F32)


def _dot_nt(a, b):
    return lax.dot_general(a, b, (((1,), (1,)), ((), ())), preferred_element_type=F32)


def _dot_tn(a, b):
    return lax.dot_general(a, b, (((0,), (0,)), ((), ())), preferred_element_type=F32)


def _layer_norm(x):
    mu = jnp.mean(x, axis=-1, keepdims=True)
    xc = x - mu
    var = jnp.mean(xc * xc, axis=-1, keepdims=True)
    return xc * lax.rsqrt(var + NORM_EPS)


def _log_sigmoid(x):
    return jnp.minimum(x, 0.0) - jnp.log1p(jnp.exp(-jnp.abs(x)))


def _sigmoid(x):
    return 1.0 / (1.0 + jnp.exp(-x))


def _split3(x):
    hi = x.astype(BF16)
    r1 = x - hi.astype(F32)
    mid = r1.astype(BF16)
    lo = (r1 - mid.astype(F32)).astype(BF16)
    return hi, mid, lo


def _const_spec(shape):
    zeros = (0,) * len(shape)
    return pl.BlockSpec(shape, lambda *_: zeros, pipeline_mode=pl.Buffered(1))


def _params(*sem):
    return pltpu.CompilerParams(dimension_semantics=sem, vmem_limit_bytes=VMEM_LIMIT)


def _mod_kernel(c_ref, w_ref, b_ref, o_ref):
    c = c_ref[...]
    s = (c * _sigmoid(c)).astype(BF16)
    o_ref[0] = _dot(s, w_ref[0].astype(BF16)) + b_ref[0]


def _modulation(cc, w_mod, b_mod):
    depth, _, n = w_mod.shape
    rows = cc.shape[0]
    return pl.pallas_call(
        _mod_kernel,
        out_shape=jax.ShapeDtypeStruct((depth, rows, n), F32),
        grid=(depth, n // MOD_COLS),
        in_specs=[
            pl.BlockSpec((rows, D_MODEL), lambda l, j: (0, 0)),
            pl.BlockSpec((1, D_MODEL, MOD_COLS), lambda l, j: (l, 0, j)),
            pl.BlockSpec((1, 1, MOD_COLS), lambda l, j: (l, 0, j)),
        ],
        out_specs=pl.BlockSpec((1, rows, MOD_COLS), lambda l, j: (l, 0, j)),
        compiler_params=_params("arbitrary", "arbitrary"),
        name="modulation",
    )(cc, w_mod, b_mod.reshape(depth, 1, n))


def _head_sumsq(x, ones_bd):
    sq = x * x
    hi = sq.astype(BF16)
    lo = (sq - hi.astype(F32)).astype(BF16)
    return _dot(hi, ones_bd) + _dot(lo, ones_bd)


def _norm_rope(x, gain, cos, sin, ones_bd, lane, scale):
    y = x * lax.rsqrt(_head_sumsq(x, ones_bd) * (1.0 / HD_ATT) + NORM_EPS) * gain
    above = pltpu.roll(y, LANES - 16, 1)
    below = pltpu.roll(y, 16, 1)
    partner = jnp.where((lane & 16) == 0, above, below)
    return (y * cos + partner * sin) * scale


def _proj_in_kernel(*refs, n_x, nb, cb):
    x_refs = refs[:n_x]
    (mod_ref, w_ref, b_ref, cos_ref, sin_ref, qg_ref, kg_ref,
     ret_ref, attq_ref, attk_ref, attvt_ref, mls_ref, gate_ref, merge_ref) = refs[n_x:]
    tm = x_refs[0].shape[0]
    x = _read_rows(x_refs, nb, cb)
    h = (_layer_norm(x) * (1.0 + mod_ref[1:2, :]) + mod_ref[0:1, :]).astype(BF16)

    def seg(lo, hi):
        return _dot(h, w_ref[:, lo:hi]) + b_ref[:, lo:hi]

    def store_qkvx(dst_ref, r, gate_fn):
        dst_ref[:, 0:256] = r[:, 0:256].astype(BF16)
        dst_ref[:, 256:512] = (r[:, 256:512] * 0.125).astype(BF16)
        dst_ref[:, 512:1024] = r[:, 512:1024].astype(BF16)
        dst_ref[:, 1024:1536] = gate_fn(r[:, 1024:1536]).astype(BF16)

    store_qkvx(ret_ref, seg(0, RET_W), lambda g: g * _sigmoid(g))
    store_qkvx(mls_ref, seg(OFF_MLS, OFF_MLS + MLS_W), _sigmoid)
    gate_ref[...] = seg(OFF_GATE, OFF_GATE + GATE_W)
    half = MERGE_W // 2
    merge_ref[:, 0:half] = _sigmoid(seg(OFF_MERGE, OFF_MERGE + half)).astype(BF16)
    merge_ref[:, half:MERGE_W] = _sigmoid(seg(OFF_MERGE + half, OFF_MERGE + MERGE_W)).astype(BF16)

    a = seg(OFF_ATT, OFF_ATT + ATT_W)
    lane = lax.broadcasted_iota(jnp.int32, (tm, LANES), 1)
    ri = lax.broadcasted_iota(jnp.int32, (LANES, LANES), 0)
    ci = lax.broadcasted_iota(jnp.int32, (LANES, LANES), 1)
    ones_bd = jnp.where((ri // HD_ATT) == (ci // HD_ATT), 1.0, 0.0).astype(BF16)
    cos = cos_ref[...]
    sin = sin_ref[...]
    qg = qg_ref[...]
    for j in range(H_ATT * HD_ATT // LANES):
        qj = _norm_rope(a[:, j * LANES:(j + 1) * LANES], qg, cos, sin, ones_bd, lane,
                        HD_ATT ** -0.5 * LOG2_E)
        attq_ref[:, j * LANES:(j + 1) * LANES] = qj.astype(BF16)
    kn = _norm_rope(a[:, 512:640], kg_ref[...], cos, sin, ones_bd, lane, 1.0)
    kr = pltpu.roll(kn, HD_ATT, 1)
    low = lane < HD_ATT
    attk_ref[:, 0:128] = jnp.where(low, kn, kr).astype(BF16)
    attk_ref[:, 128:256] = jnp.where(low, kr, kn).astype(BF16)
    attvt_ref[0] = jnp.transpose(a[:, 640:768]).astype(BF16)


def _row_maps(t_len, ctx_len, tm):
    nb = t_len // tm
    cb = ctx_len // tm

    def mod_map(i):
        return ((i // nb) * 2 + ((i % nb) >= cb).astype(jnp.int32), 0, 0)

    def pos_map(i):
        return (i % nb, 0)

    return nb, cb, mod_map, pos_map


def _row_inputs(xs, t_len, ctx_len, tm):
    nb, cb = t_len // tm, ctx_len // tm
    if not isinstance(xs, tuple):
        return [xs], [pl.BlockSpec((tm, D_MODEL), lambda i: (i, 0))], xs.shape[0]
    ctx_map = lambda i: ((i // nb) * cb + jnp.minimum(i % nb, cb - 1), 0)
    lat_map = lambda i: ((i // nb) * (nb - cb) + jnp.maximum(i % nb - cb, 0), 0)
    specs = [pl.BlockSpec((tm, D_MODEL), ctx_map), pl.BlockSpec((tm, D_MODEL), lat_map)]
    return list(xs), specs, xs[0].shape[0] + xs[1].shape[0]


def _read_rows(x_refs, nb, cb):
    if len(x_refs) == 1:
        return x_refs[0][...]
    return jnp.where(pl.program_id(0) % nb < cb, x_refs[0][...], x_refs[1][...])


def _proj_in(xs, mod_tab, w, b, cos_t, sin_t, qg, kg, t_len, ctx_len):
    tm = ROW_TILE
    nb, cb, mod_map, pos_map = _row_maps(t_len, ctx_len, tm)
    x_args, x_specs, rows = _row_inputs(xs, t_len, ctx_len, tm)
    row_spec = lambda n: pl.BlockSpec((tm, n), lambda i: (i, 0))
    out_shapes = [
        jax.ShapeDtypeStruct((rows, RET_W), BF16),
        jax.ShapeDtypeStruct((rows, H_ATT * HD_ATT), BF16),
        jax.ShapeDtypeStruct((rows, 2 * LANES), BF16),
        jax.ShapeDtypeStruct((rows // tm, KV_ATT * HD_ATT, tm), BF16),
        jax.ShapeDtypeStruct((rows, MLS_W), BF16),
        jax.ShapeDtypeStruct((rows, GATE_W), F32),
        jax.ShapeDtypeStruct((rows, MERGE_W), BF16),
    ]

    def out_spec(s):
        if len(s.shape) == 3:
            return pl.BlockSpec((1,) + s.shape[1:], lambda i: (i, 0, 0))
        return row_spec(s.shape[1])

    return pl.pallas_call(
        functools.partial(_proj_in_kernel, n_x=len(x_args), nb=nb, cb=cb),
        out_shape=out_shapes,
        grid=(rows // tm,),
        in_specs=x_specs + [
            pl.BlockSpec((None, 6, D_MODEL), mod_map),
            _const_spec((D_MODEL, N_PROJ)),
            _const_spec((1, N_PROJ)),
            pl.BlockSpec((tm, LANES), pos_map),
            pl.BlockSpec((tm, LANES), pos_map),
            _const_spec((1, LANES)),
            _const_spec((1, LANES)),
        ],
        out_specs=[out_spec(s) for s in out_shapes],
        compiler_params=_params("arbitrary"),
        name="proj_in",
    )(*x_args, mod_tab, w, b, cos_t, sin_t, qg, kg)


def _scan_chunk(d, j, n_ctx, n_chunks):
    if d == 0:
        return j
    return jnp.where(j < n_ctx, n_ctx - 1 - j, n_chunks + n_ctx - 1 - j)


def _run_scan_steps(body, n_ctx, n_chunks):
    for lo, hi in ((0, n_ctx), (n_ctx, n_chunks)):
        mid = lo + (hi - lo) // 2
        for start, stop, readout in ((lo, mid, False), (mid, hi, True)):
            if stop > start:
                lax.fori_loop(start, stop, functools.partial(body, readout=readout), 0)


def _head_norm(o):
    mu = jnp.mean(o, axis=-1, keepdims=True)
    oc = o - mu
    var = jnp.mean(oc * oc, axis=-1, keepdims=True)
    return oc * lax.rsqrt(var + NORM_EPS)


def _ret_kernel(p_ref, dl_ref, gn_ref, y_ref, o_scr, s_scr, intra_scr, qd_scr, kd_scr, cd_scr,
                *, n_ctx, n_chunks):
    lg = _log_sigmoid(dl_ref[...])
    cd_scr[...] = jnp.exp(float(CHUNK) * lg)
    ri = lax.broadcasted_iota(jnp.int32, (CHUNK, CHUNK), 0).astype(F32)
    ci = lax.broadcasted_iota(jnp.int32, (CHUNK, CHUNK), 1).astype(F32)
    for d in range(2):
        diff = ri - ci if d == 0 else ci - ri
        qpos = ri + 1.0 if d == 0 else float(CHUNK) - ri
        kpos = float(CHUNK - 1) - ri if d == 0 else ri
        keep = diff >= 0.0
        for h in range(H_RET):
            r = d * H_RET + h
            lgb = lg[r:r + 1, :]
            intra_scr[r] = jnp.where(keep, jnp.exp(jnp.where(keep, diff, 0.0) * lgb), 0.0)
            qd_scr[r] = jnp.exp(qpos * lgb)
            kd_scr[r] = jnp.exp(kpos * lgb)

    lane = lax.broadcasted_iota(jnp.int32, (CHUNK, LANES), 1)
    half_mask = (lane < DK_RET, lane >= DK_RET)

    s_scr[...] = jnp.zeros_like(s_scr)
    o_scr[...] = jnp.zeros_like(o_scr)

    def body(j, carry, readout):
        staged = []
        for d in range(2):
            c = _scan_chunk(d, j, n_ctx, n_chunks)
            rows = pl.ds(pl.multiple_of(c * CHUNK, CHUNK), CHUNK)
            for pair in range(H_RET // 2):
                qp = p_ref[rows, pair * LANES:(pair + 1) * LANES]
                kp = p_ref[rows, 256 + pair * LANES:256 + (pair + 1) * LANES]
                kpf = kp.astype(F32)
                for hh in range(2):
                    h = pair * 2 + hh
                    r = d * H_RET + h
                    qm = jnp.where(half_mask[hh], qp, jnp.zeros_like(qp))
                    v = p_ref[rows, 512 + h * DV_RET:512 + (h + 1) * DV_RET]
                    s = (_dot_nt(qm, kp) * intra_scr[r]).astype(BF16)
                    state = s_scr[r]
                    inter = _dot(qm, state.astype(BF16)) * qd_scr[r]
                    kdec = (kpf * kd_scr[r]).astype(BF16)
                    s_scr[r] = state * cd_scr[r:r + 1, :] + _dot_tn(kdec, v)
                    staged.append((rows, h, s, v, inter))
        for rows, h, s, v, inter in staged:
            cols = slice(h * DV_RET, (h + 1) * DV_RET)
            tot = o_scr[rows, cols] + (_dot(s, v) + inter)
            o_scr[rows, cols] = tot
            if readout:
                g = p_ref[rows, 1024 + h * DV_RET:1024 + (h + 1) * DV_RET].astype(F32)
                y = _head_norm(tot) * gn_ref[:, cols] * g
                y_ref[rows, cols] = y.astype(BF16)
        return carry

    _run_scan_steps(body, n_ctx, n_chunks)


def _retention(ret_p, dl, gn, batch, t_len, ctx_len):
    n_chunks = t_len // CHUNK
    kern = functools.partial(_ret_kernel, n_ctx=ctx_len // CHUNK, n_chunks=n_chunks)
    width = H_RET * DV_RET
    return pl.pallas_call(
        kern,
        out_shape=jax.ShapeDtypeStruct((batch * t_len, width), BF16),
        grid=(batch,),
        in_specs=[
            pl.BlockSpec((t_len, RET_W), lambda b: (b, 0)),
            _const_spec((2 * H_RET, LANES)),
            _const_spec((1, width)),
        ],
        out_specs=pl.BlockSpec((t_len, width), lambda b: (b, 0)),
        scratch_shapes=[
            pltpu.VMEM((t_len, width), F32),
            pltpu.VMEM((2 * H_RET, LANES, DV_RET), F32),
            pltpu.VMEM((2 * H_RET, CHUNK, CHUNK), F32),
            pltpu.VMEM((2 * H_RET, CHUNK, CHUNK), F32),
            pltpu.VMEM((2 * H_RET, CHUNK, CHUNK), F32),
            pltpu.VMEM((2 * H_RET, LANES), F32),
        ],
        compiler_params=_params("arbitrary"),
        name="retention",
    )(ret_p, dl, gn)


def _mlstm_kernel(p_ref, gate_ref, gn_ref, y_ref, h_scr, cn_scr, m_scr, *, n_ctx, n_chunks):
    ri = lax.broadcasted_iota(jnp.int32, (CHUNK, CHUNK), 0)
    ci = lax.broadcasted_iota(jnp.int32, (CHUNK, CHUNK), 1)
    lane = lax.broadcasted_iota(jnp.int32, (CHUNK, LANES), 1)
    half_mask = (lane < DK_M, lane >= DK_M)
    ones_v = jnp.ones((CHUNK, LANES), BF16)

    cn_scr[...] = jnp.zeros_like(cn_scr)
    m_scr[...] = jnp.zeros_like(m_scr)
    h_scr[...] = jnp.zeros_like(h_scr)
    keeps = (ri >= ci, ci >= ri)
    tris = tuple(jnp.where(k, 1.0, 0.0).astype(BF16) for k in keeps)
    full = (CHUNK, LANES)

    def body(j, carry, readout):
        staged = []
        for d in range(2):
            keep = keeps[d]
            end_row = CHUNK - 1 if d == 0 else 0
            c = _scan_chunk(d, j, n_ctx, n_chunks)
            rows = pl.ds(pl.multiple_of(c * CHUNK, CHUNK), CHUNK)
            gates = gate_ref[rows, :]
            hi, mid, lo = _split3(_log_sigmoid(gates))
            bcum = _dot(tris[d], hi) + _dot(tris[d], mid) + _dot(tris[d], lo)
            cdiff = gates - pltpu.roll(bcum, LANES - 2 * H_M, 1)
            cdiff_t = jnp.transpose(cdiff)
            for pair in range(H_M // 2):
                qp = p_ref[rows, pair * LANES:(pair + 1) * LANES]
                kp = p_ref[rows, 256 + pair * LANES:256 + (pair + 1) * LANES]
                kpf = kp.astype(F32)
                for hh in range(2):
                    h = pair * 2 + hh
                    r = d * H_M + h
                    gb = 2 * H_M + r
                    b_l = jnp.broadcast_to(bcum[:, gb:gb + 1], full)
                    c_l = jnp.broadcast_to(cdiff[:, r:r + 1], full)
                    c_row = cdiff_t[r:r + 1, :]
                    b_end = b_l[end_row:end_row + 1, :]
                    m_prev = m_scr[r][0:1, :]
                    qm = jnp.where(half_mask[hh], qp, jnp.zeros_like(qp))
                    v = p_ref[rows, 512 + h * DV_M:512 + (h + 1) * DV_M]
                    vext = jnp.concatenate([v, ones_v], axis=1)

                    dmat = jnp.where(keep, b_l + c_row, -jnp.inf)
                    inter = b_l + m_prev
                    row_max = jnp.broadcast_to(jnp.max(dmat, axis=-1, keepdims=True), full)
                    m_t = jnp.maximum(inter, row_max)
                    s = (_dot_nt(qm, kp) * jnp.exp(dmat - m_t)).astype(BF16)
                    a_int = jnp.exp(inter - m_t)
                    state = cn_scr[r]
                    qcn = _dot(qm, state.astype(BF16))

                    g_l = b_end + c_l
                    m_new = jnp.maximum(b_end + m_prev, jnp.max(g_l, axis=0, keepdims=True))
                    kdec = (kpf * jnp.exp(g_l - m_new)).astype(BF16)
                    carry_scale = jnp.exp(b_end + m_prev - m_new)
                    cn_scr[r] = (jnp.concatenate([carry_scale, carry_scale], axis=1) * state
                                 + _dot_tn(kdec, vext))
                    m_scr[r] = jnp.broadcast_to(m_new, (SUBLANES, LANES))
                    staged.append((rows, h, s, vext, a_int, qcn, m_t))
        for rows, h, s, vext, a_int, qcn, m_t in staged:
            cols = slice(h * DV_M, (h + 1) * DV_M)
            sv = _dot(s, vext)
            num = sv[:, 0:DV_M] + a_int * qcn[:, 0:DV_M]
            den = sv[:, DV_M:2 * DV_M] + a_int * qcn[:, DV_M:2 * DV_M]
            hval = num / jnp.maximum(jnp.abs(den), jnp.exp(-m_t))
            tot = h_scr[rows, cols] + hval
            h_scr[rows, cols] = tot
            if readout:
                og = p_ref[rows, 1024 + h * DV_M:1024 + (h + 1) * DV_M].astype(F32)
                y = og * (_head_norm(tot) * gn_ref[:, cols])
                y_ref[rows, cols] = y.astype(BF16)
        return carry

    _run_scan_steps(body, n_ctx, n_chunks)


def _mlstm(mls_p, gates, gn, batch, t_len, ctx_len):
    n_chunks = t_len // CHUNK
    kern = functools.partial(_mlstm_kernel, n_ctx=ctx_len // CHUNK, n_chunks=n_chunks)
    width = H_M * DV_M
    return pl.pallas_call(
        kern,
        out_shape=jax.ShapeDtypeStruct((batch * t_len, width), BF16),
        grid=(batch,),
        in_specs=[
            pl.BlockSpec((t_len, MLS_W), lambda b: (b, 0)),
            pl.BlockSpec((t_len, GATE_W), lambda b: (b, 0)),
            _const_spec((1, width)),
        ],
        out_specs=pl.BlockSpec((t_len, width), lambda b: (b, 0)),
        scratch_shapes=[
            pltpu.VMEM((t_len, width), F32),
            pltpu.VMEM((2 * H_M, LANES, 2 * LANES), F32),
            pltpu.VMEM((2 * H_M, SUBLANES, LANES), F32),
        ],
        compiler_params=_params("arbitrary"),
        name="mlstm",
    )(mls_p, gates, gn)


def _attn_kernel(bound_ref, q_ref, k_ref, vt_ref, o_ref, m_scr, alpha_scr, acc_scr, p_scr,
                 *, ctx_len, t_len, bounded):
    tq = q_ref.shape[0]
    tk = vt_ref.shape[2]
    rep = H_ATT // KV_ATT
    is_ctx = pl.program_id(1) < ctx_len // tq
    lane = lax.broadcasted_iota(jnp.int32, (Q_SUB, LANES), 1)
    low = lane < HD_ATT
    bound = bound_ref[0]

    chains = [(g, qb) for g in range(KV_ATT) for qb in range(tq // Q_SUB)]
    qs = []
    for g, qb in chains:
        parts = []
        for hh in range(rep):
            h = g * rep + hh
            qp = q_ref[qb * Q_SUB:(qb + 1) * Q_SUB, (h // 2) * LANES:(h // 2 + 1) * LANES]
            parts.append(jnp.where(low if h % 2 == 0 else ~low, qp, jnp.zeros_like(qp)))
        qs.append(jnp.concatenate(parts, axis=0))
    if not bounded:
        m_scr[...] = jnp.full_like(m_scr, -jnp.inf)
        alpha_scr[...] = jnp.ones_like(alpha_scr)
    acc_scr[...] = jnp.zeros_like(acc_scr)
    p_scr[:, tk:, :] = jnp.zeros((len(chains), (ATT_SLABS - 1) * tk, rep * Q_SUB), BF16)
    shift = 0.5 * bound

    def v_rows(g, first_slab, n_slabs):
        vt = [vt_ref[first_slab + i][g * HD_ATT:(g + 1) * HD_ATT, :] for i in range(n_slabs)]
        vt = vt[0] if n_slabs == 1 else jnp.concatenate(vt, axis=1)
        return jnp.concatenate([vt, jnp.ones((ATT_ONES_ROWS, n_slabs * tk), BF16)], axis=0)

    def probabilities(c, s_t):
        if bounded:
            return jnp.exp2((s_t - shift).astype(BF16))
        m_old = m_scr[c, 0:1, :]
        m_new = jnp.maximum(m_old, jnp.max(s_t, axis=0, keepdims=True))
        alpha_scr[c, 0:1, :] = jnp.exp2(m_old - m_new)
        m_scr[c, 0:1, :] = m_new
        return jnp.exp2(s_t - m_new).astype(BF16)

    def accumulate(c, pv):
        if bounded:
            acc_scr[c] += pv
        else:
            acc_scr[c] = alpha_scr[c, 0:1, :] * acc_scr[c] + pv

    for c, (g, _) in enumerate(chains):
        s_t = _dot_nt(k_ref[0:tk, g * LANES:(g + 1) * LANES], qs[c])
        p_scr[c, 0:tk] = probabilities(c, s_t)

    def pending_slab(step):
        return jnp.maximum(ATT_SLABS * step - 1, 0)

    def body(t, carry):
        keys = pl.ds(pl.multiple_of(tk + t * (ATT_SLABS * tk), tk), ATT_SLABS * tk)
        scores = [_dot_nt(k_ref[keys, g * LANES:(g + 1) * LANES], qs[c])
                  for c, (g, _) in enumerate(chains)]
        for c, (g, _) in enumerate(chains):
            accumulate(c, _dot(v_rows(g, pending_slab(t), ATT_SLABS), p_scr[c]))
        for c, s_t in enumerate(scores):
            p_scr[c] = probabilities(c, s_t)
        return carry

    n_steps = jnp.where(is_ctx, 0, (t_len - tk) // (ATT_SLABS * tk))
    lax.fori_loop(0, n_steps, body, 0)
    for c, (g, _) in enumerate(chains):
        accumulate(c, _dot(v_rows(g, pending_slab(n_steps), ATT_SLABS), p_scr[c]))

    for c, (g, qb) in enumerate(chains):
        acc = acc_scr[c]
        out_t = acc[0:HD_ATT] / acc[HD_ATT:HD_ATT + 1]
        for pp in range(rep // 2):
            pair_t = jnp.concatenate([out_t[:, (2 * pp) * Q_SUB:(2 * pp + 1) * Q_SUB],
                                      out_t[:, (2 * pp + 1) * Q_SUB:(2 * pp + 2) * Q_SUB]], axis=0)
            col = (g * rep // 2 + pp) * LANES
            o_ref[qb * Q_SUB:(qb + 1) * Q_SUB, col:col + LANES] = jnp.transpose(pair_t).astype(BF16)


def _attention(att_q, att_k, att_vt, bound, batch, t_len, ctx_len, bounded):
    tq = Q_TILE
    nq = t_len // tq
    rep = H_ATT // KV_ATT
    slabs, vrows, tk = att_vt.shape
    per_batch = slabs // batch
    n_chains = KV_ATT * (tq // Q_SUB)
    kern = functools.partial(_attn_kernel, ctx_len=ctx_len, t_len=t_len, bounded=bounded)
    width = H_ATT * HD_ATT
    return pl.pallas_call(
        kern,
        out_shape=jax.ShapeDtypeStruct((batch * t_len, width), BF16),
        grid=(batch, nq),
        in_specs=[
            pl.BlockSpec(memory_space=pltpu.SMEM),
            pl.BlockSpec((tq, width), lambda b, i: (b * nq + i, 0)),
            pl.BlockSpec((t_len, 2 * LANES), lambda b, i: (b, 0)),
            pl.BlockSpec((per_batch, vrows, tk), lambda b, i: (b, 0, 0)),
        ],
        out_specs=pl.BlockSpec((tq, width), lambda b, i: (b * nq + i, 0)),
        scratch_shapes=[
            pltpu.VMEM((n_chains, SUBLANES, rep * Q_SUB), F32),
            pltpu.VMEM((n_chains, SUBLANES, rep * Q_SUB), F32),
            pltpu.VMEM((n_chains, HD_ATT + ATT_ONES_ROWS, rep * Q_SUB), F32),
            pltpu.VMEM((n_chains, ATT_SLABS * tk, rep * Q_SUB), BF16),
        ],
        compiler_params=_params("arbitrary", "arbitrary"),
        name="attention_bounded" if bounded else "attention_online",
    )(bound, att_q, att_k, att_vt)


def _merge_kernel(*refs, alpha, n_x, nb, cb):
    x_refs = refs[:n_x]
    (mod_ref, yr_ref, ya_ref, ym_ref, mg_ref, wr_ref, wa_ref, wm_ref, wo_ref,
     lng_ref, lnb_ref, o_ref) = refs[n_x:]

    def branch(y_ref, w_ref, k):
        gate = mg_ref[:, k * D_MODEL:(k + 1) * D_MODEL].astype(F32)
        return gate * _dot(y_ref[...], w_ref[...])

    z = branch(yr_ref, wr_ref, 0) + branch(ya_ref, wa_ref, 1) + branch(ym_ref, wm_ref, 2)
    mix = _dot(z.astype(BF16), wo_ref[...])
    y = alpha * _read_rows(x_refs, nb, cb) + mod_ref[2:3, :] * mix
    o_ref[...] = _layer_norm(y) * lng_ref[...] + lnb_ref[...]


def _merge(xs, mod_tab, y_ret, y_att, y_m, mg, w_r, w_a, w_m, w_o, ln_g, ln_b, t_len, ctx_len, alpha):
    tm = ROW_TILE
    nb, cb, mod_map, _ = _row_maps(t_len, ctx_len, tm)
    x_args, x_specs, rows = _row_inputs(xs, t_len, ctx_len, tm)
    row_spec = lambda n: pl.BlockSpec((tm, n), lambda i: (i, 0))
    bw = y_ret.shape[1]
    return pl.pallas_call(
        functools.partial(_merge_kernel, alpha=alpha, n_x=len(x_args), nb=nb, cb=cb),
        out_shape=jax.ShapeDtypeStruct((rows, D_MODEL), F32),
        grid=(rows // tm,),
        in_specs=x_specs + [
            pl.BlockSpec((None, 6, D_MODEL), mod_map),
            row_spec(bw), row_spec(bw), row_spec(bw), row_spec(MERGE_W),
            _const_spec((bw, D_MODEL)), _const_spec((bw, D_MODEL)), _const_spec((bw, D_MODEL)),
            _const_spec((D_MODEL, D_MODEL)),
            _const_spec((1, D_MODEL)), _const_spec((1, D_MODEL)),
        ],
        out_specs=row_spec(D_MODEL),
        compiler_params=_params("arbitrary"),
        name="merge",
    )(*x_args, mod_tab, y_ret, y_att, y_m, mg, w_r, w_a, w_m, w_o, ln_g, ln_b)


def _ffn_kernel(x_ref, xp_ref, xn_ref, mod_ref, wu_ref, cw_ref, cb_ref, wd_ref, lng_ref, lnb_ref,
                o_ref, act_scr, *, alpha, nb, cb):
    tm = x_ref.shape[0]
    j = pl.program_id(0) % nb
    prev_ok = jnp.logical_and(j != 0, j != cb)
    next_ok = jnp.logical_and(j != cb - 1, j != nb - 1)
    shift = mod_ref[3:4, :]
    scale = 1.0 + mod_ref[4:5, :]

    def mod(xv):
        return _layer_norm(xv) * scale + shift

    x = x_ref[...]
    hp = jnp.where(prev_ok, mod(xp_ref[...]), 0.0)
    hn = jnp.where(next_ok, mod(xn_ref[...]), 0.0)
    hext = jnp.concatenate([hp, mod(x), hn], axis=0).astype(BF16)

    def conv(u, lo, hi):
        w = cw_ref[:, lo:hi]
        rows = u.shape[0]
        prev = pltpu.roll(u, 1, 0)[SUBLANES:SUBLANES + tm]
        nxt = pltpu.roll(u, rows - 1, 0)[SUBLANES:SUBLANES + tm]
        return (prev * w[0:1] + u[SUBLANES:SUBLANES + tm] * w[1:2] + nxt * w[2:3]
                + cb_ref[:, lo:hi])

    def up(c):
        lo, hi = c * FF_CHUNK, (c + 1) * FF_CHUNK
        return _dot(hext, wu_ref[:, lo:hi]), _dot(hext, wu_ref[:, D_FF + lo:D_FF + hi])

    n_chunks = D_FF // FF_CHUNK
    split = (n_chunks + 1) // 2
    acc = None
    ua, ug = up(0)
    for c in range(n_chunks):
        nxt = up(c + 1) if c + 1 < n_chunks else None
        lo, hi = c * FF_CHUNK, (c + 1) * FF_CHUNK
        a = conv(ua, lo, hi)
        g = conv(ug, D_FF + lo, D_FF + hi)
        act_scr[:, lo:hi] = (g * _sigmoid(g) * a).astype(BF16)
        if c + 1 == split:
            acc = _dot(act_scr[:, 0:hi], wd_ref[0:hi, :])
        if nxt is not None:
            ua, ug = nxt
    acc = acc + _dot(act_scr[:, split * FF_CHUNK:D_FF], wd_ref[split * FF_CHUNK:D_FF, :])
    y = alpha * x + mod_ref[5:6, :] * acc
    o_ref[...] = _layer_norm(y) * lng_ref[...] + lnb_ref[...]


def _ffn(xs, mod_tab, w_up, conv_w, conv_b, w_down, ln_g, ln_b, t_len, ctx_len, alpha, latent_only):
    rows = xs.shape[0]
    tm = ROW_TILE
    nb, cb, mod_map, _ = _row_maps(t_len, ctx_len, tm)
    per = tm // SUBLANES
    last = rows // SUBLANES - 1
    if latent_only:
        out_rows = rows // nb * (nb - cb)
        out_map = lambda i: ((i // nb) * (nb - cb) + jnp.maximum(i % nb - cb, 0), 0)
    else:
        out_rows = rows
        out_map = lambda i: (i, 0)
    return pl.pallas_call(
        functools.partial(_ffn_kernel, alpha=alpha, nb=nb, cb=cb),
        out_shape=jax.ShapeDtypeStruct((out_rows, D_MODEL), F32),
        grid=(rows // tm,),
        in_specs=[
            pl.BlockSpec((tm, D_MODEL), lambda i: (i, 0)),
            pl.BlockSpec((SUBLANES, D_MODEL), lambda i: (jnp.maximum(i * per - 1, 0), 0)),
            pl.BlockSpec((SUBLANES, D_MODEL), lambda i: (jnp.minimum((i + 1) * per, last), 0)),
            pl.BlockSpec((None, 6, D_MODEL), mod_map),
            _const_spec((D_MODEL, 2 * D_FF)),
            _const_spec((3, 2 * D_FF)),
            _const_spec((1, 2 * D_FF)),
            _const_spec((D_FF, D_MODEL)),
            _const_spec((1, D_MODEL)), _const_spec((1, D_MODEL)),
        ],
        out_specs=pl.BlockSpec((tm, D_MODEL), out_map),
        scratch_shapes=[pltpu.VMEM((tm, D_FF), BF16)],
        compiler_params=_params("arbitrary"),
        name="conv_ffn",
    )(xs, xs, xs, mod_tab, w_up, conv_w, conv_b, w_down, ln_g, ln_b)


def _rope_tables(seq, ctx_len):
    f32 = np.float32
    rows = seq // GRID_W
    row = np.repeat(np.arange(rows), GRID_W).astype(f32)
    col = np.tile(np.arange(GRID_W), rows).astype(f32)
    n_freq = HD_ATT // 4
    freqs = np.power(f32(ROPE_THETA), -np.arange(n_freq, dtype=f32) / f32(n_freq)).astype(f32)
    ar = row[:, None] * freqs[None, :]
    ac = col[:, None] * freqs[None, :]
    cos = np.concatenate([np.cos(ar), np.cos(ar), np.cos(ac), np.cos(ac)], axis=-1)
    sin = np.concatenate([-np.sin(ar), np.sin(ar), -np.sin(ac), np.sin(ac)], axis=-1)
    cos = np.concatenate([np.ones((ctx_len, HD_ATT), f32), cos], axis=0)
    sin = np.concatenate([np.zeros((ctx_len, HD_ATT), f32), sin], axis=0)
    return jnp.asarray(np.tile(cos, (1, 2)), F32), jnp.asarray(np.tile(sin, (1, 2)), F32)


def kernel(x, c, ctx, c_ctx, w_mod, b_mod, w_in, b_in, ret_decay_logit, ret_gn_g, attn_qn_g, attn_kn_g,
           mlstm_gn_g, w_br_ret, w_br_att, w_br_mlstm, w_out, ln1_g, ln1_b, w_up, conv_w, conv_b, w_down,
           ln2_g, ln2_b):
    batch, seq, _ = x.shape
    ctx_len = ctx.shape[1]
    depth = w_mod.shape[0]
    t_len = ctx_len + seq
    alpha = (2.0 * depth) ** 0.25
    assert ctx_len == ROW_TILE and seq % (ATT_SLABS * ROW_TILE) == 0
    assert seq % GRID_W == 0 and batch + 1 <= SUBLANES

    xs = (ctx.reshape(batch * ctx_len, D_MODEL), x.reshape(batch * seq, D_MODEL))

    cc = jnp.zeros((SUBLANES, D_MODEL), F32).at[:batch].set(c).at[batch].set(c_ctx)
    mod = _modulation(cc, w_mod, b_mod)
    mod_lat = mod[:, :batch].reshape(depth, batch, 1, 6, D_MODEL)
    mod_ctx = jnp.broadcast_to(mod[:, batch].reshape(depth, 1, 1, 6, D_MODEL),
                               (depth, batch, 1, 6, D_MODEL))
    mod_tab = jnp.concatenate([mod_ctx, mod_lat], axis=2).reshape(depth, batch * 2, 6, D_MODEL)

    cos_t, sin_t = _rope_tables(seq, ctx_len)
    n_gate = 4 * H_M
    split = OFF_GATE + n_gate

    def pad_proj(a, dtype=F32):
        pad = jnp.zeros(a.shape[:-1] + (GATE_W - n_gate,), dtype)
        return jnp.concatenate([a[..., :split].astype(dtype), pad, a[..., split:].astype(dtype)],
                               axis=-1)

    b_in_p = pad_proj(b_in).reshape(depth, 1, N_PROJ)
    dl = jnp.broadcast_to(ret_decay_logit.reshape(depth, 2 * H_RET, 1), (depth, 2 * H_RET, LANES))
    qg = jnp.tile(attn_qn_g, (1, 2)).reshape(depth, 1, LANES)
    kg = jnp.tile(attn_kn_g, (1, 2)).reshape(depth, 1, LANES)
    row2 = lambda a, l: a[l].reshape(1, -1)
    cast = lambda w, l: w[l].astype(BF16)

    for l in range(depth):
        ret_p, att_q, att_k, att_vt, mls_p, gates, mg = _proj_in(
            xs, mod_tab[l], pad_proj(w_in[l], BF16), b_in_p[l], cos_t, sin_t, qg[l], kg[l],
            t_len, ctx_len)
        y_ret = _retention(ret_p, dl[l], row2(ret_gn_g, l), batch, t_len, ctx_len)
        bound = (1.01 * LOG2_E * HD_ATT ** 0.5 * jnp.max(jnp.abs(attn_qn_g[l]))
                 * jnp.max(jnp.abs(attn_kn_g[l]))).reshape(1)
        attend = lambda fixed: functools.partial(
            _attention, batch=batch, t_len=t_len, ctx_len=ctx_len, bounded=fixed)
        y_att = lax.cond(bound[0] <= ATT_SAFE_BOUND, attend(True), attend(False),
                         att_q, att_k, att_vt, bound)
        y_m = _mlstm(mls_p, gates, row2(mlstm_gn_g, l), batch, t_len, ctx_len)
        xs = _merge(xs, mod_tab[l], y_ret, y_att, y_m, mg, cast(w_br_ret, l), cast(w_br_att, l),
                    cast(w_br_mlstm, l), cast(w_out, l), row2(ln1_g, l), row2(ln1_b, l),
                    t_len, ctx_len, alpha)
        xs = _ffn(xs, mod_tab[l], cast(w_up, l), conv_w[l], row2(conv_b, l), cast(w_down, l),
                  row2(ln2_g, l), row2(ln2_b, l), t_len, ctx_len, alpha,
                  latent_only=l == depth - 1)

    return xs.reshape(batch, seq, D_MODEL)
```

```python
import functools

import jax
import jax.numpy as jnp
import numpy as np
from jax import lax
from jax.experimental import pallas as pl
from jax.experimental.pallas import tpu as pltpu

F32 = jnp.float32
BF16 = jnp.bfloat16

D_MODEL = 1024
H_RET, DK_RET, DV_RET = 4, 64, 128
H_ATT, KV_ATT, HD_ATT = 8, 2, 64
H_M, DK_M, DV_M = 4, 64, 128
CHUNK = 128
D_FF = 2816
GRID_W = 64
ROPE_THETA = 10000.0
NORM_EPS = 1e-6
LOG2_E = 1.4426950408889634

LANES = 128
SUBLANES = 8
ROW_TILE = 256
Q_TILE = 256
Q_SUB = 128
ATT_SAFE_BOUND = 40.0
ATT_SLABS = 2
ATT_ONES_ROWS = 16
FF_CHUNK = 256
FF_DOWN_CHUNKS = 6
MOD_COLS = 1536
VMEM_LIMIT = 56 * 1024 * 1024

RET_W = 2 * H_RET * DK_RET + 2 * H_RET * DV_RET
ATT_W = (H_ATT + 2 * KV_ATT) * HD_ATT
MLS_W = 2 * H_M * DK_M + 2 * H_M * DV_M
GATE_W = LANES
MERGE_W = 3 * D_MODEL
OFF_ATT = RET_W
OFF_MLS = OFF_ATT + ATT_W
OFF_GATE = OFF_MLS + MLS_W
OFF_MERGE = OFF_GATE + GATE_W
N_PROJ = OFF_MERGE + MERGE_W


def _dot(a, b):
    return jnp.dot(a, b, preferred_element_type=F32)


def _dot_nt(a, b):
    return lax.dot_general(a, b, (((1,), (1,)), ((), ())), preferred_element_type=F32)


def _dot_tn(a, b):
    return lax.dot_general(a, b, (((0,), (0,)), ((), ())), preferred_element_type=F32)


def _layer_norm(x):
    mu = jnp.mean(x, axis=-1, keepdims=True)
    xc = x - mu
    var = jnp.mean(xc * xc, axis=-1, keepdims=True)
    return xc * lax.rsqrt(var + NORM_EPS)


def _log_sigmoid(x):
    return jnp.minimum(x, 0.0) - jnp.log1p(jnp.exp(-jnp.abs(x)))


def _sigmoid(x):
    return 1.0 / (1.0 + jnp.exp(-x))


def _split3(x):
    hi = x.astype(BF16)
    r1 = x - hi.astype(F32)
    mid = r1.astype(BF16)
    lo = (r1 - mid.astype(F32)).astype(BF16)
    return hi, mid, lo


def _const_spec(shape):
    zeros = (0,) * len(shape)
    return pl.BlockSpec(shape, lambda *_: zeros, pipeline_mode=pl.Buffered(1))


def _params(*sem):
    return pltpu.CompilerParams(dimension_semantics=sem, vmem_limit_bytes=VMEM_LIMIT)


def _mod_kernel(c_ref, w_ref, b_ref, o_ref):
    c = c_ref[...]
    s = (c * _sigmoid(c)).astype(BF16)
    o_ref[0] = _dot(s, w_ref[0].astype(BF16)) + b_ref[0]


def _modulation(cc, w_mod, b_mod):
    depth, _, n = w_mod.shape
    rows = cc.shape[0]
    return pl.pallas_call(
        _mod_kernel,
        out_shape=jax.ShapeDtypeStruct((depth, rows, n), F32),
        grid=(depth, n // MOD_COLS),
        in_specs=[
            pl.BlockSpec((rows, D_MODEL), lambda l, j: (0, 0)),
            pl.BlockSpec((1, D_MODEL, MOD_COLS), lambda l, j: (l, 0, j)),
            pl.BlockSpec((1, 1, MOD_COLS), lambda l, j: (l, 0, j)),
        ],
        out_specs=pl.BlockSpec((1, rows, MOD_COLS), lambda l, j: (l, 0, j)),
        compiler_params=_params("arbitrary", "arbitrary"),
        name="modulation",
    )(cc, w_mod, b_mod.reshape(depth, 1, n))


def _head_sumsq(x, ones_bd):
    sq = x * x
    hi = sq.astype(BF16)
    lo = (sq - hi.astype(F32)).astype(BF16)
    return _dot(hi, ones_bd) + _dot(lo, ones_bd)


def _norm_rope(x, gain, cos, sin, ones_bd, lane, scale):
    y = x * lax.rsqrt(_head_sumsq(x, ones_bd) * (1.0 / HD_ATT) + NORM_EPS) * gain
    above = pltpu.roll(y, LANES - 16, 1)
    below = pltpu.roll(y, 16, 1)
    partner = jnp.where((lane & 16) == 0, above, below)
    return (y * cos + partner * sin) * scale


def _proj_in_kernel(*refs, n_x, nb, cb):
    x_refs = refs[:n_x]
    (mod_ref, w_ref, b_ref, cos_ref, sin_ref, qg_ref, kg_ref,
     ret_ref, attq_ref, attk_ref, attvt_ref, mls_ref, gate_ref, merge_ref) = refs[n_x:]
    tm = x_refs[0].shape[0]
    x = _read_rows(x_refs, nb, cb)
    h = (_layer_norm(x) * (1.0 + mod_ref[1:2, :]) + mod_ref[0:1, :]).astype(BF16)

    def seg(lo, hi):
        return _dot(h, w_ref[:, lo:hi]) + b_ref[:, lo:hi]

    def store_qkvx(dst_ref, r, gate_fn):
        dst_ref[:, 0:256] = r[:, 0:256].astype(BF16)
        dst_ref[:, 256:512] = (r[:, 256:512] * 0.125).astype(BF16)
        dst_ref[:, 512:1024] = r[:, 512:1024].astype(BF16)
        dst_ref[:, 1024:1536] = gate_fn(r[:, 1024:1536]).astype(BF16)

    half = MERGE_W // 2
    a = seg(OFF_ATT, OFF_ATT + ATT_W)
    merge_ref[:, 0:half] = _sigmoid(seg(OFF_MERGE, OFF_MERGE + half)).astype(BF16)

    lane = lax.broadcasted_iota(jnp.int32, (tm, LANES), 1)
    ri = lax.broadcasted_iota(jnp.int32, (LANES, LANES), 0)
    ci = lax.broadcasted_iota(jnp.int32, (LANES, LANES), 1)
    ones_bd = jnp.where((ri // HD_ATT) == (ci // HD_ATT), 1.0, 0.0).astype(BF16)
    cos = cos_ref[...]
    sin = sin_ref[...]
    qg = qg_ref[...]
    for j in range(H_ATT * HD_ATT // LANES):
        qj = _norm_rope(a[:, j * LANES:(j + 1) * LANES], qg, cos, sin, ones_bd, lane,
                        HD_ATT ** -0.5 * LOG2_E)
        attq_ref[:, j * LANES:(j + 1) * LANES] = qj.astype(BF16)
    kn = _norm_rope(a[:, 512:640], kg_ref[...], cos, sin, ones_bd, lane, 1.0)
    kr = pltpu.roll(kn, HD_ATT, 1)
    low = lane < HD_ATT
    attk_ref[:, 0:128] = jnp.where(low, kn, kr).astype(BF16)
    attk_ref[:, 128:256] = jnp.where(low, kr, kn).astype(BF16)
    attvt_ref[0] = jnp.transpose(a[:, 640:768]).astype(BF16)

    merge_ref[:, half:MERGE_W] = _sigmoid(seg(OFF_MERGE + half, OFF_MERGE + MERGE_W)).astype(BF16)
    store_qkvx(ret_ref, seg(0, RET_W), lambda g: g * _sigmoid(g))
    store_qkvx(mls_ref, seg(OFF_MLS, OFF_MLS + MLS_W), _sigmoid)
    gate_ref[...] = seg(OFF_GATE, OFF_GATE + GATE_W)


def _row_maps(t_len, ctx_len, tm):
    nb = t_len // tm
    cb = ctx_len // tm

    def mod_map(i):
        return ((i // nb) * 2 + ((i % nb) >= cb).astype(jnp.int32), 0, 0)

    def pos_map(i):
        return (i % nb, 0)

    return nb, cb, mod_map, pos_map


def _row_inputs(xs, t_len, ctx_len, tm):
    nb, cb = t_len // tm, ctx_len // tm
    if not isinstance(xs, tuple):
        return [xs], [pl.BlockSpec((tm, D_MODEL), lambda i: (i, 0))], xs.shape[0]
    ctx_map = lambda i: ((i // nb) * cb + jnp.minimum(i % nb, cb - 1), 0)
    lat_map = lambda i: ((i // nb) * (nb - cb) + jnp.maximum(i % nb - cb, 0), 0)
    specs = [pl.BlockSpec((tm, D_MODEL), ctx_map), pl.BlockSpec((tm, D_MODEL), lat_map)]
    return list(xs), specs, xs[0].shape[0] + xs[1].shape[0]


def _read_rows(x_refs, nb, cb):
    if len(x_refs) == 1:
        return x_refs[0][...]
    return jnp.where(pl.program_id(0) % nb < cb, x_refs[0][...], x_refs[1][...])


def _proj_in(xs, mod_tab, w, b, cos_t, sin_t, qg, kg, t_len, ctx_len):
    tm = ROW_TILE
    nb, cb, mod_map, pos_map = _row_maps(t_len, ctx_len, tm)
    x_args, x_specs, rows = _row_inputs(xs, t_len, ctx_len, tm)
    row_spec = lambda n: pl.BlockSpec((tm, n), lambda i: (i, 0))
    out_shapes = [
        jax.ShapeDtypeStruct((rows, RET_W), BF16),
        jax.ShapeDtypeStruct((rows, H_ATT * HD_ATT), BF16),
        jax.ShapeDtypeStruct((rows, 2 * LANES), BF16),
        jax.ShapeDtypeStruct((rows // tm, KV_ATT * HD_ATT, tm), BF16),
        jax.ShapeDtypeStruct((rows, MLS_W), BF16),
        jax.ShapeDtypeStruct((rows, GATE_W), F32),
        jax.ShapeDtypeStruct((rows, MERGE_W), BF16),
    ]

    def out_spec(s):
        if len(s.shape) == 3:
            return pl.BlockSpec((1,) + s.shape[1:], lambda i: (i, 0, 0))
        return row_spec(s.shape[1])

    return pl.pallas_call(
        functools.partial(_proj_in_kernel, n_x=len(x_args), nb=nb, cb=cb),
        out_shape=out_shapes,
        grid=(rows // tm,),
        in_specs=x_specs + [
            pl.BlockSpec((None, 6, D_MODEL), mod_map),
            _const_spec((D_MODEL, N_PROJ)),
            _const_spec((1, N_PROJ)),
            pl.BlockSpec((tm, LANES), pos_map),
            pl.BlockSpec((tm, LANES), pos_map),
            _const_spec((1, LANES)),
            _const_spec((1, LANES)),
        ],
        out_specs=[out_spec(s) for s in out_shapes],
        compiler_params=_params("arbitrary"),
        name="proj_in",
    )(*x_args, mod_tab, w, b, cos_t, sin_t, qg, kg)


def _scan_chunk(d, j, n_ctx, n_chunks):
    if d == 0:
        return j
    return jnp.where(j < n_ctx, n_ctx - 1 - j, n_chunks + n_ctx - 1 - j)


def _run_scan_steps(body, n_ctx, n_chunks):
    for lo, hi in ((0, n_ctx), (n_ctx, n_chunks)):
        mid = lo + (hi - lo) // 2
        for start, stop, readout in ((lo, mid, False), (mid, hi, True)):
            if stop > start:
                lax.fori_loop(start, stop, functools.partial(body, readout=readout), 0)


def _head_norm(o):
    mu = jnp.mean(o, axis=-1, keepdims=True)
    oc = o - mu
    var = jnp.mean(oc * oc, axis=-1, keepdims=True)
    return oc * lax.rsqrt(var + NORM_EPS)


def _ret_kernel(p_ref, dl_ref, gn_ref, y_ref, o_scr, s_scr, intra_scr, qd_scr, kd_scr, cd_scr,
                *, n_ctx, n_chunks):
    lg = _log_sigmoid(dl_ref[...])
    cd_scr[...] = jnp.exp(float(CHUNK) * lg)
    ri = lax.broadcasted_iota(jnp.int32, (CHUNK, CHUNK), 0).astype(F32)
    ci = lax.broadcasted_iota(jnp.int32, (CHUNK, CHUNK), 1).astype(F32)
    for d in range(2):
        diff = ri - ci if d == 0 else ci - ri
        qpos = ri + 1.0 if d == 0 else float(CHUNK) - ri
        kpos = float(CHUNK - 1) - ri if d == 0 else ri
        keep = diff >= 0.0
        for h in range(H_RET):
            r = d * H_RET + h
            lgb = lg[r:r + 1, :]
            intra_scr[r] = jnp.where(keep, jnp.exp(jnp.where(keep, diff, 0.0) * lgb), 0.0)
            qd_scr[r] = jnp.exp(qpos * lgb)
            kd_scr[r] = jnp.exp(kpos * lgb)

    lane = lax.broadcasted_iota(jnp.int32, (CHUNK, LANES), 1)
    half_mask = (lane < DK_RET, lane >= DK_RET)

    s_scr[...] = jnp.zeros_like(s_scr)
    o_scr[...] = jnp.zeros_like(o_scr)

    def body(j, carry, readout):
        staged = []
        for d in range(2):
            c = _scan_chunk(d, j, n_ctx, n_chunks)
            rows = pl.ds(pl.multiple_of(c * CHUNK, CHUNK), CHUNK)
            for pair in range(H_RET // 2):
                qp = p_ref[rows, pair * LANES:(pair + 1) * LANES]
                kp = p_ref[rows, 256 + pair * LANES:256 + (pair + 1) * LANES]
                kpf = kp.astype(F32)
                for hh in range(2):
                    h = pair * 2 + hh
                    r = d * H_RET + h
                    qm = jnp.where(half_mask[hh], qp, jnp.zeros_like(qp))
                    v = p_ref[rows, 512 + h * DV_RET:512 + (h + 1) * DV_RET]
                    s = (_dot_nt(qm, kp) * intra_scr[r]).astype(BF16)
                    state = s_scr[r]
                    inter = _dot(qm, state.astype(BF16)) * qd_scr[r]
                    kdec = (kpf * kd_scr[r]).astype(BF16)
                    s_scr[r] = state * cd_scr[r:r + 1, :] + _dot_tn(kdec, v)
                    staged.append((rows, h, s, v, inter))
        for rows, h, s, v, inter in staged:
            cols = slice(h * DV_RET, (h + 1) * DV_RET)
            tot = o_scr[rows, cols] + (_dot(s, v) + inter)
            o_scr[rows, cols] = tot
            if readout:
                g = p_ref[rows, 1024 + h * DV_RET:1024 + (h + 1) * DV_RET].astype(F32)
                y = _head_norm(tot) * gn_ref[:, cols] * g
                y_ref[rows, cols] = y.astype(BF16)
        return carry

    _run_scan_steps(body, n_ctx, n_chunks)


def _retention(ret_p, dl, gn, batch, t_len, ctx_len):
    n_chunks = t_len // CHUNK
    kern = functools.partial(_ret_kernel, n_ctx=ctx_len // CHUNK, n_chunks=n_chunks)
    width = H_RET * DV_RET
    return pl.pallas_call(
        kern,
        out_shape=jax.ShapeDtypeStruct((batch * t_len, width), BF16),
        grid=(batch,),
        in_specs=[
            pl.BlockSpec((t_len, RET_W), lambda b: (b, 0)),
            _const_spec((2 * H_RET, LANES)),
            _const_spec((1, width)),
        ],
        out_specs=pl.BlockSpec((t_len, width), lambda b: (b, 0)),
        scratch_shapes=[
            pltpu.VMEM((t_len, width), F32),
            pltpu.VMEM((2 * H_RET, LANES, DV_RET), F32),
            pltpu.VMEM((2 * H_RET, CHUNK, CHUNK), F32),
            pltpu.VMEM((2 * H_RET, CHUNK, CHUNK), F32),
            pltpu.VMEM((2 * H_RET, CHUNK, CHUNK), F32),
            pltpu.VMEM((2 * H_RET, LANES), F32),
        ],
        compiler_params=_params("arbitrary"),
        name="retention",
    )(ret_p, dl, gn)


def _mlstm_kernel(p_ref, gate_ref, gn_ref, y_ref, h_scr, cn_scr, m_scr, *, n_ctx, n_chunks):
    ri = lax.broadcasted_iota(jnp.int32, (CHUNK, CHUNK), 0)
    ci = lax.broadcasted_iota(jnp.int32, (CHUNK, CHUNK), 1)
    lane = lax.broadcasted_iota(jnp.int32, (CHUNK, LANES), 1)
    half_mask = (lane < DK_M, lane >= DK_M)
    ones_v = jnp.ones((CHUNK, LANES), BF16)

    cn_scr[...] = jnp.zeros_like(cn_scr)
    m_scr[...] = jnp.zeros_like(m_scr)
    h_scr[...] = jnp.zeros_like(h_scr)
    keeps = (ri >= ci, ci >= ri)
    tris = tuple(jnp.where(k, 1.0, 0.0).astype(BF16) for k in keeps)
    full = (CHUNK, LANES)

    def body(j, carry, readout):
        staged = []
        for d in range(2):
            keep = keeps[d]
            end_row = CHUNK - 1 if d == 0 else 0
            c = _scan_chunk(d, j, n_ctx, n_chunks)
            rows = pl.ds(pl.multiple_of(c * CHUNK, CHUNK), CHUNK)
            gates = gate_ref[rows, :]
            hi, mid, lo = _split3(_log_sigmoid(gates))
            bcum = _dot(tris[d], hi) + _dot(tris[d], mid) + _dot(tris[d], lo)
            cdiff = gates - pltpu.roll(bcum, LANES - 2 * H_M, 1)
            cdiff_t = jnp.transpose(cdiff)
            for pair in range(H_M // 2):
                qp = p_ref[rows, pair * LANES:(pair + 1) * LANES]
                kp = p_ref[rows, 256 + pair * LANES:256 + (pair + 1) * LANES]
                kpf = kp.astype(F32)
                for hh in range(2):
                    h = pair * 2 + hh
                    r = d * H_M + h
                    gb = 2 * H_M + r
                    b_l = jnp.broadcast_to(bcum[:, gb:gb + 1], full)
                    c_l = jnp.broadcast_to(cdiff[:, r:r + 1], full)
                    c_row = cdiff_t[r:r + 1, :]
                    b_end = b_l[end_row:end_row + 1, :]
                    m_prev = m_scr[r][0:1, :]
                    qm = jnp.where(half_mask[hh], qp, jnp.zeros_like(qp))
                    v = p_ref[rows, 512 + h * DV_M:512 + (h + 1) * DV_M]
                    vext = jnp.concatenate([v, ones_v], axis=1)

                    dmat = jnp.where(keep, b_l + c_row, -jnp.inf)
                    inter = b_l + m_prev
                    row_max = jnp.broadcast_to(jnp.max(dmat, axis=-1, keepdims=True), full)
                    m_t = jnp.maximum(inter, row_max)
                    s = (_dot_nt(qm, kp) * jnp.exp(dmat - m_t)).astype(BF16)
                    a_int = jnp.exp(inter - m_t)
                    state = cn_scr[r]
                    qcn = _dot(qm, state.astype(BF16))

                    g_l = b_end + c_l
                    m_new = jnp.maximum(b_end + m_prev, jnp.max(g_l, axis=0, keepdims=True))
                    kdec = (kpf * jnp.exp(g_l - m_new)).astype(BF16)
                    carry_scale = jnp.exp(b_end + m_prev - m_new)
                    cn_scr[r] = (jnp.concatenate([carry_scale, carry_scale], axis=1) * state
                                 + _dot_tn(kdec, vext))
                    m_scr[r] = jnp.broadcast_to(m_new, (SUBLANES, LANES))
                    staged.append((rows, h, s, vext, a_int, qcn, m_t))
        for rows, h, s, vext, a_int, qcn, m_t in staged:
            cols = slice(h * DV_M, (h + 1) * DV_M)
            sv = _dot(s, vext)
            num = sv[:, 0:DV_M] + a_int * qcn[:, 0:DV_M]
            den = sv[:, DV_M:2 * DV_M] + a_int * qcn[:, DV_M:2 * DV_M]
            hval = num / jnp.maximum(jnp.abs(den), jnp.exp(-m_t))
            tot = h_scr[rows, cols] + hval
            h_scr[rows, cols] = tot
            if readout:
                og = p_ref[rows, 1024 + h * DV_M:1024 + (h + 1) * DV_M].astype(F32)
                y = og * (_head_norm(tot) * gn_ref[:, cols])
                y_ref[rows, cols] = y.astype(BF16)
        return carry

    _run_scan_steps(body, n_ctx, n_chunks)


def _mlstm(mls_p, gates, gn, batch, t_len, ctx_len):
    n_chunks = t_len // CHUNK
    kern = functools.partial(_mlstm_kernel, n_ctx=ctx_len // CHUNK, n_chunks=n_chunks)
    width = H_M * DV_M
    return pl.pallas_call(
        kern,
        out_shape=jax.ShapeDtypeStruct((batch * t_len, width), BF16),
        grid=(batch,),
        in_specs=[
            pl.BlockSpec((t_len, MLS_W), lambda b: (b, 0)),
            pl.BlockSpec((t_len, GATE_W), lambda b: (b, 0)),
            _const_spec((1, width)),
        ],
        out_specs=pl.BlockSpec((t_len, width), lambda b: (b, 0)),
        scratch_shapes=[
            pltpu.VMEM((t_len, width), F32),
            pltpu.VMEM((2 * H_M, LANES, 2 * LANES), F32),
            pltpu.VMEM((2 * H_M, SUBLANES, LANES), F32),
        ],
        compiler_params=_params("arbitrary"),
        name="mlstm",
    )(mls_p, gates, gn)


def _attn_kernel(bound_ref, q_ref, k_ref, vt_ref, o_ref, m_scr, alpha_scr, acc_scr, p_scr,
                 *, ctx_len, t_len, bounded):
    tq = q_ref.shape[0]
    tk = vt_ref.shape[2]
    rep = H_ATT // KV_ATT
    is_ctx = pl.program_id(1) < ctx_len // tq
    lane = lax.broadcasted_iota(jnp.int32, (Q_SUB, LANES), 1)
    low = lane < HD_ATT
    bound = bound_ref[0]

    chains = [(g, qb) for g in range(KV_ATT) for qb in range(tq // Q_SUB)]
    qs = []
    for g, qb in chains:
        parts = []
        for hh in range(rep):
            h = g * rep + hh
            qp = q_ref[qb * Q_SUB:(qb + 1) * Q_SUB, (h // 2) * LANES:(h // 2 + 1) * LANES]
            parts.append(jnp.where(low if h % 2 == 0 else ~low, qp, jnp.zeros_like(qp)))
        qs.append(jnp.concatenate(parts, axis=0))
    if not bounded:
        m_scr[...] = jnp.full_like(m_scr, -jnp.inf)
        alpha_scr[...] = jnp.ones_like(alpha_scr)
    acc_scr[...] = jnp.zeros_like(acc_scr)
    p_scr[:, tk:, :] = jnp.zeros((len(chains), (ATT_SLABS - 1) * tk, rep * Q_SUB), BF16)
    shift = 0.5 * bound

    def v_rows(g, first_slab, n_slabs):
        vt = [vt_ref[first_slab + i][g * HD_ATT:(g + 1) * HD_ATT, :] for i in range(n_slabs)]
        vt = vt[0] if n_slabs == 1 else jnp.concatenate(vt, axis=1)
        return jnp.concatenate([vt, jnp.ones((ATT_ONES_ROWS, n_slabs * tk), BF16)], axis=0)

    def probabilities(c, s_t):
        if bounded:
            return jnp.exp2((s_t - shift).astype(BF16))
        m_old = m_scr[c, 0:1, :]
        m_new = jnp.maximum(m_old, jnp.max(s_t, axis=0, keepdims=True))
        alpha_scr[c, 0:1, :] = jnp.exp2(m_old - m_new)
        m_scr[c, 0:1, :] = m_new
        return jnp.exp2(s_t - m_new).astype(BF16)

    def accumulate(c, pv):
        if bounded:
            acc_scr[c] += pv
        else:
            acc_scr[c] = alpha_scr[c, 0:1, :] * acc_scr[c] + pv

    for c, (g, _) in enumerate(chains):
        s_t = _dot_nt(k_ref[0:tk, g * LANES:(g + 1) * LANES], qs[c])
        p_scr[c, 0:tk] = probabilities(c, s_t)

    def pending_slab(step):
        return jnp.maximum(ATT_SLABS * step - 1, 0)

    def body(t, carry):
        keys = pl.ds(pl.multiple_of(tk + t * (ATT_SLABS * tk), tk), ATT_SLABS * tk)
        scores = [_dot_nt(k_ref[keys, g * LANES:(g + 1) * LANES], qs[c])
                  for c, (g, _) in enumerate(chains)]
        for c, (g, _) in enumerate(chains):
            accumulate(c, _dot(v_rows(g, pending_slab(t), ATT_SLABS), p_scr[c]))
        for c, s_t in enumerate(scores):
            p_scr[c] = probabilities(c, s_t)
        return carry

    n_steps = jnp.where(is_ctx, 0, (t_len - tk) // (ATT_SLABS * tk))
    lax.fori_loop(0, n_steps, body, 0)
    for c, (g, _) in enumerate(chains):
        accumulate(c, _dot(v_rows(g, pending_slab(n_steps), ATT_SLABS), p_scr[c]))

    for c, (g, qb) in enumerate(chains):
        acc = acc_scr[c]
        out_t = acc[0:HD_ATT] / acc[HD_ATT:HD_ATT + 1]
        for pp in range(rep // 2):
            pair_t = jnp.concatenate([out_t[:, (2 * pp) * Q_SUB:(2 * pp + 1) * Q_SUB],
                                      out_t[:, (2 * pp + 1) * Q_SUB:(2 * pp + 2) * Q_SUB]], axis=0)
            col = (g * rep // 2 + pp) * LANES
            o_ref[qb * Q_SUB:(qb + 1) * Q_SUB, col:col + LANES] = jnp.transpose(pair_t).astype(BF16)


def _attention(att_q, att_k, att_vt, bound, batch, t_len, ctx_len, bounded):
    tq = Q_TILE
    nq = t_len // tq
    rep = H_ATT // KV_ATT
    slabs, vrows, tk = att_vt.shape
    per_batch = slabs // batch
    n_chains = KV_ATT * (tq // Q_SUB)
    kern = functools.partial(_attn_kernel, ctx_len=ctx_len, t_len=t_len, bounded=bounded)
    width = H_ATT * HD_ATT
    return pl.pallas_call(
        kern,
        out_shape=jax.ShapeDtypeStruct((batch * t_len, width), BF16),
        grid=(batch, nq),
        in_specs=[
            pl.BlockSpec(memory_space=pltpu.SMEM),
            pl.BlockSpec((tq, width), lambda b, i: (b * nq + i, 0)),
            pl.BlockSpec((t_len, 2 * LANES), lambda b, i: (b, 0)),
            pl.BlockSpec((per_batch, vrows, tk), lambda b, i: (b, 0, 0)),
        ],
        out_specs=pl.BlockSpec((tq, width), lambda b, i: (b * nq + i, 0)),
        scratch_shapes=[
            pltpu.VMEM((n_chains, SUBLANES, rep * Q_SUB), F32),
            pltpu.VMEM((n_chains, SUBLANES, rep * Q_SUB), F32),
            pltpu.VMEM((n_chains, HD_ATT + ATT_ONES_ROWS, rep * Q_SUB), F32),
            pltpu.VMEM((n_chains, ATT_SLABS * tk, rep * Q_SUB), BF16),
        ],
        compiler_params=_params("arbitrary", "arbitrary"),
        name="attention_bounded" if bounded else "attention_online",
    )(bound, att_q, att_k, att_vt)


def _merge_kernel(*refs, alpha, n_x, nb, cb):
    x_refs = refs[:n_x]
    (mod_ref, yr_ref, ya_ref, ym_ref, mg_ref, wr_ref, wa_ref, wm_ref, wo_ref,
     lng_ref, lnb_ref, o_ref) = refs[n_x:]

    def branch(y_ref, w_ref, k):
        gate = mg_ref[:, k * D_MODEL:(k + 1) * D_MODEL].astype(F32)
        return gate * _dot(y_ref[...], w_ref[...])

    z = branch(yr_ref, wr_ref, 0) + branch(ya_ref, wa_ref, 1) + branch(ym_ref, wm_ref, 2)
    mix = _dot(z.astype(BF16), wo_ref[...])
    y = alpha * _read_rows(x_refs, nb, cb) + mod_ref[2:3, :] * mix
    o_ref[...] = _layer_norm(y) * lng_ref[...] + lnb_ref[...]


def _merge(xs, mod_tab, y_ret, y_att, y_m, mg, w_r, w_a, w_m, w_o, ln_g, ln_b, t_len, ctx_len, alpha):
    tm = ROW_TILE
    nb, cb, mod_map, _ = _row_maps(t_len, ctx_len, tm)
    x_args, x_specs, rows = _row_inputs(xs, t_len, ctx_len, tm)
    row_spec = lambda n: pl.BlockSpec((tm, n), lambda i: (i, 0))
    bw = y_ret.shape[1]
    return pl.pallas_call(
        functools.partial(_merge_kernel, alpha=alpha, n_x=len(x_args), nb=nb, cb=cb),
        out_shape=jax.ShapeDtypeStruct((rows, D_MODEL), F32),
        grid=(rows // tm,),
        in_specs=x_specs + [
            pl.BlockSpec((None, 6, D_MODEL), mod_map),
            row_spec(bw), row_spec(bw), row_spec(bw), row_spec(MERGE_W),
            _const_spec((bw, D_MODEL)), _const_spec((bw, D_MODEL)), _const_spec((bw, D_MODEL)),
            _const_spec((D_MODEL, D_MODEL)),
            _const_spec((1, D_MODEL)), _const_spec((1, D_MODEL)),
        ],
        out_specs=row_spec(D_MODEL),
        compiler_params=_params("arbitrary"),
        name="merge",
    )(*x_args, mod_tab, y_ret, y_att, y_m, mg, w_r, w_a, w_m, w_o, ln_g, ln_b)


def _ffn_kernel(x_ref, xp_ref, xn_ref, mod_ref, wu_ref, cw_ref, cb_ref, wd_ref, lng_ref, lnb_ref,
                o_ref, act_scr, *, alpha, nb, cb):
    tm = x_ref.shape[0]
    j = pl.program_id(0) % nb
    prev_ok = jnp.logical_and(j != 0, j != cb)
    next_ok = jnp.logical_and(j != cb - 1, j != nb - 1)
    shift = mod_ref[3:4, :]
    scale = 1.0 + mod_ref[4:5, :]

    def mod(xv):
        return _layer_norm(xv) * scale + shift

    x = x_ref[...]
    hp = jnp.where(prev_ok, mod(xp_ref[...]), 0.0)
    hn = jnp.where(next_ok, mod(xn_ref[...]), 0.0)
    hext = jnp.concatenate([hp, mod(x), hn], axis=0).astype(BF16)

    def conv(u, lo, hi):
        w = cw_ref[:, lo:hi]
        rows = u.shape[0]
        prev = pltpu.roll(u, 1, 0)[SUBLANES:SUBLANES + tm]
        nxt = pltpu.roll(u, rows - 1, 0)[SUBLANES:SUBLANES + tm]
        return (prev * w[0:1] + u[SUBLANES:SUBLANES + tm] * w[1:2] + nxt * w[2:3]
                + cb_ref[:, lo:hi])

    def up(c):
        lo, hi = c * FF_CHUNK, (c + 1) * FF_CHUNK
        return _dot(hext, wu_ref[:, lo:hi]), _dot(hext, wu_ref[:, D_FF + lo:D_FF + hi])

    n_chunks = D_FF // FF_CHUNK
    acc = None
    done = 0
    ua, ug = up(0)
    for c in range(n_chunks):
        nxt = up(c + 1) if c + 1 < n_chunks else None
        lo, hi = c * FF_CHUNK, (c + 1) * FF_CHUNK
        a = conv(ua, lo, hi)
        g = conv(ug, D_FF + lo, D_FF + hi)
        act_scr[:, lo:hi] = (g * _sigmoid(g) * a).astype(BF16)
        if (c + 1) % FF_DOWN_CHUNKS == 0 or c + 1 == n_chunks:
            part = _dot(act_scr[:, done:hi], wd_ref[done:hi, :])
            acc = part if acc is None else acc + part
            done = hi
        if nxt is not None:
            ua, ug = nxt
    y = alpha * x + mod_ref[5:6, :] * acc
    o_ref[...] = _layer_norm(y) * lng_ref[...] + lnb_ref[...]


def _ffn(xs, mod_tab, w_up, conv_w, conv_b, w_down, ln_g, ln_b, t_len, ctx_len, alpha, latent_only):
    rows = xs.shape[0]
    tm = ROW_TILE
    nb, cb, mod_map, _ = _row_maps(t_len, ctx_len, tm)
    per = tm // SUBLANES
    last = rows // SUBLANES - 1
    if latent_only:
        out_rows = rows // nb * (nb - cb)
        out_map = lambda i: ((i // nb) * (nb - cb) + jnp.maximum(i % nb - cb, 0), 0)
    else:
        out_rows = rows
        out_map = lambda i: (i, 0)
    return pl.pallas_call(
        functools.partial(_ffn_kernel, alpha=alpha, nb=nb, cb=cb),
        out_shape=jax.ShapeDtypeStruct((out_rows, D_MODEL), F32),
        grid=(rows // tm,),
        in_specs=[
            pl.BlockSpec((tm, D_MODEL), lambda i: (i, 0)),
            pl.BlockSpec((SUBLANES, D_MODEL), lambda i: (jnp.maximum(i * per - 1, 0), 0)),
            pl.BlockSpec((SUBLANES, D_MODEL), lambda i: (jnp.minimum((i + 1) * per, last), 0)),
            pl.BlockSpec((None, 6, D_MODEL), mod_map),
            _const_spec((D_MODEL, 2 * D_FF)),
            _const_spec((3, 2 * D_FF)),
            _const_spec((1, 2 * D_FF)),
            _const_spec((D_FF, D_MODEL)),
            _const_spec((1, D_MODEL)), _const_spec((1, D_MODEL)),
        ],
        out_specs=pl.BlockSpec((tm, D_MODEL), out_map),
        scratch_shapes=[pltpu.VMEM((tm, D_FF), BF16)],
        compiler_params=_params("arbitrary"),
        name="conv_ffn",
    )(xs, xs, xs, mod_tab, w_up, conv_w, conv_b, w_down, ln_g, ln_b)


def _rope_tables(seq, ctx_len):
    f32 = np.float32
    rows = seq // GRID_W
    row = np.repeat(np.arange(rows), GRID_W).astype(f32)
    col = np.tile(np.arange(GRID_W), rows).astype(f32)
    n_freq = HD_ATT // 4
    freqs = np.power(f32(ROPE_THETA), -np.arange(n_freq, dtype=f32) / f32(n_freq)).astype(f32)
    ar = row[:, None] * freqs[None, :]
    ac = col[:, None] * freqs[None, :]
    cos = np.concatenate([np.cos(ar), np.cos(ar), np.cos(ac), np.cos(ac)], axis=-1)
    sin = np.concatenate([-np.sin(ar), np.sin(ar), -np.sin(ac), np.sin(ac)], axis=-1)
    cos = np.concatenate([np.ones((ctx_len, HD_ATT), f32), cos], axis=0)
    sin = np.concatenate([np.zeros((ctx_len, HD_ATT), f32), sin], axis=0)
    return jnp.asarray(np.tile(cos, (1, 2)), F32), jnp.asarray(np.tile(sin, (1, 2)), F32)


def kernel(x, c, ctx, c_ctx, w_mod, b_mod, w_in, b_in, ret_decay_logit, ret_gn_g, attn_qn_g, attn_kn_g,
           mlstm_gn_g, w_br_ret, w_br_att, w_br_mlstm, w_out, ln1_g, ln1_b, w_up, conv_w, conv_b, w_down,
           ln2_g, ln2_b):
    batch, seq, _ = x.shape
    ctx_len = ctx.shape[1]
    depth = w_mod.shape[0]
    t_len = ctx_len + seq
    alpha = (2.0 * depth) ** 0.25
    assert ctx_len == ROW_TILE and seq % (ATT_SLABS * ROW_TILE) == 0
    assert seq % GRID_W == 0 and batch + 1 <= SUBLANES

    xs = (ctx.reshape(batch * ctx_len, D_MODEL), x.reshape(batch * seq, D_MODEL))

    cc = jnp.zeros((SUBLANES, D_MODEL), F32).at[:batch].set(c).at[batch].set(c_ctx)
    mod = _modulation(cc, w_mod, b_mod)
    mod_lat = mod[:, :batch].reshape(depth, batch, 1, 6, D_MODEL)
    mod_ctx = jnp.broadcast_to(mod[:, batch].reshape(depth, 1, 1, 6, D_MODEL),
                               (depth, batch, 1, 6, D_MODEL))
    mod_tab = jnp.concatenate([mod_ctx, mod_lat], axis=2).reshape(depth, batch * 2, 6, D_MODEL)

    cos_t, sin_t = _rope_tables(seq, ctx_len)
    n_gate = 4 * H_M
    split = OFF_GATE + n_gate

    def pad_proj(a, dtype=F32):
        pad = jnp.zeros(a.shape[:-1] + (GATE_W - n_gate,), dtype)
        return jnp.concatenate([a[..., :split].astype(dtype), pad, a[..., split:].astype(dtype)],
                               axis=-1)

    b_in_p = pad_proj(b_in).reshape(depth, 1, N_PROJ)
    dl = jnp.broadcast_to(ret_decay_logit.reshape(depth, 2 * H_RET, 1), (depth, 2 * H_RET, LANES))
    qg = jnp.tile(attn_qn_g, (1, 2)).reshape(depth, 1, LANES)
    kg = jnp.tile(attn_kn_g, (1, 2)).reshape(depth, 1, LANES)
    row2 = lambda a, l: a[l].reshape(1, -1)
    cast = lambda w, l: w[l].astype(BF16)

    for l in range(depth):
        ret_p, att_q, att_k, att_vt, mls_p, gates, mg = _proj_in(
            xs, mod_tab[l], pad_proj(w_in[l], BF16), b_in_p[l], cos_t, sin_t, qg[l], kg[l],
            t_len, ctx_len)
        y_ret = _retention(ret_p, dl[l], row2(ret_gn_g, l), batch, t_len, ctx_len)
        bound = (1.01 * LOG2_E * HD_ATT ** 0.5 * jnp.max(jnp.abs(attn_qn_g[l]))
                 * jnp.max(jnp.abs(attn_kn_g[l]))).reshape(1)
        attend = lambda fixed: functools.partial(
            _attention, batch=batch, t_len=t_len, ctx_len=ctx_len, bounded=fixed)
        y_att = lax.cond(bound[0] <= ATT_SAFE_BOUND, attend(True), attend(False),
                         att_q, att_k, att_vt, bound)
        y_m = _mlstm(mls_p, gates, row2(mlstm_gn_g, l), batch, t_len, ctx_len)
        xs = _merge(xs, mod_tab[l], y_ret, y_att, y_m, mg, cast(w_br_ret, l), cast(w_br_att, l),
                    cast(w_br_mlstm, l), cast(w_out, l), row2(ln1_g, l), row2(ln1_b, l),
                    t_len, ctx_len, alpha)
        xs = _ffn(xs, mod_tab[l], cast(w_up, l), conv_w[l], row2(conv_b, l), cast(w_down, l),
                  row2(ln2_g, l), row2(ln2_b, l), t_len, ctx_len, alpha,
                  latent_only=l == depth - 1)

    return xs.reshape(batch, seq, D_MODEL)
```

```python
import functools

import jax
import jax.numpy as jnp
import numpy as np
from jax import lax
from jax.experimental import pallas as pl
from jax.experimental.pallas import tpu as pltpu

F32 = jnp.float32
BF16 = jnp.bfloat16

D_MODEL = 1024
H_RET, DK_RET, DV_RET = 4, 64, 128
H_ATT, KV_ATT, HD_ATT = 8, 2, 64
H_M, DK_M, DV_M = 4, 64, 128
CHUNK = 128
D_FF = 2816
GRID_W = 64
ROPE_THETA = 10000.0
NORM_EPS = 1e-6
LOG2_E = 1.4426950408889634

LANES = 128
SUBLANES = 8
ROW_TILE = 256
Q_TILE = 256
Q_SUB = 128
ATT_SAFE_BOUND = 40.0
ATT_SLABS = 2
ATT_ONES_ROWS = 16
FF_CHUNK = 256
FF_DOWN_CHUNKS = 6
MOD_COLS = 1536
VMEM_LIMIT = 56 * 1024 * 1024

RET_W = 2 * H_RET * DK_RET + 2 * H_RET * DV_RET
ATT_W = (H_ATT + 2 * KV_ATT) * HD_ATT
MLS_W = 2 * H_M * DK_M + 2 * H_M * DV_M
GATE_W = LANES
MERGE_W = 3 * D_MODEL
OFF_ATT = RET_W
OFF_MLS = OFF_ATT + ATT_W
OFF_GATE = OFF_MLS + MLS_W
OFF_MERGE = OFF_GATE + GATE_W
N_PROJ = OFF_MERGE + MERGE_W


def _dot(a, b):
    return jnp.dot(a, b, preferred_element_type=F32)


def _dot_nt(a, b):
    return lax.dot_general(a, b, (((1,), (1,)), ((), ())), preferred_element_type=F32)


def _dot_tn(a, b):
    return lax.dot_general(a, b, (((0,), (0,)), ((), ())), preferred_element_type=F32)


def _layer_norm(x):
    mu = jnp.mean(x, axis=-1, keepdims=True)
    xc = x - mu
    var = jnp.mean(xc * xc, axis=-1, keepdims=True)
    return xc * lax.rsqrt(var + NORM_EPS)


def _log_sigmoid(x):
    return jnp.minimum(x, 0.0) - jnp.log1p(jnp.exp(-jnp.abs(x)))


def _sigmoid(x):
    return 1.0 / (1.0 + jnp.exp(-x))


def _split3(x):
    hi = x.astype(BF16)
    r1 = x - hi.astype(F32)
    mid = r1.astype(BF16)
    lo = (r1 - mid.astype(F32)).astype(BF16)
    return hi, mid, lo


def _const_spec(shape):
    zeros = (0,) * len(shape)
    return pl.BlockSpec(shape, lambda *_: zeros, pipeline_mode=pl.Buffered(1))


def _params(*sem):
    return pltpu.CompilerParams(dimension_semantics=sem, vmem_limit_bytes=VMEM_LIMIT)


def _mod_kernel(c_ref, w_ref, b_ref, o_ref):
    c = c_ref[...]
    s = (c * _sigmoid(c)).astype(BF16)
    o_ref[0] = _dot(s, w_ref[0].astype(BF16)) + b_ref[0]


def _modulation(cc, w_mod, b_mod):
    depth, _, n = w_mod.shape
    rows = cc.shape[0]
    return pl.pallas_call(
        _mod_kernel,
        out_shape=jax.ShapeDtypeStruct((depth, rows, n), F32),
        grid=(depth, n // MOD_COLS),
        in_specs=[
            pl.BlockSpec((rows, D_MODEL), lambda l, j: (0, 0)),
            pl.BlockSpec((1, D_MODEL, MOD_COLS), lambda l, j: (l, 0, j)),
            pl.BlockSpec((1, 1, MOD_COLS), lambda l, j: (l, 0, j)),
        ],
        out_specs=pl.BlockSpec((1, rows, MOD_COLS), lambda l, j: (l, 0, j)),
        compiler_params=_params("arbitrary", "arbitrary"),
        name="modulation",
    )(cc, w_mod, b_mod.reshape(depth, 1, n))


def _head_sumsq(x, ones_bd):
    sq = x * x
    hi = sq.astype(BF16)
    lo = (sq - hi.astype(F32)).astype(BF16)
    return _dot(hi, ones_bd) + _dot(lo, ones_bd)


def _norm_rope(x, gain, cos, sin, ones_bd, lane, scale):
    y = x * lax.rsqrt(_head_sumsq(x, ones_bd) * (1.0 / HD_ATT) + NORM_EPS) * gain
    above = pltpu.roll(y, LANES - 16, 1)
    below = pltpu.roll(y, 16, 1)
    partner = jnp.where((lane & 16) == 0, above, below)
    return (y * cos + partner * sin) * scale


def _proj_in_kernel(*refs, n_x, nb, cb):
    x_refs = refs[:n_x]
    (mod_ref, wa_ref, wg_ref, wb_ref, b_ref, cos_ref, sin_ref, qg_ref, kg_ref,
     ret_ref, attq_ref, attk_ref, attvt_ref, mls_ref, gate_ref, merge_ref) = refs[n_x:]
    tm = x_refs[0].shape[0]
    x = _read_rows(x_refs, nb, cb)
    h = (_layer_norm(x) * (1.0 + mod_ref[1:2, :]) + mod_ref[0:1, :]).astype(BF16)

    def seg(lo, hi):
        if hi <= OFF_GATE:
            w = wa_ref[:, lo:hi]
        elif lo >= OFF_MERGE:
            w = wb_ref[:, lo - OFF_MERGE:hi - OFF_MERGE]
        else:
            w = wg_ref[...]
        return _dot(h, w) + b_ref[:, lo:hi]

    def store_qkvx(dst_ref, r, gate_fn):
        dst_ref[:, 0:256] = r[:, 0:256].astype(BF16)
        dst_ref[:, 256:512] = (r[:, 256:512] * 0.125).astype(BF16)
        dst_ref[:, 512:1024] = r[:, 512:1024].astype(BF16)
        dst_ref[:, 1024:1536] = gate_fn(r[:, 1024:1536]).astype(BF16)

    half = MERGE_W // 2
    a = seg(OFF_ATT, OFF_ATT + ATT_W)
    merge_ref[:, 0:half] = _sigmoid(seg(OFF_MERGE, OFF_MERGE + half)).astype(BF16)

    lane = lax.broadcasted_iota(jnp.int32, (tm, LANES), 1)
    ri = lax.broadcasted_iota(jnp.int32, (LANES, LANES), 0)
    ci = lax.broadcasted_iota(jnp.int32, (LANES, LANES), 1)
    ones_bd = jnp.where((ri // HD_ATT) == (ci // HD_ATT), 1.0, 0.0).astype(BF16)
    cos = cos_ref[...]
    sin = sin_ref[...]
    qg = qg_ref[...]
    for j in range(H_ATT * HD_ATT // LANES):
        qj = _norm_rope(a[:, j * LANES:(j + 1) * LANES], qg, cos, sin, ones_bd, lane,
                        HD_ATT ** -0.5 * LOG2_E)
        attq_ref[:, j * LANES:(j + 1) * LANES] = qj.astype(BF16)
    kn = _norm_rope(a[:, 512:640], kg_ref[...], cos, sin, ones_bd, lane, 1.0)
    kr = pltpu.roll(kn, HD_ATT, 1)
    low = lane < HD_ATT
    attk_ref[:, 0:128] = jnp.where(low, kn, kr).astype(BF16)
    attk_ref[:, 128:256] = jnp.where(low, kr, kn).astype(BF16)
    attvt_ref[0] = jnp.transpose(a[:, 640:768]).astype(BF16)

    merge_ref[:, half:MERGE_W] = _sigmoid(seg(OFF_MERGE + half, OFF_MERGE + MERGE_W)).astype(BF16)
    store_qkvx(ret_ref, seg(0, RET_W), lambda g: g * _sigmoid(g))
    store_qkvx(mls_ref, seg(OFF_MLS, OFF_MLS + MLS_W), _sigmoid)
    gate_ref[...] = seg(OFF_GATE, OFF_GATE + GATE_W)


def _row_maps(t_len, ctx_len, tm):
    nb = t_len // tm
    cb = ctx_len // tm

    def mod_map(i):
        return ((i // nb) * 2 + ((i % nb) >= cb).astype(jnp.int32), 0, 0)

    def pos_map(i):
        return (i % nb, 0)

    return nb, cb, mod_map, pos_map


def _row_inputs(xs, t_len, ctx_len, tm):
    nb, cb = t_len // tm, ctx_len // tm
    if not isinstance(xs, tuple):
        return [xs], [pl.BlockSpec((tm, D_MODEL), lambda i: (i, 0))], xs.shape[0]
    ctx_map = lambda i: ((i // nb) * cb + jnp.minimum(i % nb, cb - 1), 0)
    lat_map = lambda i: ((i // nb) * (nb - cb) + jnp.maximum(i % nb - cb, 0), 0)
    specs = [pl.BlockSpec((tm, D_MODEL), ctx_map), pl.BlockSpec((tm, D_MODEL), lat_map)]
    return list(xs), specs, xs[0].shape[0] + xs[1].shape[0]


def _read_rows(x_refs, nb, cb):
    if len(x_refs) == 1:
        return x_refs[0][...]
    return jnp.where(pl.program_id(0) % nb < cb, x_refs[0][...], x_refs[1][...])


def _proj_in(xs, mod_tab, w, b, cos_t, sin_t, qg, kg, t_len, ctx_len):
    tm = ROW_TILE
    nb, cb, mod_map, pos_map = _row_maps(t_len, ctx_len, tm)
    x_args, x_specs, rows = _row_inputs(xs, t_len, ctx_len, tm)
    row_spec = lambda n: pl.BlockSpec((tm, n), lambda i: (i, 0))
    out_shapes = [
        jax.ShapeDtypeStruct((rows, RET_W), BF16),
        jax.ShapeDtypeStruct((rows, H_ATT * HD_ATT), BF16),
        jax.ShapeDtypeStruct((rows, 2 * LANES), BF16),
        jax.ShapeDtypeStruct((rows // tm, KV_ATT * HD_ATT, tm), BF16),
        jax.ShapeDtypeStruct((rows, MLS_W), BF16),
        jax.ShapeDtypeStruct((rows, GATE_W), F32),
        jax.ShapeDtypeStruct((rows, MERGE_W), BF16),
    ]

    def out_spec(s):
        if len(s.shape) == 3:
            return pl.BlockSpec((1,) + s.shape[1:], lambda i: (i, 0, 0))
        return row_spec(s.shape[1])

    return pl.pallas_call(
        functools.partial(_proj_in_kernel, n_x=len(x_args), nb=nb, cb=cb),
        out_shape=out_shapes,
        grid=(rows // tm,),
        in_specs=x_specs + [
            pl.BlockSpec((None, 6, D_MODEL), mod_map),
            _const_spec((D_MODEL, OFF_GATE)),
            _const_spec((D_MODEL, GATE_W)),
            _const_spec((D_MODEL, MERGE_W)),
            _const_spec((1, N_PROJ)),
            pl.BlockSpec((tm, LANES), pos_map),
            pl.BlockSpec((tm, LANES), pos_map),
            _const_spec((1, LANES)),
            _const_spec((1, LANES)),
        ],
        out_specs=[out_spec(s) for s in out_shapes],
        compiler_params=_params("arbitrary"),
        name="proj_in",
    )(*x_args, mod_tab, *w, b, cos_t, sin_t, qg, kg)


def _scan_chunk(d, j, n_ctx, n_chunks):
    if d == 0:
        return j
    return jnp.where(j < n_ctx, n_ctx - 1 - j, n_chunks + n_ctx - 1 - j)


def _run_scan_steps(body, n_ctx, n_chunks):
    for lo, hi in ((0, n_ctx), (n_ctx, n_chunks)):
        mid = lo + (hi - lo) // 2
        for start, stop, readout in ((lo, mid, False), (mid, hi, True)):
            if stop > start:
                lax.fori_loop(start, stop, functools.partial(body, readout=readout), 0)


def _head_norm(o):
    mu = jnp.mean(o, axis=-1, keepdims=True)
    oc = o - mu
    var = jnp.mean(oc * oc, axis=-1, keepdims=True)
    return oc * lax.rsqrt(var + NORM_EPS)


def _ret_kernel(p_ref, dl_ref, gn_ref, y_ref, o_scr, s_scr, intra_scr, qd_scr, kd_scr, cd_scr,
                *, n_ctx, n_chunks):
    lg = _log_sigmoid(dl_ref[...])
    cd_scr[...] = jnp.exp(float(CHUNK) * lg)
    ri = lax.broadcasted_iota(jnp.int32, (CHUNK, CHUNK), 0).astype(F32)
    ci = lax.broadcasted_iota(jnp.int32, (CHUNK, CHUNK), 1).astype(F32)
    for d in range(2):
        diff = ri - ci if d == 0 else ci - ri
        qpos = ri + 1.0 if d == 0 else float(CHUNK) - ri
        kpos = float(CHUNK - 1) - ri if d == 0 else ri
        keep = diff >= 0.0
        for h in range(H_RET):
            r = d * H_RET + h
            lgb = lg[r:r + 1, :]
            intra_scr[r] = jnp.where(keep, jnp.exp(jnp.where(keep, diff, 0.0) * lgb), 0.0)
            qd_scr[r] = jnp.exp(qpos * lgb)
            kd_scr[r] = jnp.exp(kpos * lgb)

    lane = lax.broadcasted_iota(jnp.int32, (CHUNK, LANES), 1)
    half_mask = (lane < DK_RET, lane >= DK_RET)

    s_scr[...] = jnp.zeros_like(s_scr)
    o_scr[...] = jnp.zeros_like(o_scr)

    def body(j, carry, readout):
        staged = []
        for d in range(2):
            c = _scan_chunk(d, j, n_ctx, n_chunks)
            rows = pl.ds(pl.multiple_of(c * CHUNK, CHUNK), CHUNK)
            for pair in range(H_RET // 2):
                qp = p_ref[rows, pair * LANES:(pair + 1) * LANES]
                kp = p_ref[rows, 256 + pair * LANES:256 + (pair + 1) * LANES]
                kpf = kp.astype(F32)
                for hh in range(2):
                    h = pair * 2 + hh
                    r = d * H_RET + h
                    qm = jnp.where(half_mask[hh], qp, jnp.zeros_like(qp))
                    v = p_ref[rows, 512 + h * DV_RET:512 + (h + 1) * DV_RET]
                    s = (_dot_nt(qm, kp) * intra_scr[r]).astype(BF16)
                    state = s_scr[r]
                    inter = _dot(qm, state.astype(BF16)) * qd_scr[r]
                    kdec = (kpf * kd_scr[r]).astype(BF16)
                    s_scr[r] = state * cd_scr[r:r + 1, :] + _dot_tn(kdec, v)
                    staged.append((rows, h, s, v, inter))
        for rows, h, s, v, inter in staged:
            cols = slice(h * DV_RET, (h + 1) * DV_RET)
            tot = o_scr[rows, cols] + (_dot(s, v) + inter)
            o_scr[rows, cols] = tot
            if readout:
                g = p_ref[rows, 1024 + h * DV_RET:1024 + (h + 1) * DV_RET].astype(F32)
                y = _head_norm(tot) * gn_ref[:, cols] * g
                y_ref[rows, cols] = y.astype(BF16)
        return carry

    _run_scan_steps(body, n_ctx, n_chunks)


def _retention(ret_p, dl, gn, batch, t_len, ctx_len):
    n_chunks = t_len // CHUNK
    kern = functools.partial(_ret_kernel, n_ctx=ctx_len // CHUNK, n_chunks=n_chunks)
    width = H_RET * DV_RET
    return pl.pallas_call(
        kern,
        out_shape=jax.ShapeDtypeStruct((batch * t_len, width), BF16),
        grid=(batch,),
        in_specs=[
            pl.BlockSpec((t_len, RET_W), lambda b: (b, 0)),
            _const_spec((2 * H_RET, LANES)),
            _const_spec((1, width)),
        ],
        out_specs=pl.BlockSpec((t_len, width), lambda b: (b, 0)),
        scratch_shapes=[
            pltpu.VMEM((t_len, width), F32),
            pltpu.VMEM((2 * H_RET, LANES, DV_RET), F32),
            pltpu.VMEM((2 * H_RET, CHUNK, CHUNK), F32),
            pltpu.VMEM((2 * H_RET, CHUNK, CHUNK), F32),
            pltpu.VMEM((2 * H_RET, CHUNK, CHUNK), F32),
            pltpu.VMEM((2 * H_RET, LANES), F32),
        ],
        compiler_params=_params("arbitrary"),
        name="retention",
    )(ret_p, dl, gn)


def _mlstm_kernel(p_ref, gate_ref, gn_ref, y_ref, h_scr, cn_scr, m_scr, *, n_ctx, n_chunks):
    ri = lax.broadcasted_iota(jnp.int32, (CHUNK, CHUNK), 0)
    ci = lax.broadcasted_iota(jnp.int32, (CHUNK, CHUNK), 1)
    lane = lax.broadcasted_iota(jnp.int32, (CHUNK, LANES), 1)
    half_mask = (lane < DK_M, lane >= DK_M)
    ones_v = jnp.ones((CHUNK, LANES), BF16)

    cn_scr[...] = jnp.zeros_like(cn_scr)
    m_scr[...] = jnp.zeros_like(m_scr)
    h_scr[...] = jnp.zeros_like(h_scr)
    keeps = (ri >= ci, ci >= ri)
    tris = tuple(jnp.where(k, 1.0, 0.0).astype(BF16) for k in keeps)
    full = (CHUNK, LANES)

    def body(j, carry, readout):
        staged = []
        for d in range(2):
            keep = keeps[d]
            end_row = CHUNK - 1 if d == 0 else 0
            c = _scan_chunk(d, j, n_ctx, n_chunks)
            rows = pl.ds(pl.multiple_of(c * CHUNK, CHUNK), CHUNK)
            gates = gate_ref[rows, :]
            hi, mid, lo = _split3(_log_sigmoid(gates))
            bcum = _dot(tris[d], hi) + _dot(tris[d], mid) + _dot(tris[d], lo)
            cdiff = gates - pltpu.roll(bcum, LANES - 2 * H_M, 1)
            cdiff_t = jnp.transpose(cdiff)
            for pair in range(H_M // 2):
                qp = p_ref[rows, pair * LANES:(pair + 1) * LANES]
                kp = p_ref[rows, 256 + pair * LANES:256 + (pair + 1) * LANES]
                kpf = kp.astype(F32)
                for hh in range(2):
                    h = pair * 2 + hh
                    r = d * H_M + h
                    gb = 2 * H_M + r
                    b_l = jnp.broadcast_to(bcum[:, gb:gb + 1], full)
                    c_l = jnp.broadcast_to(cdiff[:, r:r + 1], full)
                    c_row = cdiff_t[r:r + 1, :]
                    b_end = b_l[end_row:end_row + 1, :]
                    m_prev = m_scr[r][0:1, :]
                    qm = jnp.where(half_mask[hh], qp, jnp.zeros_like(qp))
                    v = p_ref[rows, 512 + h * DV_M:512 + (h + 1) * DV_M]
                    vext = jnp.concatenate([v, ones_v], axis=1)

                    dmat = jnp.where(keep, b_l + c_row, -jnp.inf)
                    inter = b_l + m_prev
                    row_max = jnp.broadcast_to(jnp.max(dmat, axis=-1, keepdims=True), full)
                    m_t = jnp.maximum(inter, row_max)
                    s = (_dot_nt(qm, kp) * jnp.exp(dmat - m_t)).astype(BF16)
                    a_int = jnp.exp(inter - m_t)
                    state = cn_scr[r]
                    qcn = _dot(qm, state.astype(BF16))

                    g_l = b_end + c_l
                    m_new = jnp.maximum(b_end + m_prev, jnp.max(g_l, axis=0, keepdims=True))
                    kdec = (kpf * jnp.exp(g_l - m_new)).astype(BF16)
                    carry_scale = jnp.exp(b_end + m_prev - m_new)
                    cn_scr[r] = (jnp.concatenate([carry_scale, carry_scale], axis=1) * state
                                 + _dot_tn(kdec, vext))
                    m_scr[r] = jnp.broadcast_to(m_new, (SUBLANES, LANES))
                    staged.append((rows, h, s, vext, a_int, qcn, m_t))
        for rows, h, s, vext, a_int, qcn, m_t in staged:
            cols = slice(h * DV_M, (h + 1) * DV_M)
            sv = _dot(s, vext)
            num = sv[:, 0:DV_M] + a_int * qcn[:, 0:DV_M]
            den = sv[:, DV_M:2 * DV_M] + a_int * qcn[:, DV_M:2 * DV_M]
            hval = num / jnp.maximum(jnp.abs(den), jnp.exp(-m_t))
            tot = h_scr[rows, cols] + hval
            h_scr[rows, cols] = tot
            if readout:
                og = p_ref[rows, 1024 + h * DV_M:1024 + (h + 1) * DV_M].astype(F32)
                y = og * (_head_norm(tot) * gn_ref[:, cols])
                y_ref[rows, cols] = y.astype(BF16)
        return carry

    _run_scan_steps(body, n_ctx, n_chunks)


def _mlstm(mls_p, gates, gn, batch, t_len, ctx_len):
    n_chunks = t_len // CHUNK
    kern = functools.partial(_mlstm_kernel, n_ctx=ctx_len // CHUNK, n_chunks=n_chunks)
    width = H_M * DV_M
    return pl.pallas_call(
        kern,
        out_shape=jax.ShapeDtypeStruct((batch * t_len, width), BF16),
        grid=(batch,),
        in_specs=[
            pl.BlockSpec((t_len, MLS_W), lambda b: (b, 0)),
            pl.BlockSpec((t_len, GATE_W), lambda b: (b, 0)),
            _const_spec((1, width)),
        ],
        out_specs=pl.BlockSpec((t_len, width), lambda b: (b, 0)),
        scratch_shapes=[
            pltpu.VMEM((t_len, width), F32),
            pltpu.VMEM((2 * H_M, LANES, 2 * LANES), F32),
            pltpu.VMEM((2 * H_M, SUBLANES, LANES), F32),
        ],
        compiler_params=_params("arbitrary"),
        name="mlstm",
    )(mls_p, gates, gn)


def _attn_kernel(bound_ref, q_ref, k_ref, vt_ref, o_ref, m_scr, alpha_scr, acc_scr, p_scr,
                 *, ctx_len, t_len, bounded):
    tq = q_ref.shape[0]
    tk = vt_ref.shape[2]
    rep = H_ATT // KV_ATT
    is_ctx = pl.program_id(1) < ctx_len // tq
    lane = lax.broadcasted_iota(jnp.int32, (Q_SUB, LANES), 1)
    low = lane < HD_ATT
    bound = bound_ref[0]

    chains = [(g, qb) for g in range(KV_ATT) for qb in range(tq // Q_SUB)]
    qs = []
    for g, qb in chains:
        parts = []
        for hh in range(rep):
            h = g * rep + hh
            qp = q_ref[qb * Q_SUB:(qb + 1) * Q_SUB, (h // 2) * LANES:(h // 2 + 1) * LANES]
            parts.append(jnp.where(low if h % 2 == 0 else ~low, qp, jnp.zeros_like(qp)))
        qs.append(jnp.concatenate(parts, axis=0))
    if not bounded:
        m_scr[...] = jnp.full_like(m_scr, -jnp.inf)
        alpha_scr[...] = jnp.ones_like(alpha_scr)
    acc_scr[...] = jnp.zeros_like(acc_scr)
    p_scr[:, tk:, :] = jnp.zeros((len(chains), (ATT_SLABS - 1) * tk, rep * Q_SUB), BF16)
    shift = 0.5 * bound

    def v_rows(g, first_slab, n_slabs):
        vt = [vt_ref[first_slab + i][g * HD_ATT:(g + 1) * HD_ATT, :] for i in range(n_slabs)]
        vt = vt[0] if n_slabs == 1 else jnp.concatenate(vt, axis=1)
        return jnp.concatenate([vt, jnp.ones((ATT_ONES_ROWS, n_slabs * tk), BF16)], axis=0)

    def probabilities(c, s_t):
        if bounded:
            return jnp.exp2((s_t - shift).astype(BF16))
        m_old = m_scr[c, 0:1, :]
        m_new = jnp.maximum(m_old, jnp.max(s_t, axis=0, keepdims=True))
        alpha_scr[c, 0:1, :] = jnp.exp2(m_old - m_new)
        m_scr[c, 0:1, :] = m_new
        return jnp.exp2(s_t - m_new).astype(BF16)

    def accumulate(c, pv):
        if bounded:
            acc_scr[c] += pv
        else:
            acc_scr[c] = alpha_scr[c, 0:1, :] * acc_scr[c] + pv

    for c, (g, _) in enumerate(chains):
        s_t = _dot_nt(k_ref[0:tk, g * LANES:(g + 1) * LANES], qs[c])
        p_scr[c, 0:tk] = probabilities(c, s_t)

    def pending_slab(step):
        return jnp.maximum(ATT_SLABS * step - 1, 0)

    def body(t, carry):
        keys = pl.ds(pl.multiple_of(tk + t * (ATT_SLABS * tk), tk), ATT_SLABS * tk)
        scores = [_dot_nt(k_ref[keys, g * LANES:(g + 1) * LANES], qs[c])
                  for c, (g, _) in enumerate(chains)]
        for c, (g, _) in enumerate(chains):
            accumulate(c, _dot(v_rows(g, pending_slab(t), ATT_SLABS), p_scr[c]))
        for c, s_t in enumerate(scores):
            p_scr[c] = probabilities(c, s_t)
        return carry

    n_steps = jnp.where(is_ctx, 0, (t_len - tk) // (ATT_SLABS * tk))
    lax.fori_loop(0, n_steps, body, 0)
    for c, (g, _) in enumerate(chains):
        accumulate(c, _dot(v_rows(g, pending_slab(n_steps), ATT_SLABS), p_scr[c]))

    for c, (g, qb) in enumerate(chains):
        acc = acc_scr[c]
        out_t = acc[0:HD_ATT] / acc[HD_ATT:HD_ATT + 1]
        for pp in range(rep // 2):
            pair_t = jnp.concatenate([out_t[:, (2 * pp) * Q_SUB:(2 * pp + 1) * Q_SUB],
                                      out_t[:, (2 * pp + 1) * Q_SUB:(2 * pp + 2) * Q_SUB]], axis=0)
            col = (g * rep // 2 + pp) * LANES
            o_ref[qb * Q_SUB:(qb + 1) * Q_SUB, col:col + LANES] = jnp.transpose(pair_t).astype(BF16)


def _attention(att_q, att_k, att_vt, bound, batch, t_len, ctx_len, bounded):
    tq = Q_TILE
    nq = t_len // tq
    rep = H_ATT // KV_ATT
    slabs, vrows, tk = att_vt.shape
    per_batch = slabs // batch
    n_chains = KV_ATT * (tq // Q_SUB)
    kern = functools.partial(_attn_kernel, ctx_len=ctx_len, t_len=t_len, bounded=bounded)
    width = H_ATT * HD_ATT
    return pl.pallas_call(
        kern,
        out_shape=jax.ShapeDtypeStruct((batch * t_len, width), BF16),
        grid=(batch, nq),
        in_specs=[
            pl.BlockSpec(memory_space=pltpu.SMEM),
            pl.BlockSpec((tq, width), lambda b, i: (b * nq + i, 0)),
            pl.BlockSpec((t_len, 2 * LANES), lambda b, i: (b, 0)),
            pl.BlockSpec((per_batch, vrows, tk), lambda b, i: (b, 0, 0)),
        ],
        out_specs=pl.BlockSpec((tq, width), lambda b, i: (b * nq + i, 0)),
        scratch_shapes=[
            pltpu.VMEM((n_chains, SUBLANES, rep * Q_SUB), F32),
            pltpu.VMEM((n_chains, SUBLANES, rep * Q_SUB), F32),
            pltpu.VMEM((n_chains, HD_ATT + ATT_ONES_ROWS, rep * Q_SUB), F32),
            pltpu.VMEM((n_chains, ATT_SLABS * tk, rep * Q_SUB), BF16),
        ],
        compiler_params=_params("arbitrary", "arbitrary"),
        name="attention_bounded" if bounded else "attention_online",
    )(bound, att_q, att_k, att_vt)


def _merge_kernel(*refs, alpha, n_x, nb, cb):
    x_refs = refs[:n_x]
    (mod_ref, yr_ref, ya_ref, ym_ref, mg_ref, wr_ref, wa_ref, wm_ref, wo_ref,
     lng_ref, lnb_ref, o_ref) = refs[n_x:]

    def branch(y_ref, w_ref, k):
        gate = mg_ref[:, k * D_MODEL:(k + 1) * D_MODEL].astype(F32)
        return gate * _dot(y_ref[...], w_ref[...])

    z = branch(yr_ref, wr_ref, 0) + branch(ya_ref, wa_ref, 1) + branch(ym_ref, wm_ref, 2)
    mix = _dot(z.astype(BF16), wo_ref[...])
    y = alpha * _read_rows(x_refs, nb, cb) + mod_ref[2:3, :] * mix
    o_ref[...] = _layer_norm(y) * lng_ref[...] + lnb_ref[...]


def _merge(xs, mod_tab, y_ret, y_att, y_m, mg, w_r, w_a, w_m, w_o, ln_g, ln_b, t_len, ctx_len, alpha):
    tm = ROW_TILE
    nb, cb, mod_map, _ = _row_maps(t_len, ctx_len, tm)
    x_args, x_specs, rows = _row_inputs(xs, t_len, ctx_len, tm)
    row_spec = lambda n: pl.BlockSpec((tm, n), lambda i: (i, 0))
    bw = y_ret.shape[1]
    return pl.pallas_call(
        functools.partial(_merge_kernel, alpha=alpha, n_x=len(x_args), nb=nb, cb=cb),
        out_shape=jax.ShapeDtypeStruct((rows, D_MODEL), F32),
        grid=(rows // tm,),
        in_specs=x_specs + [
            pl.BlockSpec((None, 6, D_MODEL), mod_map),
            row_spec(bw), row_spec(bw), row_spec(bw), row_spec(MERGE_W),
            _const_spec((bw, D_MODEL)), _const_spec((bw, D_MODEL)), _const_spec((bw, D_MODEL)),
            _const_spec((D_MODEL, D_MODEL)),
            _const_spec((1, D_MODEL)), _const_spec((1, D_MODEL)),
        ],
        out_specs=row_spec(D_MODEL),
        compiler_params=_params("arbitrary"),
        name="merge",
    )(*x_args, mod_tab, y_ret, y_att, y_m, mg, w_r, w_a, w_m, w_o, ln_g, ln_b)


def _ffn_kernel(x_ref, xp_ref, xn_ref, mod_ref, wu_ref, cw_ref, cb_ref, wd_ref, lng_ref, lnb_ref,
                o_ref, act_scr, *, alpha, nb, cb):
    tm = x_ref.shape[0]
    j = pl.program_id(0) % nb
    prev_ok = jnp.logical_and(j != 0, j != cb)
    next_ok = jnp.logical_and(j != cb - 1, j != nb - 1)
    shift = mod_ref[3:4, :]
    scale = 1.0 + mod_ref[4:5, :]

    def mod(xv):
        return _layer_norm(xv) * scale + shift

    x = x_ref[...]
    hp = jnp.where(prev_ok, mod(xp_ref[...]), 0.0)
    hn = jnp.where(next_ok, mod(xn_ref[...]), 0.0)
    hext = jnp.concatenate([hp, mod(x), hn], axis=0).astype(BF16)

    def conv(u, lo, hi):
        w = cw_ref[:, lo:hi]
        rows = u.shape[0]
        prev = pltpu.roll(u, 1, 0)[SUBLANES:SUBLANES + tm]
        nxt = pltpu.roll(u, rows - 1, 0)[SUBLANES:SUBLANES + tm]
        return (prev * w[0:1] + u[SUBLANES:SUBLANES + tm] * w[1:2] + nxt * w[2:3]
                + cb_ref[:, lo:hi])

    def up(c):
        lo, hi = c * FF_CHUNK, (c + 1) * FF_CHUNK
        return _dot(hext, wu_ref[:, lo:hi]), _dot(hext, wu_ref[:, D_FF + lo:D_FF + hi])

    n_chunks = D_FF // FF_CHUNK
    acc = None
    done = 0
    ua, ug = up(0)
    for c in range(n_chunks):
        nxt = up(c + 1) if c + 1 < n_chunks else None
        lo, hi = c * FF_CHUNK, (c + 1) * FF_CHUNK
        a = conv(ua, lo, hi)
        g = conv(ug, D_FF + lo, D_FF + hi)
        act_scr[:, lo:hi] = (g * _sigmoid(g) * a).astype(BF16)
        if (c + 1) % FF_DOWN_CHUNKS == 0 or c + 1 == n_chunks:
            part = _dot(act_scr[:, done:hi], wd_ref[done:hi, :])
            acc = part if acc is None else acc + part
            done = hi
        if nxt is not None:
            ua, ug = nxt
    y = alpha * x + mod_ref[5:6, :] * acc
    o_ref[...] = _layer_norm(y) * lng_ref[...] + lnb_ref[...]


def _ffn(xs, mod_tab, w_up, conv_w, conv_b, w_down, ln_g, ln_b, t_len, ctx_len, alpha, latent_only):
    rows = xs.shape[0]
    tm = ROW_TILE
    nb, cb, mod_map, _ = _row_maps(t_len, ctx_len, tm)
    per = tm // SUBLANES
    last = rows // SUBLANES - 1
    if latent_only:
        out_rows = rows // nb * (nb - cb)
        out_map = lambda i: ((i // nb) * (nb - cb) + jnp.maximum(i % nb - cb, 0), 0)
    else:
        out_rows = rows
        out_map = lambda i: (i, 0)
    return pl.pallas_call(
        functools.partial(_ffn_kernel, alpha=alpha, nb=nb, cb=cb),
        out_shape=jax.ShapeDtypeStruct((out_rows, D_MODEL), F32),
        grid=(rows // tm,),
        in_specs=[
            pl.BlockSpec((tm, D_MODEL), lambda i: (i, 0)),
            pl.BlockSpec((SUBLANES, D_MODEL), lambda i: (jnp.maximum(i * per - 1, 0), 0)),
            pl.BlockSpec((SUBLANES, D_MODEL), lambda i: (jnp.minimum((i + 1) * per, last), 0)),
            pl.BlockSpec((None, 6, D_MODEL), mod_map),
            _const_spec((D_MODEL, 2 * D_FF)),
            _const_spec((3, 2 * D_FF)),
            _const_spec((1, 2 * D_FF)),
            _const_spec((D_FF, D_MODEL)),
            _const_spec((1, D_MODEL)), _const_spec((1, D_MODEL)),
        ],
        out_specs=pl.BlockSpec((tm, D_MODEL), out_map),
        scratch_shapes=[pltpu.VMEM((tm, D_FF), BF16)],
        compiler_params=_params("arbitrary"),
        name="conv_ffn",
    )(xs, xs, xs, mod_tab, w_up, conv_w, conv_b, w_down, ln_g, ln_b)


def _rope_tables(seq, ctx_len):
    f32 = np.float32
    rows = seq // GRID_W
    row = np.repeat(np.arange(rows), GRID_W).astype(f32)
    col = np.tile(np.arange(GRID_W), rows).astype(f32)
    n_freq = HD_ATT // 4
    freqs = np.power(f32(ROPE_THETA), -np.arange(n_freq, dtype=f32) / f32(n_freq)).astype(f32)
    ar = row[:, None] * freqs[None, :]
    ac = col[:, None] * freqs[None, :]
    cos = np.concatenate([np.cos(ar), np.cos(ar), np.cos(ac), np.cos(ac)], axis=-1)
    sin = np.concatenate([-np.sin(ar), np.sin(ar), -np.sin(ac), np.sin(ac)], axis=-1)
    cos = np.concatenate([np.ones((ctx_len, HD_ATT), f32), cos], axis=0)
    sin = np.concatenate([np.zeros((ctx_len, HD_ATT), f32), sin], axis=0)
    return jnp.asarray(np.tile(cos, (1, 2)), F32), jnp.asarray(np.tile(sin, (1, 2)), F32)


def kernel(x, c, ctx, c_ctx, w_mod, b_mod, w_in, b_in, ret_decay_logit, ret_gn_g, attn_qn_g, attn_kn_g,
           mlstm_gn_g, w_br_ret, w_br_att, w_br_mlstm, w_out, ln1_g, ln1_b, w_up, conv_w, conv_b, w_down,
           ln2_g, ln2_b):
    batch, seq, _ = x.shape
    ctx_len = ctx.shape[1]
    depth = w_mod.shape[0]
    t_len = ctx_len + seq
    alpha = (2.0 * depth) ** 0.25
    assert ctx_len == ROW_TILE and seq % (ATT_SLABS * ROW_TILE) == 0
    assert seq % GRID_W == 0 and batch + 1 <= SUBLANES

    xs = (ctx.reshape(batch * ctx_len, D_MODEL), x.reshape(batch * seq, D_MODEL))

    cc = jnp.zeros((SUBLANES, D_MODEL), F32).at[:batch].set(c).at[batch].set(c_ctx)
    mod = _modulation(cc, w_mod, b_mod)
    mod_lat = mod[:, :batch].reshape(depth, batch, 1, 6, D_MODEL)
    mod_ctx = jnp.broadcast_to(mod[:, batch].reshape(depth, 1, 1, 6, D_MODEL),
                               (depth, batch, 1, 6, D_MODEL))
    mod_tab = jnp.concatenate([mod_ctx, mod_lat], axis=2).reshape(depth, batch * 2, 6, D_MODEL)

    cos_t, sin_t = _rope_tables(seq, ctx_len)
    n_gate = 4 * H_M
    split = OFF_GATE + n_gate

    def pad_proj(a):
        pad = jnp.zeros(a.shape[:-1] + (GATE_W - n_gate,), a.dtype)
        return jnp.concatenate([a[..., :split], pad, a[..., split:]], axis=-1)

    def proj_weights(w):
        gate = jnp.pad(w[:, OFF_GATE:split], ((0, 0), (0, GATE_W - n_gate)))
        return w[:, :OFF_GATE].astype(BF16), gate.astype(BF16), w[:, split:].astype(BF16)

    b_in_p = pad_proj(b_in).reshape(depth, 1, N_PROJ)
    dl = jnp.broadcast_to(ret_decay_logit.reshape(depth, 2 * H_RET, 1), (depth, 2 * H_RET, LANES))
    qg = jnp.tile(attn_qn_g, (1, 2)).reshape(depth, 1, LANES)
    kg = jnp.tile(attn_kn_g, (1, 2)).reshape(depth, 1, LANES)
    row2 = lambda a, l: a[l].reshape(1, -1)
    cast = lambda w, l: w[l].astype(BF16)

    for l in range(depth):
        ret_p, att_q, att_k, att_vt, mls_p, gates, mg = _proj_in(
            xs, mod_tab[l], proj_weights(w_in[l]), b_in_p[l], cos_t, sin_t, qg[l], kg[l],
            t_len, ctx_len)
        y_ret = _retention(ret_p, dl[l], row2(ret_gn_g, l), batch, t_len, ctx_len)
        bound = (1.01 * LOG2_E * HD_ATT ** 0.5 * jnp.max(jnp.abs(attn_qn_g[l]))
                 * jnp.max(jnp.abs(attn_kn_g[l]))).reshape(1)
        attend = lambda fixed: functools.partial(
            _attention, batch=batch, t_len=t_len, ctx_len=ctx_len, bounded=fixed)
        y_att = lax.cond(bound[0] <= ATT_SAFE_BOUND, attend(True), attend(False),
                         att_q, att_k, att_vt, bound)
        y_m = _mlstm(mls_p, gates, row2(mlstm_gn_g, l), batch, t_len, ctx_len)
        xs = _merge(xs, mod_tab[l], y_ret, y_att, y_m, mg, cast(w_br_ret, l), cast(w_br_att, l),
                    cast(w_br_mlstm, l), cast(w_out, l), row2(ln1_g, l), row2(ln1_b, l),
                    t_len, ctx_len, alpha)
        xs = _ffn(xs, mod_tab[l], cast(w_up, l), conv_w[l], row2(conv_b, l), cast(w_down, l),
                  row2(ln2_g, l), row2(ln2_b, l), t_len, ctx_len, alpha,
                  latent_only=l == depth - 1)

    return xs.reshape(batch, seq, D_MODEL)
```

```python
import functools

import jax
import jax.numpy as jnp
import numpy as np
from jax import lax
from jax.experimental import pallas as pl
from jax.experimental.pallas import tpu as pltpu

F32 = jnp.float32
BF16 = jnp.bfloat16

D_MODEL = 1024
H_RET, DK_RET, DV_RET = 4, 64, 128
H_ATT, KV_ATT, HD_ATT = 8, 2, 64
H_M, DK_M, DV_M = 4, 64, 128
CHUNK = 128
D_FF = 2816
GRID_W = 64
ROPE_THETA = 10000.0
NORM_EPS = 1e-6
LOG2_E = 1.4426950408889634

LANES = 128
SUBLANES = 8
ROW_TILE = 256
Q_TILE = 256
Q_SUB = 128
ATT_SAFE_BOUND = 40.0
ATT_SLABS = 4
ATT_ONES_ROWS = 16
FF_CHUNK = 256
FF_DOWN_CHUNKS = 6
MOD_COLS = 3072
VMEM_LIMIT = 56 * 1024 * 1024

RET_W = 2 * H_RET * DK_RET + 2 * H_RET * DV_RET
ATT_W = (H_ATT + 2 * KV_ATT) * HD_ATT
MLS_W = 2 * H_M * DK_M + 2 * H_M * DV_M
GATE_W = LANES
MERGE_W = 3 * D_MODEL
OFF_ATT = RET_W
OFF_MLS = OFF_ATT + ATT_W
OFF_GATE = OFF_MLS + MLS_W
OFF_MERGE = OFF_GATE + GATE_W
N_PROJ = OFF_MERGE + MERGE_W


def _dot(a, b):
    return jnp.dot(a, b, preferred_element_type=F32)


def _dot_nt(a, b):
    return lax.dot_general(a, b, (((1,), (1,)), ((), ())), preferred_element_type=F32)


def _dot_tn(a, b):
    return lax.dot_general(a, b, (((0,), (0,)), ((), ())), preferred_element_type=F32)


def _layer_norm(x):
    mu = jnp.mean(x, axis=-1, keepdims=True)
    xc = x - mu
    var = jnp.mean(xc * xc, axis=-1, keepdims=True)
    return xc * lax.rsqrt(var + NORM_EPS)


def _log_sigmoid(x):
    return jnp.minimum(x, 0.0) - jnp.log1p(jnp.exp(-jnp.abs(x)))


def _sigmoid(x):
    return 1.0 / (1.0 + jnp.exp(-x))


def _split3(x):
    hi = x.astype(BF16)
    r1 = x - hi.astype(F32)
    mid = r1.astype(BF16)
    lo = (r1 - mid.astype(F32)).astype(BF16)
    return hi, mid, lo


def _const_spec(shape):
    zeros = (0,) * len(shape)
    return pl.BlockSpec(shape, lambda *_: zeros, pipeline_mode=pl.Buffered(1))


def _params(*sem):
    return pltpu.CompilerParams(dimension_semantics=sem, vmem_limit_bytes=VMEM_LIMIT)


def _mod_kernel(c_ref, w_ref, b_ref, o_ref):
    c = c_ref[...]
    s = (c * _sigmoid(c)).astype(BF16)
    o_ref[0] = _dot(s, w_ref[0].astype(BF16)) + b_ref[0]


def _modulation(cc, w_mod, b_mod):
    depth, _, n = w_mod.shape
    rows = cc.shape[0]
    return pl.pallas_call(
        _mod_kernel,
        out_shape=jax.ShapeDtypeStruct((depth, rows, n), F32),
        grid=(depth, n // MOD_COLS),
        in_specs=[
            pl.BlockSpec((rows, D_MODEL), lambda l, j: (0, 0)),
            pl.BlockSpec((1, D_MODEL, MOD_COLS), lambda l, j: (l, 0, j)),
            pl.BlockSpec((1, 1, MOD_COLS), lambda l, j: (l, 0, j)),
        ],
        out_specs=pl.BlockSpec((1, rows, MOD_COLS), lambda l, j: (l, 0, j)),
        compiler_params=_params("arbitrary", "arbitrary"),
        name="modulation",
    )(cc, w_mod, b_mod.reshape(depth, 1, n))


def _head_sumsq(x, ones_bd):
    sq = x * x
    hi = sq.astype(BF16)
    lo = (sq - hi.astype(F32)).astype(BF16)
    return _dot(hi, ones_bd) + _dot(lo, ones_bd)


def _norm_rope(x, gain, cos, sin, ones_bd, lane, scale):
    y = x * lax.rsqrt(_head_sumsq(x, ones_bd) * (1.0 / HD_ATT) + NORM_EPS) * gain
    above = pltpu.roll(y, LANES - 16, 1)
    below = pltpu.roll(y, 16, 1)
    partner = jnp.where((lane & 16) == 0, above, below)
    return (y * cos + partner * sin) * scale


def _proj_in_kernel(*refs, n_x, nb, cb):
    x_refs = refs[:n_x]
    (mod_ref, w_ref, b_ref, cos_ref, sin_ref, qg_ref, kg_ref,
     ret_ref, attq_ref, attk_ref, attvt_ref, mls_ref, gate_ref, merge_ref) = refs[n_x:]
    tm = x_refs[0].shape[0]
    x = _read_rows(x_refs, nb, cb)
    h = (_layer_norm(x) * (1.0 + mod_ref[1:2, :]) + mod_ref[0:1, :]).astype(BF16)

    def seg(lo, hi):
        return _dot(h, w_ref[:, lo:hi]) + b_ref[:, lo:hi]

    def store_qkvx(dst_ref, r, gate_fn):
        dst_ref[:, 0:256] = r[:, 0:256].astype(BF16)
        dst_ref[:, 256:512] = (r[:, 256:512] * 0.125).astype(BF16)
        dst_ref[:, 512:1024] = r[:, 512:1024].astype(BF16)
        dst_ref[:, 1024:1536] = gate_fn(r[:, 1024:1536]).astype(BF16)

    half = MERGE_W // 2
    a = seg(OFF_ATT, OFF_ATT + ATT_W)
    merge_ref[:, 0:half] = _sigmoid(seg(OFF_MERGE, OFF_MERGE + half)).astype(BF16)

    lane = lax.broadcasted_iota(jnp.int32, (tm, LANES), 1)
    ri = lax.broadcasted_iota(jnp.int32, (LANES, LANES), 0)
    ci = lax.broadcasted_iota(jnp.int32, (LANES, LANES), 1)
    ones_bd = jnp.where((ri // HD_ATT) == (ci // HD_ATT), 1.0, 0.0).astype(BF16)
    cos = cos_ref[...]
    sin = sin_ref[...]
    qg = qg_ref[...]
    for j in range(H_ATT * HD_ATT // LANES):
        qj = _norm_rope(a[:, j * LANES:(j + 1) * LANES], qg, cos, sin, ones_bd, lane,
                        HD_ATT ** -0.5 * LOG2_E)
        attq_ref[:, j * LANES:(j + 1) * LANES] = qj.astype(BF16)
    kn = _norm_rope(a[:, 512:640], kg_ref[...], cos, sin, ones_bd, lane, 1.0)
    kr = pltpu.roll(kn, HD_ATT, 1)
    low = lane < HD_ATT
    attk_ref[:, 0:128] = jnp.where(low, kn, kr).astype(BF16)
    attk_ref[:, 128:256] = jnp.where(low, kr, kn).astype(BF16)
    attvt_ref[0] = jnp.transpose(a[:, 640:768]).astype(BF16)

    merge_ref[:, half:MERGE_W] = _sigmoid(seg(OFF_MERGE + half, OFF_MERGE + MERGE_W)).astype(BF16)
    store_qkvx(ret_ref, seg(0, RET_W), lambda g: g * _sigmoid(g))
    store_qkvx(mls_ref, seg(OFF_MLS, OFF_MLS + MLS_W), _sigmoid)
    gate_ref[...] = seg(OFF_GATE, OFF_GATE + GATE_W)


def _row_maps(t_len, ctx_len, tm):
    nb = t_len // tm
    cb = ctx_len // tm

    def mod_map(i):
        return ((i // nb) * 2 + ((i % nb) >= cb).astype(jnp.int32), 0, 0)

    def pos_map(i):
        return (i % nb, 0)

    return nb, cb, mod_map, pos_map


def _row_inputs(xs, t_len, ctx_len, tm):
    nb, cb = t_len // tm, ctx_len // tm
    if not isinstance(xs, tuple):
        return [xs], [pl.BlockSpec((tm, D_MODEL), lambda i: (i, 0))], xs.shape[0]
    ctx_map = lambda i: ((i // nb) * cb + jnp.minimum(i % nb, cb - 1), 0)
    lat_map = lambda i: ((i // nb) * (nb - cb) + jnp.maximum(i % nb - cb, 0), 0)
    specs = [pl.BlockSpec((tm, D_MODEL), ctx_map), pl.BlockSpec((tm, D_MODEL), lat_map)]
    return list(xs), specs, xs[0].shape[0] + xs[1].shape[0]


def _read_rows(x_refs, nb, cb):
    if len(x_refs) == 1:
        return x_refs[0][...]
    return jnp.where(pl.program_id(0) % nb < cb, x_refs[0][...], x_refs[1][...])


def _proj_in(xs, mod_tab, w, b, cos_t, sin_t, qg, kg, t_len, ctx_len):
    tm = ROW_TILE
    nb, cb, mod_map, pos_map = _row_maps(t_len, ctx_len, tm)
    x_args, x_specs, rows = _row_inputs(xs, t_len, ctx_len, tm)
    row_spec = lambda n: pl.BlockSpec((tm, n), lambda i: (i, 0))
    out_shapes = [
        jax.ShapeDtypeStruct((rows, RET_W), BF16),
        jax.ShapeDtypeStruct((rows, H_ATT * HD_ATT), BF16),
        jax.ShapeDtypeStruct((rows, 2 * LANES), BF16),
        jax.ShapeDtypeStruct((rows // tm, KV_ATT * HD_ATT, tm), BF16),
        jax.ShapeDtypeStruct((rows, MLS_W), BF16),
        jax.ShapeDtypeStruct((rows, GATE_W), F32),
        jax.ShapeDtypeStruct((rows, MERGE_W), BF16),
    ]

    def out_spec(s):
        if len(s.shape) == 3:
            return pl.BlockSpec((1,) + s.shape[1:], lambda i: (i, 0, 0))
        return row_spec(s.shape[1])

    return pl.pallas_call(
        functools.partial(_proj_in_kernel, n_x=len(x_args), nb=nb, cb=cb),
        out_shape=out_shapes,
        grid=(rows // tm,),
        in_specs=x_specs + [
            pl.BlockSpec((None, 6, D_MODEL), mod_map),
            _const_spec((D_MODEL, N_PROJ)),
            _const_spec((1, N_PROJ)),
            pl.BlockSpec((tm, LANES), pos_map),
            pl.BlockSpec((tm, LANES), pos_map),
            _const_spec((1, LANES)),
            _const_spec((1, LANES)),
        ],
        out_specs=[out_spec(s) for s in out_shapes],
        compiler_params=_params("arbitrary"),
        name="proj_in",
    )(*x_args, mod_tab, w, b, cos_t, sin_t, qg, kg)


def _scan_chunk(d, j, n_ctx, n_chunks):
    if d == 0:
        return j
    return jnp.where(j < n_ctx, n_ctx - 1 - j, n_chunks + n_ctx - 1 - j)


def _run_scan_steps(body, n_ctx, n_chunks):
    for lo, hi in ((0, n_ctx), (n_ctx, n_chunks)):
        mid = lo + (hi - lo) // 2
        for start, stop, readout in ((lo, mid, False), (mid, hi, True)):
            if stop > start:
                lax.fori_loop(start, stop, functools.partial(body, readout=readout), 0)


def _head_norm(o):
    mu = jnp.mean(o, axis=-1, keepdims=True)
    oc = o - mu
    var = jnp.mean(oc * oc, axis=-1, keepdims=True)
    return oc * lax.rsqrt(var + NORM_EPS)


def _ret_kernel(p_ref, dl_ref, gn_ref, y_ref, o_scr, s_scr, intra_scr, qd_scr, kd_scr, cd_scr,
                *, n_ctx, n_chunks):
    lg = _log_sigmoid(dl_ref[...])
    cd_scr[...] = jnp.exp(float(CHUNK) * lg)
    ri = lax.broadcasted_iota(jnp.int32, (CHUNK, CHUNK), 0).astype(F32)
    ci = lax.broadcasted_iota(jnp.int32, (CHUNK, CHUNK), 1).astype(F32)
    for d in range(2):
        diff = ri - ci if d == 0 else ci - ri
        qpos = ri + 1.0 if d == 0 else float(CHUNK) - ri
        kpos = float(CHUNK - 1) - ri if d == 0 else ri
        keep = diff >= 0.0
        for h in range(H_RET):
            r = d * H_RET + h
            lgb = lg[r:r + 1, :]
            intra_scr[r] = jnp.where(keep, jnp.exp(jnp.where(keep, diff, 0.0) * lgb), 0.0)
            qd_scr[r] = jnp.exp(qpos * lgb)
            kd_scr[r] = jnp.exp(kpos * lgb)

    lane = lax.broadcasted_iota(jnp.int32, (CHUNK, LANES), 1)
    half_mask = (lane < DK_RET, lane >= DK_RET)

    s_scr[...] = jnp.zeros_like(s_scr)
    o_scr[...] = jnp.zeros_like(o_scr)

    def body(j, carry, readout):
        staged = []
        for d in range(2):
            c = _scan_chunk(d, j, n_ctx, n_chunks)
            rows = pl.ds(pl.multiple_of(c * CHUNK, CHUNK), CHUNK)
            for pair in range(H_RET // 2):
                qp = p_ref[rows, pair * LANES:(pair + 1) * LANES]
                kp = p_ref[rows, 256 + pair * LANES:256 + (pair + 1) * LANES]
                kpf = kp.astype(F32)
                for hh in range(2):
                    h = pair * 2 + hh
                    r = d * H_RET + h
                    qm = jnp.where(half_mask[hh], qp, jnp.zeros_like(qp))
                    v = p_ref[rows, 512 + h * DV_RET:512 + (h + 1) * DV_RET]
                    s = (_dot_nt(qm, kp) * intra_scr[r]).astype(BF16)
                    state = s_scr[r]
                    inter = _dot(qm, state.astype(BF16)) * qd_scr[r]
                    kdec = (kpf * kd_scr[r]).astype(BF16)
                    s_scr[r] = state * cd_scr[r:r + 1, :] + _dot_tn(kdec, v)
                    staged.append((rows, h, s, v, inter))
        for rows, h, s, v, inter in staged:
            cols = slice(h * DV_RET, (h + 1) * DV_RET)
            tot = o_scr[rows, cols] + (_dot(s, v) + inter)
            o_scr[rows, cols] = tot
            if readout:
                g = p_ref[rows, 1024 + h * DV_RET:1024 + (h + 1) * DV_RET].astype(F32)
                y = _head_norm(tot) * gn_ref[:, cols] * g
                y_ref[rows, cols] = y.astype(BF16)
        return carry

    _run_scan_steps(body, n_ctx, n_chunks)


def _retention(ret_p, dl, gn, batch, t_len, ctx_len):
    n_chunks = t_len // CHUNK
    kern = functools.partial(_ret_kernel, n_ctx=ctx_len // CHUNK, n_chunks=n_chunks)
    width = H_RET * DV_RET
    return pl.pallas_call(
        kern,
        out_shape=jax.ShapeDtypeStruct((batch * t_len, width), BF16),
        grid=(batch,),
        in_specs=[
            pl.BlockSpec((t_len, RET_W), lambda b: (b, 0)),
            _const_spec((2 * H_RET, LANES)),
            _const_spec((1, width)),
        ],
        out_specs=pl.BlockSpec((t_len, width), lambda b: (b, 0)),
        scratch_shapes=[
            pltpu.VMEM((t_len, width), F32),
            pltpu.VMEM((2 * H_RET, LANES, DV_RET), F32),
            pltpu.VMEM((2 * H_RET, CHUNK, CHUNK), F32),
            pltpu.VMEM((2 * H_RET, CHUNK, CHUNK), F32),
            pltpu.VMEM((2 * H_RET, CHUNK, CHUNK), F32),
            pltpu.VMEM((2 * H_RET, LANES), F32),
        ],
        compiler_params=_params("arbitrary"),
        name="retention",
    )(ret_p, dl, gn)


def _mlstm_kernel(p_ref, gate_ref, gn_ref, y_ref, h_scr, cn_scr, m_scr, *, n_ctx, n_chunks):
    ri = lax.broadcasted_iota(jnp.int32, (CHUNK, CHUNK), 0)
    ci = lax.broadcasted_iota(jnp.int32, (CHUNK, CHUNK), 1)
    lane = lax.broadcasted_iota(jnp.int32, (CHUNK, LANES), 1)
    half_mask = (lane < DK_M, lane >= DK_M)
    ones_v = jnp.ones((CHUNK, LANES), BF16)

    cn_scr[...] = jnp.zeros_like(cn_scr)
    m_scr[...] = jnp.zeros_like(m_scr)
    h_scr[...] = jnp.zeros_like(h_scr)
    keeps = (ri >= ci, ci >= ri)
    tris = tuple(jnp.where(k, 1.0, 0.0).astype(BF16) for k in keeps)
    full = (CHUNK, LANES)

    def body(j, carry, readout):
        staged = []
        for d in range(2):
            keep = keeps[d]
            end_row = CHUNK - 1 if d == 0 else 0
            c = _scan_chunk(d, j, n_ctx, n_chunks)
            rows = pl.ds(pl.multiple_of(c * CHUNK, CHUNK), CHUNK)
            gates = gate_ref[rows, :]
            hi, mid, lo = _split3(_log_sigmoid(gates))
            bcum = _dot(tris[d], hi) + _dot(tris[d], mid) + _dot(tris[d], lo)
            cdiff = gates - pltpu.roll(bcum, LANES - 2 * H_M, 1)
            cdiff_t = jnp.transpose(cdiff)
            for pair in range(H_M // 2):
                qp = p_ref[rows, pair * LANES:(pair + 1) * LANES]
                kp = p_ref[rows, 256 + pair * LANES:256 + (pair + 1) * LANES]
                kpf = kp.astype(F32)
                for hh in range(2):
                    h = pair * 2 + hh
                    r = d * H_M + h
                    gb = 2 * H_M + r
                    b_l = jnp.broadcast_to(bcum[:, gb:gb + 1], full)
                    c_l = jnp.broadcast_to(cdiff[:, r:r + 1], full)
                    c_row = cdiff_t[r:r + 1, :]
                    b_end = b_l[end_row:end_row + 1, :]
                    m_prev = m_scr[r][0:1, :]
                    qm = jnp.where(half_mask[hh], qp, jnp.zeros_like(qp))
                    v = p_ref[rows, 512 + h * DV_M:512 + (h + 1) * DV_M]
                    vext = jnp.concatenate([v, ones_v], axis=1)

                    dmat = jnp.where(keep, b_l + c_row, -jnp.inf)
                    inter = b_l + m_prev
                    row_max = jnp.broadcast_to(jnp.max(dmat, axis=-1, keepdims=True), full)
                    m_t = jnp.maximum(inter, row_max)
                    s = (_dot_nt(qm, kp) * jnp.exp(dmat - m_t)).astype(BF16)
                    a_int = jnp.exp(inter - m_t)
                    state = cn_scr[r]
                    qcn = _dot(qm, state.astype(BF16))

                    g_l = b_end + c_l
                    m_new = jnp.maximum(b_end + m_prev, jnp.max(g_l, axis=0, keepdims=True))
                    kdec = (kpf * jnp.exp(g_l - m_new)).astype(BF16)
                    carry_scale = jnp.exp(b_end + m_prev - m_new)
                    cn_scr[r] = (jnp.concatenate([carry_scale, carry_scale], axis=1) * state
                                 + _dot_tn(kdec, vext))
                    m_scr[r] = jnp.broadcast_to(m_new, (SUBLANES, LANES))
                    staged.append((rows, h, s, vext, a_int, qcn, m_t))
        for rows, h, s, vext, a_int, qcn, m_t in staged:
            cols = slice(h * DV_M, (h + 1) * DV_M)
            sv = _dot(s, vext)
            num = sv[:, 0:DV_M] + a_int * qcn[:, 0:DV_M]
            den = sv[:, DV_M:2 * DV_M] + a_int * qcn[:, DV_M:2 * DV_M]
            hval = num / jnp.maximum(jnp.abs(den), jnp.exp(-m_t))
            tot = h_scr[rows, cols] + hval
            h_scr[rows, cols] = tot
            if readout:
                og = p_ref[rows, 1024 + h * DV_M:1024 + (h + 1) * DV_M].astype(F32)
                y = og * (_head_norm(tot) * gn_ref[:, cols])
                y_ref[rows, cols] = y.astype(BF16)
        return carry

    _run_scan_steps(body, n_ctx, n_chunks)


def _mlstm(mls_p, gates, gn, batch, t_len, ctx_len):
    n_chunks = t_len // CHUNK
    kern = functools.partial(_mlstm_kernel, n_ctx=ctx_len // CHUNK, n_chunks=n_chunks)
    width = H_M * DV_M
    return pl.pallas_call(
        kern,
        out_shape=jax.ShapeDtypeStruct((batch * t_len, width), BF16),
        grid=(batch,),
        in_specs=[
            pl.BlockSpec((t_len, MLS_W), lambda b: (b, 0)),
            pl.BlockSpec((t_len, GATE_W), lambda b: (b, 0)),
            _const_spec((1, width)),
        ],
        out_specs=pl.BlockSpec((t_len, width), lambda b: (b, 0)),
        scratch_shapes=[
            pltpu.VMEM((t_len, width), F32),
            pltpu.VMEM((2 * H_M, LANES, 2 * LANES), F32),
            pltpu.VMEM((2 * H_M, SUBLANES, LANES), F32),
        ],
        compiler_params=_params("arbitrary"),
        name="mlstm",
    )(mls_p, gates, gn)


def _attn_kernel(bound_ref, q_ref, k_ref, vt_ref, o_ref, m_scr, alpha_scr, acc_scr, p_scr,
                 *, ctx_len, t_len, bounded):
    tq = q_ref.shape[0]
    tk = vt_ref.shape[2]
    rep = H_ATT // KV_ATT
    is_ctx = pl.program_id(1) < ctx_len // tq
    lane = lax.broadcasted_iota(jnp.int32, (Q_SUB, LANES), 1)
    low = lane < HD_ATT
    bound = bound_ref[0]

    chains = [(g, qb) for g in range(KV_ATT) for qb in range(tq // Q_SUB)]
    qs = []
    for g, qb in chains:
        parts = []
        for hh in range(rep):
            h = g * rep + hh
            qp = q_ref[qb * Q_SUB:(qb + 1) * Q_SUB, (h // 2) * LANES:(h // 2 + 1) * LANES]
            parts.append(jnp.where(low if h % 2 == 0 else ~low, qp, jnp.zeros_like(qp)))
        qs.append(jnp.concatenate(parts, axis=0))
    if not bounded:
        m_scr[...] = jnp.full_like(m_scr, -jnp.inf)
        alpha_scr[...] = jnp.ones_like(alpha_scr)
    acc_scr[...] = jnp.zeros_like(acc_scr)
    p_scr[:, tk:, :] = jnp.zeros((len(chains), (ATT_SLABS - 1) * tk, rep * Q_SUB), BF16)
    shift = 0.5 * bound

    def v_rows(g, first_slab, n_slabs):
        vt = [vt_ref[first_slab + i][g * HD_ATT:(g + 1) * HD_ATT, :] for i in range(n_slabs)]
        vt = vt[0] if n_slabs == 1 else jnp.concatenate(vt, axis=1)
        return jnp.concatenate([vt, jnp.ones((ATT_ONES_ROWS, n_slabs * tk), BF16)], axis=0)

    def probabilities(c, s_t):
        if bounded:
            return jnp.exp2((s_t - shift).astype(BF16))
        m_old = m_scr[c, 0:1, :]
        m_new = jnp.maximum(m_old, jnp.max(s_t, axis=0, keepdims=True))
        alpha_scr[c, 0:1, :] = jnp.exp2(m_old - m_new)
        m_scr[c, 0:1, :] = m_new
        return jnp.exp2(s_t - m_new).astype(BF16)

    def accumulate(c, pv):
        if bounded:
            acc_scr[c] += pv
        else:
            acc_scr[c] = alpha_scr[c, 0:1, :] * acc_scr[c] + pv

    for c, (g, _) in enumerate(chains):
        s_t = _dot_nt(k_ref[0:tk, g * LANES:(g + 1) * LANES], qs[c])
        p_scr[c, 0:tk] = probabilities(c, s_t)

    def pending_slab(step):
        return jnp.where(step == 0, 0, 1 + (step - 1) * ATT_SLABS)

    def body(t, carry):
        keys = pl.ds(pl.multiple_of(tk + t * (ATT_SLABS * tk), tk), ATT_SLABS * tk)
        scores = [_dot_nt(k_ref[keys, g * LANES:(g + 1) * LANES], qs[c])
                  for c, (g, _) in enumerate(chains)]
        for c, (g, _) in enumerate(chains):
            accumulate(c, _dot(v_rows(g, pending_slab(t), ATT_SLABS), p_scr[c]))
        for c, s_t in enumerate(scores):
            p_scr[c] = probabilities(c, s_t)
        return carry

    n_steps = jnp.where(is_ctx, 0, (t_len - tk) // (ATT_SLABS * tk))
    lax.fori_loop(0, n_steps, body, 0)
    for c, (g, _) in enumerate(chains):
        accumulate(c, _dot(v_rows(g, pending_slab(n_steps), ATT_SLABS), p_scr[c]))

    for c, (g, qb) in enumerate(chains):
        acc = acc_scr[c]
        out_t = acc[0:HD_ATT] / acc[HD_ATT:HD_ATT + 1]
        for pp in range(rep // 2):
            pair_t = jnp.concatenate([out_t[:, (2 * pp) * Q_SUB:(2 * pp + 1) * Q_SUB],
                                      out_t[:, (2 * pp + 1) * Q_SUB:(2 * pp + 2) * Q_SUB]], axis=0)
            col = (g * rep // 2 + pp) * LANES
            o_ref[qb * Q_SUB:(qb + 1) * Q_SUB, col:col + LANES] = jnp.transpose(pair_t).astype(BF16)


def _attention(att_q, att_k, att_vt, bound, batch, t_len, ctx_len, bounded):
    tq = Q_TILE
    nq = t_len // tq
    rep = H_ATT // KV_ATT
    slabs, vrows, tk = att_vt.shape
    per_batch = slabs // batch
    n_chains = KV_ATT * (tq // Q_SUB)
    kern = functools.partial(_attn_kernel, ctx_len=ctx_len, t_len=t_len, bounded=bounded)
    width = H_ATT * HD_ATT
    return pl.pallas_call(
        kern,
        out_shape=jax.ShapeDtypeStruct((batch * t_len, width), BF16),
        grid=(batch, nq),
        in_specs=[
            pl.BlockSpec(memory_space=pltpu.SMEM),
            pl.BlockSpec((tq, width), lambda b, i: (b * nq + i, 0)),
            pl.BlockSpec((t_len, 2 * LANES), lambda b, i: (b, 0)),
            pl.BlockSpec((per_batch, vrows, tk), lambda b, i: (b, 0, 0)),
        ],
        out_specs=pl.BlockSpec((tq, width), lambda b, i: (b * nq + i, 0)),
        scratch_shapes=[
            pltpu.VMEM((n_chains, SUBLANES, rep * Q_SUB), F32),
            pltpu.VMEM((n_chains, SUBLANES, rep * Q_SUB), F32),
            pltpu.VMEM((n_chains, HD_ATT + ATT_ONES_ROWS, rep * Q_SUB), F32),
            pltpu.VMEM((n_chains, ATT_SLABS * tk, rep * Q_SUB), BF16),
        ],
        compiler_params=_params("arbitrary", "arbitrary"),
        name="attention_bounded" if bounded else "attention_online",
    )(bound, att_q, att_k, att_vt)


def _merge_kernel(*refs, alpha, n_x, nb, cb):
    x_refs = refs[:n_x]
    (mod_ref, yr_ref, ya_ref, ym_ref, mg_ref, wr_ref, wa_ref, wm_ref, wo_ref,
     lng_ref, lnb_ref, o_ref) = refs[n_x:]

    def branch(y_ref, w_ref, k):
        gate = mg_ref[:, k * D_MODEL:(k + 1) * D_MODEL].astype(F32)
        return gate * _dot(y_ref[...], w_ref[...])

    z = branch(yr_ref, wr_ref, 0) + branch(ya_ref, wa_ref, 1) + branch(ym_ref, wm_ref, 2)
    mix = _dot(z.astype(BF16), wo_ref[...])
    y = alpha * _read_rows(x_refs, nb, cb) + mod_ref[2:3, :] * mix
    o_ref[...] = _layer_norm(y) * lng_ref[...] + lnb_ref[...]


def _merge(xs, mod_tab, y_ret, y_att, y_m, mg, w_r, w_a, w_m, w_o, ln_g, ln_b, t_len, ctx_len, alpha):
    tm = ROW_TILE
    nb, cb, mod_map, _ = _row_maps(t_len, ctx_len, tm)
    x_args, x_specs, rows = _row_inputs(xs, t_len, ctx_len, tm)
    row_spec = lambda n: pl.BlockSpec((tm, n), lambda i: (i, 0))
    bw = y_ret.shape[1]
    return pl.pallas_call(
        functools.partial(_merge_kernel, alpha=alpha, n_x=len(x_args), nb=nb, cb=cb),
        out_shape=jax.ShapeDtypeStruct((rows, D_MODEL), F32),
        grid=(rows // tm,),
        in_specs=x_specs + [
            pl.BlockSpec((None, 6, D_MODEL), mod_map),
            row_spec(bw), row_spec(bw), row_spec(bw), row_spec(MERGE_W),
            _const_spec((bw, D_MODEL)), _const_spec((bw, D_MODEL)), _const_spec((bw, D_MODEL)),
            _const_spec((D_MODEL, D_MODEL)),
            _const_spec((1, D_MODEL)), _const_spec((1, D_MODEL)),
        ],
        out_specs=row_spec(D_MODEL),
        compiler_params=_params("arbitrary"),
        name="merge",
    )(*x_args, mod_tab, y_ret, y_att, y_m, mg, w_r, w_a, w_m, w_o, ln_g, ln_b)


def _ffn_kernel(x_ref, xp_ref, xn_ref, mod_ref, wu_ref, cw_ref, cb_ref, wd_ref, lng_ref, lnb_ref,
                o_ref, act_scr, *, alpha, nb, cb):
    tm = x_ref.shape[0]
    j = pl.program_id(0) % nb
    prev_ok = jnp.logical_and(j != 0, j != cb)
    next_ok = jnp.logical_and(j != cb - 1, j != nb - 1)
    shift = mod_ref[3:4, :]
    scale = 1.0 + mod_ref[4:5, :]

    def mod(xv):
        return _layer_norm(xv) * scale + shift

    x = x_ref[...]
    hp = jnp.where(prev_ok, mod(xp_ref[...]), 0.0)
    hn = jnp.where(next_ok, mod(xn_ref[...]), 0.0)
    hext = jnp.concatenate([hp, mod(x), hn], axis=0).astype(BF16)

    def conv(u, lo, hi):
        w = cw_ref[:, lo:hi]
        rows = u.shape[0]
        prev = pltpu.roll(u, 1, 0)[SUBLANES:SUBLANES + tm]
        nxt = pltpu.roll(u, rows - 1, 0)[SUBLANES:SUBLANES + tm]
        return (prev * w[0:1] + u[SUBLANES:SUBLANES + tm] * w[1:2] + nxt * w[2:3]
                + cb_ref[:, lo:hi])

    def up(c):
        lo, hi = c * FF_CHUNK, (c + 1) * FF_CHUNK
        return _dot(hext, wu_ref[:, lo:hi]), _dot(hext, wu_ref[:, D_FF + lo:D_FF + hi])

    n_chunks = D_FF // FF_CHUNK
    acc = None
    done = 0
    ua, ug = up(0)
    for c in range(n_chunks):
        nxt = up(c + 1) if c + 1 < n_chunks else None
        lo, hi = c * FF_CHUNK, (c + 1) * FF_CHUNK
        a = conv(ua, lo, hi)
        g = conv(ug, D_FF + lo, D_FF + hi)
        act_scr[:, lo:hi] = (g * _sigmoid(g) * a).astype(BF16)
        if (c + 1) % FF_DOWN_CHUNKS == 0 or c + 1 == n_chunks:
            part = _dot(act_scr[:, done:hi], wd_ref[done:hi, :])
            acc = part if acc is None else acc + part
            done = hi
        if nxt is not None:
            ua, ug = nxt
    y = alpha * x + mod_ref[5:6, :] * acc
    o_ref[...] = _layer_norm(y) * lng_ref[...] + lnb_ref[...]


def _ffn(xs, mod_tab, w_up, conv_w, conv_b, w_down, ln_g, ln_b, t_len, ctx_len, alpha, latent_only):
    rows = xs.shape[0]
    tm = ROW_TILE
    nb, cb, mod_map, _ = _row_maps(t_len, ctx_len, tm)
    per = tm // SUBLANES
    last = rows // SUBLANES - 1
    if latent_only:
        out_rows = rows // nb * (nb - cb)
        out_map = lambda i: ((i // nb) * (nb - cb) + jnp.maximum(i % nb - cb, 0), 0)
    else:
        out_rows = rows
        out_map = lambda i: (i, 0)
    return pl.pallas_call(
        functools.partial(_ffn_kernel, alpha=alpha, nb=nb, cb=cb),
        out_shape=jax.ShapeDtypeStruct((out_rows, D_MODEL), F32),
        grid=(rows // tm,),
        in_specs=[
            pl.BlockSpec((tm, D_MODEL), lambda i: (i, 0)),
            pl.BlockSpec((SUBLANES, D_MODEL), lambda i: (jnp.maximum(i * per - 1, 0), 0)),
            pl.BlockSpec((SUBLANES, D_MODEL), lambda i: (jnp.minimum((i + 1) * per, last), 0)),
            pl.BlockSpec((None, 6, D_MODEL), mod_map),
            _const_spec((D_MODEL, 2 * D_FF)),
            _const_spec((3, 2 * D_FF)),
            _const_spec((1, 2 * D_FF)),
            _const_spec((D_FF, D_MODEL)),
            _const_spec((1, D_MODEL)), _const_spec((1, D_MODEL)),
        ],
        out_specs=pl.BlockSpec((tm, D_MODEL), out_map),
        scratch_shapes=[pltpu.VMEM((tm, D_FF), BF16)],
        compiler_params=_params("arbitrary"),
        name="conv_ffn",
    )(xs, xs, xs, mod_tab, w_up, conv_w, conv_b, w_down, ln_g, ln_b)


def _rope_tables(seq, ctx_len):
    f32 = np.float32
    rows = seq // GRID_W
    row = np.repeat(np.arange(rows), GRID_W).astype(f32)
    col = np.tile(np.arange(GRID_W), rows).astype(f32)
    n_freq = HD_ATT // 4
    freqs = np.power(f32(ROPE_THETA), -np.arange(n_freq, dtype=f32) / f32(n_freq)).astype(f32)
    ar = row[:, None] * freqs[None, :]
    ac = col[:, None] * freqs[None, :]
    cos = np.concatenate([np.cos(ar), np.cos(ar), np.cos(ac), np.cos(ac)], axis=-1)
    sin = np.concatenate([-np.sin(ar), np.sin(ar), -np.sin(ac), np.sin(ac)], axis=-1)
    cos = np.concatenate([np.ones((ctx_len, HD_ATT), f32), cos], axis=0)
    sin = np.concatenate([np.zeros((ctx_len, HD_ATT), f32), sin], axis=0)
    return jnp.asarray(np.tile(cos, (1, 2)), F32), jnp.asarray(np.tile(sin, (1, 2)), F32)


def kernel(x, c, ctx, c_ctx, w_mod, b_mod, w_in, b_in, ret_decay_logit, ret_gn_g, attn_qn_g, attn_kn_g,
           mlstm_gn_g, w_br_ret, w_br_att, w_br_mlstm, w_out, ln1_g, ln1_b, w_up, conv_w, conv_b, w_down,
           ln2_g, ln2_b):
    batch, seq, _ = x.shape
    ctx_len = ctx.shape[1]
    depth = w_mod.shape[0]
    t_len = ctx_len + seq
    alpha = (2.0 * depth) ** 0.25
    assert ctx_len == ROW_TILE and seq % (ATT_SLABS * ROW_TILE) == 0
    assert seq % GRID_W == 0 and batch + 1 <= SUBLANES

    xs = (ctx.reshape(batch * ctx_len, D_MODEL), x.reshape(batch * seq, D_MODEL))

    cc = jnp.zeros((SUBLANES, D_MODEL), F32).at[:batch].set(c).at[batch].set(c_ctx)
    mod = _modulation(cc, w_mod, b_mod)
    mod_lat = mod[:, :batch].reshape(depth, batch, 1, 6, D_MODEL)
    mod_ctx = jnp.broadcast_to(mod[:, batch].reshape(depth, 1, 1, 6, D_MODEL),
                               (depth, batch, 1, 6, D_MODEL))
    mod_tab = jnp.concatenate([mod_ctx, mod_lat], axis=2).reshape(depth, batch * 2, 6, D_MODEL)

    cos_t, sin_t = _rope_tables(seq, ctx_len)
    n_gate = 4 * H_M
    split = OFF_GATE + n_gate

    def pad_proj(a, dtype=F32):
        pad = jnp.zeros(a.shape[:-1] + (GATE_W - n_gate,), dtype)
        return jnp.concatenate([a[..., :split].astype(dtype), pad, a[..., split:].astype(dtype)],
                               axis=-1)

    b_in_p = pad_proj(b_in).reshape(depth, 1, N_PROJ)
    dl = jnp.broadcast_to(ret_decay_logit.reshape(depth, 2 * H_RET, 1), (depth, 2 * H_RET, LANES))
    qg = jnp.tile(attn_qn_g, (1, 2)).reshape(depth, 1, LANES)
    kg = jnp.tile(attn_kn_g, (1, 2)).reshape(depth, 1, LANES)
    row2 = lambda a, l: a[l].reshape(1, -1)
    cast = lambda w, l: w[l].astype(BF16)

    for l in range(depth):
        ret_p, att_q, att_k, att_vt, mls_p, gates, mg = _proj_in(
            xs, mod_tab[l], pad_proj(w_in[l], BF16), b_in_p[l], cos_t, sin_t, qg[l], kg[l],
            t_len, ctx_len)
        y_ret = _retention(ret_p, dl[l], row2(ret_gn_g, l), batch, t_len, ctx_len)
        bound = (1.01 * LOG2_E * HD_ATT ** 0.5 * jnp.max(jnp.abs(attn_qn_g[l]))
                 * jnp.max(jnp.abs(attn_kn_g[l]))).reshape(1)
        attend = lambda fixed: functools.partial(
            _attention, batch=batch, t_len=t_len, ctx_len=ctx_len, bounded=fixed)
        y_att = lax.cond(bound[0] <= ATT_SAFE_BOUND, attend(True), attend(False),
                         att_q, att_k, att_vt, bound)
        y_m = _mlstm(mls_p, gates, row2(mlstm_gn_g, l), batch, t_len, ctx_len)
        xs = _merge(xs, mod_tab[l], y_ret, y_att, y_m, mg, cast(w_br_ret, l), cast(w_br_att, l),
                    cast(w_br_mlstm, l), cast(w_out, l), row2(ln1_g, l), row2(ln1_b, l),
                    t_len, ctx_len, alpha)
        xs = _ffn(xs, mod_tab[l], cast(w_up, l), conv_w[l], row2(conv_b, l), cast(w_down, l),
                  row2(ln2_g, l), row2(ln2_b, l), t_len, ctx_len, alpha,
                  latent_only=l == depth - 1)

    return xs.reshape(batch, seq, D_MODEL)
```

```python
import functools

import jax
import jax.numpy as jnp
import numpy as np
from jax import lax
from jax.experimental import pallas as pl
from jax.experimental.pallas import tpu as pltpu

F32 = jnp.float32
BF16 = jnp.bfloat16

D_MODEL = 1024
H_RET, DK_RET, DV_RET = 4, 64, 128
H_ATT, KV_ATT, HD_ATT = 8, 2, 64
H_M, DK_M, DV_M = 4, 64, 128
CHUNK = 128
D_FF = 2816
GRID_W = 64
ROPE_THETA = 10000.0
NORM_EPS = 1e-6
LOG2_E = 1.4426950408889634

LANES = 128
SUBLANES = 8
ROW_TILE = 256
Q_TILE = 256
Q_SUB = 128
ATT_SAFE_BOUND = 40.0
ATT_SLABS = 2
ATT_ONES_ROWS = 16
FF_CHUNK = 256
FF_DOWN_CHUNKS = 6
MOD_COLS = 1536
VMEM_LIMIT = 56 * 1024 * 1024

RET_W = 2 * H_RET * DK_RET + 2 * H_RET * DV_RET
ATT_W = (H_ATT + 2 * KV_ATT) * HD_ATT
MLS_W = 2 * H_M * DK_M + 2 * H_M * DV_M
GATE_W = LANES
MERGE_W = 3 * D_MODEL
OFF_ATT = RET_W
OFF_MLS = OFF_ATT + ATT_W
OFF_GATE = OFF_MLS + MLS_W
OFF_MERGE = OFF_GATE + GATE_W
N_PROJ = OFF_MERGE + MERGE_W


def _dot(a, b):
    return jnp.dot(a, b, preferred_element_type=F32)


def _dot_nt(a, b):
    return lax.dot_general(a, b, (((1,), (1,)), ((), ())), preferred_element_type=F32)


def _dot_tn(a, b):
    return lax.dot_general(a, b, (((0,), (0,)), ((), ())), preferred_element_type=F32)


def _layer_norm(x):
    mu = jnp.mean(x, axis=-1, keepdims=True)
    xc = x - mu
    var = jnp.mean(xc * xc, axis=-1, keepdims=True)
    return xc * lax.rsqrt(var + NORM_EPS)


def _log_sigmoid(x):
    return jnp.minimum(x, 0.0) - jnp.log1p(jnp.exp(-jnp.abs(x)))


def _sigmoid(x):
    return 1.0 / (1.0 + jnp.exp(-x))


def _split3(x):
    hi = x.astype(BF16)
    r1 = x - hi.astype(F32)
    mid = r1.astype(BF16)
    lo = (r1 - mid.astype(F32)).astype(BF16)
    return hi, mid, lo


def _const_spec(shape):
    zeros = (0,) * len(shape)
    return pl.BlockSpec(shape, lambda *_: zeros, pipeline_mode=pl.Buffered(1))


def _params(*sem):
    return pltpu.CompilerParams(dimension_semantics=sem, vmem_limit_bytes=VMEM_LIMIT)


def _mod_kernel(c_ref, w_ref, b_ref, o_ref):
    c = c_ref[...]
    s = (c * _sigmoid(c)).astype(BF16)
    o_ref[0] = _dot(s, w_ref[0].astype(BF16)) + b_ref[0]


def _modulation(cc, w_mod, b_mod):
    depth, _, n = w_mod.shape
    rows = cc.shape[0]
    return pl.pallas_call(
        _mod_kernel,
        out_shape=jax.ShapeDtypeStruct((depth, rows, n), F32),
        grid=(depth, n // MOD_COLS),
        in_specs=[
            pl.BlockSpec((rows, D_MODEL), lambda l, j: (0, 0)),
            pl.BlockSpec((1, D_MODEL, MOD_COLS), lambda l, j: (l, 0, j)),
            pl.BlockSpec((1, 1, MOD_COLS), lambda l, j: (l, 0, j)),
        ],
        out_specs=pl.BlockSpec((1, rows, MOD_COLS), lambda l, j: (l, 0, j)),
        compiler_params=_params("arbitrary", "arbitrary"),
        name="modulation",
    )(cc, w_mod, b_mod.reshape(depth, 1, n))


def _head_sumsq(x, ones_bd):
    sq = x * x
    hi = sq.astype(BF16)
    lo = (sq - hi.astype(F32)).astype(BF16)
    return _dot(hi, ones_bd) + _dot(lo, ones_bd)


def _norm_rope(x, gain, cos, sin, ones_bd, lane, scale):
    y = x * lax.rsqrt(_head_sumsq(x, ones_bd) * (1.0 / HD_ATT) + NORM_EPS) * gain
    above = pltpu.roll(y, LANES - 16, 1)
    below = pltpu.roll(y, 16, 1)
    partner = jnp.where((lane & 16) == 0, above, below)
    return (y * cos + partner * sin) * scale


def _proj_in_kernel(*refs, n_x, nb, cb):
    x_refs = refs[:n_x]
    (mod_ref, w_ref, b_ref, cos_ref, sin_ref, qg_ref, kg_ref,
     ret_ref, attq_ref, attk_ref, attvt_ref, mls_ref, gate_ref, merge_ref) = refs[n_x:]
    tm = x_refs[0].shape[0]
    x = _read_rows(x_refs, nb, cb)
    h = (_layer_norm(x) * (1.0 + mod_ref[1:2, :]) + mod_ref[0:1, :]).astype(BF16)

    def seg(lo, hi):
        return _dot(h, w_ref[:, lo:hi]) + b_ref[:, lo:hi]

    def store_qkvx(dst_ref, r, gate_fn):
        dst_ref[:, 0:256] = r[:, 0:256].astype(BF16)
        dst_ref[:, 256:512] = (r[:, 256:512] * 0.125).astype(BF16)
        dst_ref[:, 512:1024] = r[:, 512:1024].astype(BF16)
        dst_ref[:, 1024:1536] = gate_fn(r[:, 1024:1536]).astype(BF16)

    half = MERGE_W // 2
    a = seg(OFF_ATT, OFF_ATT + ATT_W)
    merge_ref[:, 0:half] = _sigmoid(seg(OFF_MERGE, OFF_MERGE + half)).astype(BF16)

    lane = lax.broadcasted_iota(jnp.int32, (tm, LANES), 1)
    ri = lax.broadcasted_iota(jnp.int32, (LANES, LANES), 0)
    ci = lax.broadcasted_iota(jnp.int32, (LANES, LANES), 1)
    ones_bd = jnp.where((ri // HD_ATT) == (ci // HD_ATT), 1.0, 0.0).astype(BF16)
    cos = cos_ref[...]
    sin = sin_ref[...]
    qg = qg_ref[...]
    for j in range(H_ATT * HD_ATT // LANES):
        qj = _norm_rope(a[:, j * LANES:(j + 1) * LANES], qg, cos, sin, ones_bd, lane,
                        HD_ATT ** -0.5 * LOG2_E)
        attq_ref[:, j * LANES:(j + 1) * LANES] = qj.astype(BF16)
    kn = _norm_rope(a[:, 512:640], kg_ref[...], cos, sin, ones_bd, lane, 1.0)
    kr = pltpu.roll(kn, HD_ATT, 1)
    low = lane < HD_ATT
    attk_ref[:, 0:128] = jnp.where(low, kn, kr).astype(BF16)
    attk_ref[:, 128:256] = jnp.where(low, kr, kn).astype(BF16)
    attvt_ref[0] = jnp.transpose(a[:, 640:768]).astype(BF16)

    merge_ref[:, half:MERGE_W] = _sigmoid(seg(OFF_MERGE + half, OFF_MERGE + MERGE_W)).astype(BF16)
    store_qkvx(ret_ref, seg(0, RET_W), lambda g: g * _sigmoid(g))
    store_qkvx(mls_ref, seg(OFF_MLS, OFF_MLS + MLS_W), _sigmoid)
    gate_ref[...] = seg(OFF_GATE, OFF_GATE + GATE_W)


def _row_maps(t_len, ctx_len, tm):
    nb = t_len // tm
    cb = ctx_len // tm

    def mod_map(i):
        return ((i // nb) * 2 + ((i % nb) >= cb).astype(jnp.int32), 0, 0)

    def pos_map(i):
        return (i % nb, 0)

    return nb, cb, mod_map, pos_map


def _row_inputs(xs, t_len, ctx_len, tm):
    nb, cb = t_len // tm, ctx_len // tm
    if not isinstance(xs, tuple):
        return [xs], [pl.BlockSpec((tm, D_MODEL), lambda i: (i, 0))], xs.shape[0]
    ctx_map = lambda i: ((i // nb) * cb + jnp.minimum(i % nb, cb - 1), 0)
    lat_map = lambda i: ((i // nb) * (nb - cb) + jnp.maximum(i % nb - cb, 0), 0)
    specs = [pl.BlockSpec((tm, D_MODEL), ctx_map), pl.BlockSpec((tm, D_MODEL), lat_map)]
    return list(xs), specs, xs[0].shape[0] + xs[1].shape[0]


def _read_rows(x_refs, nb, cb):
    if len(x_refs) == 1:
        return x_refs[0][...]
    return jnp.where(pl.program_id(0) % nb < cb, x_refs[0][...], x_refs[1][...])


def _proj_in(xs, mod_tab, w, b, cos_t, sin_t, qg, kg, t_len, ctx_len):
    tm = ROW_TILE
    nb, cb, mod_map, pos_map = _row_maps(t_len, ctx_len, tm)
    x_args, x_specs, rows = _row_inputs(xs, t_len, ctx_len, tm)
    row_spec = lambda n: pl.BlockSpec((tm, n), lambda i: (i, 0))
    out_shapes = [
        jax.ShapeDtypeStruct((rows, RET_W), BF16),
        jax.ShapeDtypeStruct((rows, H_ATT * HD_ATT), BF16),
        jax.ShapeDtypeStruct((rows, 2 * LANES), BF16),
        jax.ShapeDtypeStruct((rows // tm, KV_ATT * HD_ATT, tm), BF16),
        jax.ShapeDtypeStruct((rows, MLS_W), BF16),
        jax.ShapeDtypeStruct((rows, GATE_W), F32),
        jax.ShapeDtypeStruct((rows, MERGE_W), BF16),
    ]

    def out_spec(s):
        if len(s.shape) == 3:
            return pl.BlockSpec((1,) + s.shape[1:], lambda i: (i, 0, 0))
        return row_spec(s.shape[1])

    return pl.pallas_call(
        functools.partial(_proj_in_kernel, n_x=len(x_args), nb=nb, cb=cb),
        out_shape=out_shapes,
        grid=(rows // tm,),
        in_specs=x_specs + [
            pl.BlockSpec((None, 6, D_MODEL), mod_map),
            _const_spec((D_MODEL, N_PROJ)),
            _const_spec((1, N_PROJ)),
            pl.BlockSpec((tm, LANES), pos_map),
            pl.BlockSpec((tm, LANES), pos_map),
            _const_spec((1, LANES)),
            _const_spec((1, LANES)),
        ],
        out_specs=[out_spec(s) for s in out_shapes],
        compiler_params=_params("arbitrary"),
        name="proj_in",
    )(*x_args, mod_tab, w, b, cos_t, sin_t, qg, kg)


def _scan_chunk(d, j, n_ctx, n_chunks):
    if d == 0:
        return j
    return jnp.where(j < n_ctx, n_ctx - 1 - j, n_chunks + n_ctx - 1 - j)


def _run_scan_steps(body, n_ctx, n_chunks):
    assert n_ctx % 2 == 0 and (n_chunks - n_ctx) % 2 == 0
    for lo, hi in ((0, n_ctx), (n_ctx, n_chunks)):
        mid = lo + (hi - lo) // 2
        for start, stop, readout in ((lo, mid, False), (mid, hi, True)):
            if stop > start:
                lax.fori_loop(start, stop, functools.partial(body, readout=readout), 0)


def _head_norm(o):
    mu = jnp.mean(o, axis=-1, keepdims=True)
    oc = o - mu
    var = jnp.mean(oc * oc, axis=-1, keepdims=True)
    return oc * lax.rsqrt(var + NORM_EPS)


def _ret_kernel(p_ref, dl_ref, gn_ref, y_ref, o_scr, s_scr, intra_scr, qd_scr, kd_scr, cd_scr,
                *, n_ctx, n_chunks):
    lg = _log_sigmoid(dl_ref[...])
    cd_scr[...] = jnp.exp(float(CHUNK) * lg)
    ri = lax.broadcasted_iota(jnp.int32, (CHUNK, CHUNK), 0).astype(F32)
    ci = lax.broadcasted_iota(jnp.int32, (CHUNK, CHUNK), 1).astype(F32)
    for d in range(2):
        diff = ri - ci if d == 0 else ci - ri
        qpos = ri + 1.0 if d == 0 else float(CHUNK) - ri
        kpos = float(CHUNK - 1) - ri if d == 0 else ri
        keep = diff >= 0.0
        for h in range(H_RET):
            r = d * H_RET + h
            lgb = lg[r:r + 1, :]
            intra_scr[r] = jnp.where(keep, jnp.exp(jnp.where(keep, diff, 0.0) * lgb), 0.0)
            qd_scr[r] = jnp.exp(qpos * lgb)
            kd_scr[r] = jnp.exp(kpos * lgb)

    lane = lax.broadcasted_iota(jnp.int32, (CHUNK, LANES), 1)
    half_mask = (lane < DK_RET, lane >= DK_RET)

    s_scr[...] = jnp.zeros_like(s_scr)

    def body(j, carry, readout):
        staged = []
        for d in range(2):
            c = _scan_chunk(d, j, n_ctx, n_chunks)
            rows = pl.ds(pl.multiple_of(c * CHUNK, CHUNK), CHUNK)
            for pair in range(H_RET // 2):
                qp = p_ref[rows, pair * LANES:(pair + 1) * LANES]
                kp = p_ref[rows, 256 + pair * LANES:256 + (pair + 1) * LANES]
                kpf = kp.astype(F32)
                for hh in range(2):
                    h = pair * 2 + hh
                    r = d * H_RET + h
                    qm = jnp.where(half_mask[hh], qp, jnp.zeros_like(qp))
                    v = p_ref[rows, 512 + h * DV_RET:512 + (h + 1) * DV_RET]
                    s = (_dot_nt(qm, kp) * intra_scr[r]).astype(BF16)
                    state = s_scr[r]
                    inter = _dot(qm, state.astype(BF16)) * qd_scr[r]
                    kdec = (kpf * kd_scr[r]).astype(BF16)
                    s_scr[r] = state * cd_scr[r:r + 1, :] + _dot_tn(kdec, v)
                    staged.append((rows, h, s, v, inter))
        for rows, h, s, v, inter in staged:
            cols = slice(h * DV_RET, (h + 1) * DV_RET)
            o = _dot(s, v) + inter
            if not readout:
                o_scr[rows, cols] = o
            else:
                tot = o_scr[rows, cols] + o
                g = p_ref[rows, 1024 + h * DV_RET:1024 + (h + 1) * DV_RET].astype(F32)
                y = _head_norm(tot) * gn_ref[:, cols] * g
                y_ref[rows, cols] = y.astype(BF16)
        return carry

    _run_scan_steps(body, n_ctx, n_chunks)


def _retention(ret_p, dl, gn, batch, t_len, ctx_len):
    n_chunks = t_len // CHUNK
    kern = functools.partial(_ret_kernel, n_ctx=ctx_len // CHUNK, n_chunks=n_chunks)
    width = H_RET * DV_RET
    return pl.pallas_call(
        kern,
        out_shape=jax.ShapeDtypeStruct((batch * t_len, width), BF16),
        grid=(batch,),
        in_specs=[
            pl.BlockSpec((t_len, RET_W), lambda b: (b, 0)),
            _const_spec((2 * H_RET, LANES)),
            _const_spec((1, width)),
        ],
        out_specs=pl.BlockSpec((t_len, width), lambda b: (b, 0)),
        scratch_shapes=[
            pltpu.VMEM((t_len, width), F32),
            pltpu.VMEM((2 * H_RET, LANES, DV_RET), F32),
            pltpu.VMEM((2 * H_RET, CHUNK, CHUNK), F32),
            pltpu.VMEM((2 * H_RET, CHUNK, CHUNK), F32),
            pltpu.VMEM((2 * H_RET, CHUNK, CHUNK), F32),
            pltpu.VMEM((2 * H_RET, LANES), F32),
        ],
        compiler_params=_params("arbitrary"),
        name="retention",
    )(ret_p, dl, gn)


def _mlstm_kernel(p_ref, gate_ref, gn_ref, y_ref, h_scr, cn_scr, m_scr, *, n_ctx, n_chunks):
    ri = lax.broadcasted_iota(jnp.int32, (CHUNK, CHUNK), 0)
    ci = lax.broadcasted_iota(jnp.int32, (CHUNK, CHUNK), 1)
    lane = lax.broadcasted_iota(jnp.int32, (CHUNK, LANES), 1)
    half_mask = (lane < DK_M, lane >= DK_M)
    ones_v = jnp.ones((CHUNK, LANES), BF16)

    cn_scr[...] = jnp.zeros_like(cn_scr)
    m_scr[...] = jnp.zeros_like(m_scr)
    keeps = (ri >= ci, ci >= ri)
    tris = tuple(jnp.where(k, 1.0, 0.0).astype(BF16) for k in keeps)
    full = (CHUNK, LANES)

    def body(j, carry, readout):
        staged = []
        for d in range(2):
            keep = keeps[d]
            end_row = CHUNK - 1 if d == 0 else 0
            c = _scan_chunk(d, j, n_ctx, n_chunks)
            rows = pl.ds(pl.multiple_of(c * CHUNK, CHUNK), CHUNK)
            gates = gate_ref[rows, :]
            hi, mid, lo = _split3(_log_sigmoid(gates))
            bcum = _dot(tris[d], hi) + _dot(tris[d], mid) + _dot(tris[d], lo)
            cdiff = gates - pltpu.roll(bcum, LANES - 2 * H_M, 1)
            cdiff_t = jnp.transpose(cdiff)
            for pair in range(H_M // 2):
                qp = p_ref[rows, pair * LANES:(pair + 1) * LANES]
                kp = p_ref[rows, 256 + pair * LANES:256 + (pair + 1) * LANES]
                kpf = kp.astype(F32)
                for hh in range(2):
                    h = pair * 2 + hh
                    r = d * H_M + h
                    gb = 2 * H_M + r
                    b_l = jnp.broadcast_to(bcum[:, gb:gb + 1], full)
                    c_l = jnp.broadcast_to(cdiff[:, r:r + 1], full)
                    c_row = cdiff_t[r:r + 1, :]
                    b_end = b_l[end_row:end_row + 1, :]
                    m_prev = m_scr[r][0:1, :]
                    qm = jnp.where(half_mask[hh], qp, jnp.zeros_like(qp))
                    v = p_ref[rows, 512 + h * DV_M:512 + (h + 1) * DV_M]
                    vext = jnp.concatenate([v, ones_v], axis=1)

                    dmat = jnp.where(keep, b_l + c_row, -jnp.inf)
                    inter = b_l + m_prev
                    row_max = jnp.broadcast_to(jnp.max(dmat, axis=-1, keepdims=True), full)
                    m_t = jnp.maximum(inter, row_max)
                    s = (_dot_nt(qm, kp) * jnp.exp(dmat - m_t)).astype(BF16)
                    a_int = jnp.exp(inter - m_t)
                    state = cn_scr[r]
                    qcn = _dot(qm, state.astype(BF16))

                    g_l = b_end + c_l
                    m_new = jnp.maximum(b_end + m_prev, jnp.max(g_l, axis=0, keepdims=True))
                    kdec = (kpf * jnp.exp(g_l - m_new)).astype(BF16)
                    carry_scale = jnp.exp(b_end + m_prev - m_new)
                    cn_scr[r] = (jnp.concatenate([carry_scale, carry_scale], axis=1) * state
                                 + _dot_tn(kdec, vext))
                    m_scr[r] = jnp.broadcast_to(m_new, (SUBLANES, LANES))
                    staged.append((rows, h, s, vext, a_int, qcn, m_t))
        for rows, h, s, vext, a_int, qcn, m_t in staged:
            cols = slice(h * DV_M, (h + 1) * DV_M)
            sv = _dot(s, vext)
            num = sv[:, 0:DV_M] + a_int * qcn[:, 0:DV_M]
            den = sv[:, DV_M:2 * DV_M] + a_int * qcn[:, DV_M:2 * DV_M]
            hval = num / jnp.maximum(jnp.abs(den), jnp.exp(-m_t))
            if not readout:
                h_scr[rows, cols] = hval
            else:
                tot = h_scr[rows, cols] + hval
                og = p_ref[rows, 1024 + h * DV_M:1024 + (h + 1) * DV_M].astype(F32)
                y = og * (_head_norm(tot) * gn_ref[:, cols])
                y_ref[rows, cols] = y.astype(BF16)
        return carry

    _run_scan_steps(body, n_ctx, n_chunks)


def _mlstm(mls_p, gates, gn, batch, t_len, ctx_len):
    n_chunks = t_len // CHUNK
    kern = functools.partial(_mlstm_kernel, n_ctx=ctx_len // CHUNK, n_chunks=n_chunks)
    width = H_M * DV_M
    return pl.pallas_call(
        kern,
        out_shape=jax.ShapeDtypeStruct((batch * t_len, width), BF16),
        grid=(batch,),
        in_specs=[
            pl.BlockSpec((t_len, MLS_W), lambda b: (b, 0)),
            pl.BlockSpec((t_len, GATE_W), lambda b: (b, 0)),
            _const_spec((1, width)),
        ],
        out_specs=pl.BlockSpec((t_len, width), lambda b: (b, 0)),
        scratch_shapes=[
            pltpu.VMEM((t_len, width), F32),
            pltpu.VMEM((2 * H_M, LANES, 2 * LANES), F32),
            pltpu.VMEM((2 * H_M, SUBLANES, LANES), F32),
        ],
        compiler_params=_params("arbitrary"),
        name="mlstm",
    )(mls_p, gates, gn)


def _attn_kernel(bound_ref, q_ref, k_ref, vt_ref, o_ref, m_scr, alpha_scr, acc_scr, p_scr,
                 *, ctx_len, t_len, bounded):
    tq = q_ref.shape[0]
    tk = vt_ref.shape[2]
    rep = H_ATT // KV_ATT
    is_ctx = pl.program_id(1) < ctx_len // tq
    lane = lax.broadcasted_iota(jnp.int32, (Q_SUB, LANES), 1)
    low = lane < HD_ATT
    bound = bound_ref[0]

    chains = [(g, qb) for g in range(KV_ATT) for qb in range(tq // Q_SUB)]
    qs = []
    for g, qb in chains:
        parts = []
        for hh in range(rep):
            h = g * rep + hh
            qp = q_ref[qb * Q_SUB:(qb + 1) * Q_SUB, (h // 2) * LANES:(h // 2 + 1) * LANES]
            parts.append(jnp.where(low if h % 2 == 0 else ~low, qp, jnp.zeros_like(qp)))
        qs.append(jnp.concatenate(parts, axis=0))
    if not bounded:
        m_scr[...] = jnp.full_like(m_scr, -jnp.inf)
        alpha_scr[...] = jnp.ones_like(alpha_scr)
    acc_scr[...] = jnp.zeros_like(acc_scr)
    p_scr[:, tk:, :] = jnp.zeros((len(chains), (ATT_SLABS - 1) * tk, rep * Q_SUB), BF16)
    shift = 0.5 * bound

    def v_rows(g, first_slab, n_slabs):
        vt = [vt_ref[first_slab + i][g * HD_ATT:(g + 1) * HD_ATT, :] for i in range(n_slabs)]
        vt = vt[0] if n_slabs == 1 else jnp.concatenate(vt, axis=1)
        return jnp.concatenate([vt, jnp.ones((ATT_ONES_ROWS, n_slabs * tk), BF16)], axis=0)

    def probabilities(c, s_t):
        if bounded:
            return jnp.exp2((s_t - shift).astype(BF16))
        m_old = m_scr[c, 0:1, :]
        m_new = jnp.maximum(m_old, jnp.max(s_t, axis=0, keepdims=True))
        alpha_scr[c, 0:1, :] = jnp.exp2(m_old - m_new)
        m_scr[c, 0:1, :] = m_new
        return jnp.exp2(s_t - m_new).astype(BF16)

    def accumulate(c, pv):
        if bounded:
            acc_scr[c] += pv
        else:
            acc_scr[c] = alpha_scr[c, 0:1, :] * acc_scr[c] + pv

    for c, (g, _) in enumerate(chains):
        s_t = _dot_nt(k_ref[0:tk, g * LANES:(g + 1) * LANES], qs[c])
        p_scr[c, 0:tk] = probabilities(c, s_t)

    def pending_slab(step):
        return jnp.maximum(ATT_SLABS * step - 1, 0)

    def body(t, carry):
        keys = pl.ds(pl.multiple_of(tk + t * (ATT_SLABS * tk), tk), ATT_SLABS * tk)
        scores = [_dot_nt(k_ref[keys, g * LANES:(g + 1) * LANES], qs[c])
                  for c, (g, _) in enumerate(chains)]
        for c, (g, _) in enumerate(chains):
            accumulate(c, _dot(v_rows(g, pending_slab(t), ATT_SLABS), p_scr[c]))
        for c, s_t in enumerate(scores):
            p_scr[c] = probabilities(c, s_t)
        return carry

    n_steps = jnp.where(is_ctx, 0, (t_len - tk) // (ATT_SLABS * tk))
    lax.fori_loop(0, n_steps, body, 0)
    for c, (g, _) in enumerate(chains):
        accumulate(c, _dot(v_rows(g, pending_slab(n_steps), ATT_SLABS), p_scr[c]))

    for c, (g, qb) in enumerate(chains):
        acc = acc_scr[c]
        out_t = acc[0:HD_ATT] / acc[HD_ATT:HD_ATT + 1]
        for pp in range(rep // 2):
            pair_t = jnp.concatenate([out_t[:, (2 * pp) * Q_SUB:(2 * pp + 1) * Q_SUB],
                                      out_t[:, (2 * pp + 1) * Q_SUB:(2 * pp + 2) * Q_SUB]], axis=0)
            col = (g * rep // 2 + pp) * LANES
            o_ref[qb * Q_SUB:(qb + 1) * Q_SUB, col:col + LANES] = jnp.transpose(pair_t).astype(BF16)


def _attention(att_q, att_k, att_vt, bound, batch, t_len, ctx_len, bounded):
    tq = Q_TILE
    nq = t_len // tq
    rep = H_ATT // KV_ATT
    slabs, vrows, tk = att_vt.shape
    per_batch = slabs // batch
    n_chains = KV_ATT * (tq // Q_SUB)
    kern = functools.partial(_attn_kernel, ctx_len=ctx_len, t_len=t_len, bounded=bounded)
    width = H_ATT * HD_ATT
    return pl.pallas_call(
        kern,
        out_shape=jax.ShapeDtypeStruct((batch * t_len, width), BF16),
        grid=(batch, nq),
        in_specs=[
            pl.BlockSpec(memory_space=pltpu.SMEM),
            pl.BlockSpec((tq, width), lambda b, i: (b * nq + i, 0)),
            pl.BlockSpec((t_len, 2 * LANES), lambda b, i: (b, 0)),
            pl.BlockSpec((per_batch, vrows, tk), lambda b, i: (b, 0, 0)),
        ],
        out_specs=pl.BlockSpec((tq, width), lambda b, i: (b * nq + i, 0)),
        scratch_shapes=[
            pltpu.VMEM((n_chains, SUBLANES, rep * Q_SUB), F32),
            pltpu.VMEM((n_chains, SUBLANES, rep * Q_SUB), F32),
            pltpu.VMEM((n_chains, HD_ATT + ATT_ONES_ROWS, rep * Q_SUB), F32),
            pltpu.VMEM((n_chains, ATT_SLABS * tk, rep * Q_SUB), BF16),
        ],
        compiler_params=_params("arbitrary", "arbitrary"),
        name="attention_bounded" if bounded else "attention_online",
    )(bound, att_q, att_k, att_vt)


def _merge_kernel(*refs, alpha, n_x, nb, cb):
    x_refs = refs[:n_x]
    (mod_ref, yr_ref, ya_ref, ym_ref, mg_ref, wr_ref, wa_ref, wm_ref, wo_ref,
     lng_ref, lnb_ref, o_ref) = refs[n_x:]

    def branch(y_ref, w_ref, k):
        gate = mg_ref[:, k * D_MODEL:(k + 1) * D_MODEL].astype(F32)
        return gate * _dot(y_ref[...], w_ref[...])

    z = branch(yr_ref, wr_ref, 0) + branch(ya_ref, wa_ref, 1) + branch(ym_ref, wm_ref, 2)
    mix = _dot(z.astype(BF16), wo_ref[...])
    y = alpha * _read_rows(x_refs, nb, cb) + mod_ref[2:3, :] * mix
    o_ref[...] = _layer_norm(y) * lng_ref[...] + lnb_ref[...]


def _merge(xs, mod_tab, y_ret, y_att, y_m, mg, w_r, w_a, w_m, w_o, ln_g, ln_b, t_len, ctx_len, alpha):
    tm = ROW_TILE
    nb, cb, mod_map, _ = _row_maps(t_len, ctx_len, tm)
    x_args, x_specs, rows = _row_inputs(xs, t_len, ctx_len, tm)
    row_spec = lambda n: pl.BlockSpec((tm, n), lambda i: (i, 0))
    bw = y_ret.shape[1]
    return pl.pallas_call(
        functools.partial(_merge_kernel, alpha=alpha, n_x=len(x_args), nb=nb, cb=cb),
        out_shape=jax.ShapeDtypeStruct((rows, D_MODEL), F32),
        grid=(rows // tm,),
        in_specs=x_specs + [
            pl.BlockSpec((None, 6, D_MODEL), mod_map),
            row_spec(bw), row_spec(bw), row_spec(bw), row_spec(MERGE_W),
            _const_spec((bw, D_MODEL)), _const_spec((bw, D_MODEL)), _const_spec((bw, D_MODEL)),
            _const_spec((D_MODEL, D_MODEL)),
            _const_spec((1, D_MODEL)), _const_spec((1, D_MODEL)),
        ],
        out_specs=row_spec(D_MODEL),
        compiler_params=_params("arbitrary"),
        name="merge",
    )(*x_args, mod_tab, y_ret, y_att, y_m, mg, w_r, w_a, w_m, w_o, ln_g, ln_b)


def _ffn_kernel(x_ref, xp_ref, xn_ref, mod_ref, wu_ref, cw_ref, cb_ref, wd_ref, lng_ref, lnb_ref,
                o_ref, act_scr, *, alpha, nb, cb):
    tm = x_ref.shape[0]
    j = pl.program_id(0) % nb
    prev_ok = jnp.logical_and(j != 0, j != cb)
    next_ok = jnp.logical_and(j != cb - 1, j != nb - 1)
    shift = mod_ref[3:4, :]
    scale = 1.0 + mod_ref[4:5, :]

    def mod(xv):
        return _layer_norm(xv) * scale + shift

    x = x_ref[...]
    hp = jnp.where(prev_ok, mod(xp_ref[...]), 0.0)
    hn = jnp.where(next_ok, mod(xn_ref[...]), 0.0)
    hext = jnp.concatenate([hp, mod(x), hn], axis=0).astype(BF16)

    def conv(u, lo, hi):
        w = cw_ref[:, lo:hi]
        rows = u.shape[0]
        prev = pltpu.roll(u, 1, 0)[SUBLANES:SUBLANES + tm]
        nxt = pltpu.roll(u, rows - 1, 0)[SUBLANES:SUBLANES + tm]
        return (prev * w[0:1] + u[SUBLANES:SUBLANES + tm] * w[1:2] + nxt * w[2:3]
                + cb_ref[:, lo:hi])

    def up(c):
        lo, hi = c * FF_CHUNK, (c + 1) * FF_CHUNK
        return _dot(hext, wu_ref[:, lo:hi]), _dot(hext, wu_ref[:, D_FF + lo:D_FF + hi])

    n_chunks = D_FF // FF_CHUNK
    acc = None
    done = 0
    ua, ug = up(0)
    for c in range(n_chunks):
        nxt = up(c + 1) if c + 1 < n_chunks else None
        lo, hi = c * FF_CHUNK, (c + 1) * FF_CHUNK
        a = conv(ua, lo, hi)
        g = conv(ug, D_FF + lo, D_FF + hi)
        act_scr[:, lo:hi] = (g * _sigmoid(g) * a).astype(BF16)
        if (c + 1) % FF_DOWN_CHUNKS == 0 or c + 1 == n_chunks:
            part = _dot(act_scr[:, done:hi], wd_ref[done:hi, :])
            acc = part if acc is None else acc + part
            done = hi
        if nxt is not None:
            ua, ug = nxt
    y = alpha * x + mod_ref[5:6, :] * acc
    o_ref[...] = _layer_norm(y) * lng_ref[...] + lnb_ref[...]


def _ffn(xs, mod_tab, w_up, conv_w, conv_b, w_down, ln_g, ln_b, t_len, ctx_len, alpha, latent_only):
    rows = xs.shape[0]
    tm = ROW_TILE
    nb, cb, mod_map, _ = _row_maps(t_len, ctx_len, tm)
    per = tm // SUBLANES
    last = rows // SUBLANES - 1
    if latent_only:
        out_rows = rows // nb * (nb - cb)
        out_map = lambda i: ((i // nb) * (nb - cb) + jnp.maximum(i % nb - cb, 0), 0)
    else:
        out_rows = rows
        out_map = lambda i: (i, 0)
    return pl.pallas_call(
        functools.partial(_ffn_kernel, alpha=alpha, nb=nb, cb=cb),
        out_shape=jax.ShapeDtypeStruct((out_rows, D_MODEL), F32),
        grid=(rows // tm,),
        in_specs=[
            pl.BlockSpec((tm, D_MODEL), lambda i: (i, 0)),
            pl.BlockSpec((SUBLANES, D_MODEL), lambda i: (jnp.maximum(i * per - 1, 0), 0)),
            pl.BlockSpec((SUBLANES, D_MODEL), lambda i: (jnp.minimum((i + 1) * per, last), 0)),
            pl.BlockSpec((None, 6, D_MODEL), mod_map),
            _const_spec((D_MODEL, 2 * D_FF)),
            _const_spec((3, 2 * D_FF)),
            _const_spec((1, 2 * D_FF)),
            _const_spec((D_FF, D_MODEL)),
            _const_spec((1, D_MODEL)), _const_spec((1, D_MODEL)),
        ],
        out_specs=pl.BlockSpec((tm, D_MODEL), out_map),
        scratch_shapes=[pltpu.VMEM((tm, D_FF), BF16)],
        compiler_params=_params("arbitrary"),
        name="conv_ffn",
    )(xs, xs, xs, mod_tab, w_up, conv_w, conv_b, w_down, ln_g, ln_b)


def _rope_tables(seq, ctx_len):
    f32 = np.float32
    rows = seq // GRID_W
    row = np.repeat(np.arange(rows), GRID_W).astype(f32)
    col = np.tile(np.arange(GRID_W), rows).astype(f32)
    n_freq = HD_ATT // 4
    freqs = np.power(f32(ROPE_THETA), -np.arange(n_freq, dtype=f32) / f32(n_freq)).astype(f32)
    ar = row[:, None] * freqs[None, :]
    ac = col[:, None] * freqs[None, :]
    cos = np.concatenate([np.cos(ar), np.cos(ar), np.cos(ac), np.cos(ac)], axis=-1)
    sin = np.concatenate([-np.sin(ar), np.sin(ar), -np.sin(ac), np.sin(ac)], axis=-1)
    cos = np.concatenate([np.ones((ctx_len, HD_ATT), f32), cos], axis=0)
    sin = np.concatenate([np.zeros((ctx_len, HD_ATT), f32), sin], axis=0)
    return jnp.asarray(np.tile(cos, (1, 2)), F32), jnp.asarray(np.tile(sin, (1, 2)), F32)


def kernel(x, c, ctx, c_ctx, w_mod, b_mod, w_in, b_in, ret_decay_logit, ret_gn_g, attn_qn_g, attn_kn_g,
           mlstm_gn_g, w_br_ret, w_br_att, w_br_mlstm, w_out, ln1_g, ln1_b, w_up, conv_w, conv_b, w_down,
           ln2_g, ln2_b):
    batch, seq, _ = x.shape
    ctx_len = ctx.shape[1]
    depth = w_mod.shape[0]
    t_len = ctx_len + seq
    alpha = (2.0 * depth) ** 0.25
    assert ctx_len == ROW_TILE and seq % (ATT_SLABS * ROW_TILE) == 0
    assert seq % GRID_W == 0 and batch + 1 <= SUBLANES

    xs = (ctx.reshape(batch * ctx_len, D_MODEL), x.reshape(batch * seq, D_MODEL))

    cc = jnp.zeros((SUBLANES, D_MODEL), F32).at[:batch].set(c).at[batch].set(c_ctx)
    mod = _modulation(cc, w_mod, b_mod)
    mod_lat = mod[:, :batch].reshape(depth, batch, 1, 6, D_MODEL)
    mod_ctx = jnp.broadcast_to(mod[:, batch].reshape(depth, 1, 1, 6, D_MODEL),
                               (depth, batch, 1, 6, D_MODEL))
    mod_tab = jnp.concatenate([mod_ctx, mod_lat], axis=2).reshape(depth, batch * 2, 6, D_MODEL)

    cos_t, sin_t = _rope_tables(seq, ctx_len)
    n_gate = 4 * H_M
    split = OFF_GATE + n_gate

    def pad_proj(a, dtype=F32):
        pad = jnp.zeros(a.shape[:-1] + (GATE_W - n_gate,), dtype)
        return jnp.concatenate([a[..., :split].astype(dtype), pad, a[..., split:].astype(dtype)],
                               axis=-1)

    b_in_p = pad_proj(b_in).reshape(depth, 1, N_PROJ)
    dl = jnp.broadcast_to(ret_decay_logit.reshape(depth, 2 * H_RET, 1), (depth, 2 * H_RET, LANES))
    qg = jnp.tile(attn_qn_g, (1, 2)).reshape(depth, 1, LANES)
    kg = jnp.tile(attn_kn_g, (1, 2)).reshape(depth, 1, LANES)
    row2 = lambda a, l: a[l].reshape(1, -1)
    cast = lambda w, l: w[l].astype(BF16)

    for l in range(depth):
        ret_p, att_q, att_k, att_vt, mls_p, gates, mg = _proj_in(
            xs, mod_tab[l], pad_proj(w_in[l], BF16), b_in_p[l], cos_t, sin_t, qg[l], kg[l],
            t_len, ctx_len)
        y_ret = _retention(ret_p, dl[l], row2(ret_gn_g, l), batch, t_len, ctx_len)
        bound = (1.01 * LOG2_E * HD_ATT ** 0.5 * jnp.max(jnp.abs(attn_qn_g[l]))
                 * jnp.max(jnp.abs(attn_kn_g[l]))).reshape(1)
        attend = lambda fixed: functools.partial(
            _attention, batch=batch, t_len=t_len, ctx_len=ctx_len, bounded=fixed)
        y_att = lax.cond(bound[0] <= ATT_SAFE_BOUND, attend(True), attend(False),
                         att_q, att_k, att_vt, bound)
        y_m = _mlstm(mls_p, gates, row2(mlstm_gn_g, l), batch, t_len, ctx_len)
        xs = _merge(xs, mod_tab[l], y_ret, y_att, y_m, mg, cast(w_br_ret, l), cast(w_br_att, l),
                    cast(w_br_mlstm, l), cast(w_out, l), row2(ln1_g, l), row2(ln1_b, l),
                    t_len, ctx_len, alpha)
        xs = _ffn(xs, mod_tab[l], cast(w_up, l), conv_w[l], row2(conv_b, l), cast(w_down, l),
                  row2(ln2_g, l), row2(ln2_b, l), t_len, ctx_len, alpha,
                  latent_only=l == depth - 1)

    return xs.reshape(batch, seq, D_MODEL)
```
